```python
import jax, jax.numpy as jnp
from jax import lax
import numpy as np

D_MODEL = 2048
BATCH = 2
SEQ = 4096
DEPTH = 1

N_HEADS_MOBA = 8
HEAD_DIM_MOBA = 128
MOBA_BLOCK = 256
MOBA_TOPK = 3
MOBA_Q_CHUNK = 32
N_HEADS_RET = 8
HEAD_DIM_RET_QK = 128
HEAD_DIM_RET_V = 256
RET_CHUNK = 128
ROPE_BASE = 10000.0
N_GROUPS = 4
EXPERTS_PER_GROUP = 8
N_EXPERTS = N_GROUPS * EXPERTS_PER_GROUP
EXPERT_TOPK = 2
D_EXPERT = 512

RMS_EPS = 1e-6
GN_EPS = 1e-6
NEG_BIG = -1e30

W_MOBA = N_HEADS_MOBA * HEAD_DIM_MOBA
W_RET_QK = N_HEADS_RET * HEAD_DIM_RET_QK
W_RET_V = N_HEADS_RET * HEAD_DIM_RET_V
IN_SIZES = (W_MOBA, W_MOBA, W_MOBA, W_RET_QK, W_RET_QK, W_RET_V, W_RET_V, D_MODEL, D_MODEL)
N_IN = sum(IN_SIZES)

kernel_name = "hybrid_moba_retention_hmoe_block"


def rmsnorm(x, w):
    xf = x.astype(jnp.float32)
    xf = xf * lax.rsqrt(jnp.mean(xf * xf, axis=-1, keepdims=True) + RMS_EPS)
    return xf.astype(x.dtype) * w


def to_heads(t, n_heads):
    b, s, _ = t.shape
    return t.reshape(b, s, n_heads, -1).transpose(0, 2, 1, 3)


def moba_attention(q, k, v):
    B, H, S, hd = q.shape
    nb = -(-S // MOBA_BLOCK)
    pad = nb * MOBA_BLOCK - S
    kp = jnp.pad(k, ((0, 0), (0, 0), (0, pad), (0, 0)))
    vp = jnp.pad(v, ((0, 0), (0, 0), (0, pad), (0, 0)))
    kb = kp.reshape(B, H, nb, MOBA_BLOCK, hd)
    vb = vp.reshape(B, H, nb, MOBA_BLOCK, hd)
    scale = hd ** -0.5
    k_mean = jnp.mean(kb.astype(jnp.float32), axis=3)
    gate = jnp.einsum('bhsd,bhnd->bhsn', q.astype(jnp.float32), k_mean)
    q_block = jnp.arange(S) // MOBA_BLOCK
    past = jnp.arange(nb)[None, :] < q_block[:, None]
    gate = jnp.where(past, gate, -jnp.inf)
    n_sel = min(MOBA_TOPK, nb)
    _, sel = lax.top_k(gate, n_sel)
    valid = sel < q_block[None, None, :, None]

    nq = S // MOBA_Q_CHUNK
    qc = q.reshape(B, H, nq, MOBA_Q_CHUNK, hd).transpose(2, 0, 1, 3, 4)
    selc = sel.reshape(B, H, nq, MOBA_Q_CHUNK, n_sel).transpose(2, 0, 1, 3, 4)
    validc = valid.reshape(B, H, nq, MOBA_Q_CHUNK, n_sel).transpose(2, 0, 1, 3, 4)
    bi = jnp.arange(B)[:, None, None, None]
    hi = jnp.arange(H)[None, :, None, None]
    n_sel_keys = n_sel * MOBA_BLOCK

    def one_chunk(args):
        c, q_c, sel_c, valid_c = args
        start = c * MOBA_Q_CHUNK
        blk = start // MOBA_BLOCK
        q_pos = start + jnp.arange(MOBA_Q_CHUNK)
        k_sel = kb[bi, hi, sel_c]
        v_sel = vb[bi, hi, sel_c]
        s_sel = jnp.einsum('bhqd,bhqnkd->bhqnk', q_c, k_sel).astype(jnp.float32) * scale
        s_sel = jnp.where(valid_c[..., None], s_sel, NEG_BIG).reshape(B, H, MOBA_Q_CHUNK, n_sel_keys)
        k_own = lax.dynamic_slice_in_dim(kp, blk * MOBA_BLOCK, MOBA_BLOCK, axis=2)
        v_own = lax.dynamic_slice_in_dim(vp, blk * MOBA_BLOCK, MOBA_BLOCK, axis=2)
        s_own = jnp.einsum('bhqd,bhkd->bhqk', q_c, k_own).astype(jnp.float32) * scale
        k_pos = blk * MOBA_BLOCK + jnp.arange(MOBA_BLOCK)
        s_own = jnp.where(k_pos[None, :] <= q_pos[:, None], s_own, NEG_BIG)
        p = jax.nn.softmax(jnp.concatenate([s_sel, s_own], axis=-1), axis=-1)
        p_sel = p[..., :n_sel_keys].reshape(B, H, MOBA_Q_CHUNK, n_sel, MOBA_BLOCK)
        p_own = p[..., n_sel_keys:]
        out = (jnp.einsum('bhqnk,bhqnkd->bhqd', p_sel, v_sel.astype(jnp.float32))
               + jnp.einsum('bhqk,bhkd->bhqd', p_own, v_own.astype(jnp.float32)))
        return out.astype(q.dtype)

    out = lax.map(one_chunk, (jnp.arange(nq), qc, selc, validc))
    return out.transpose(1, 2, 0, 3, 4).reshape(B, H, S, hd)


def rotary(x, pos):
    half = x.shape[-1] // 2
    inv = ROPE_BASE ** (-jnp.arange(half, dtype=jnp.float32) / half)
    ang = pos.astype(jnp.float32)[:, None] * inv[None, :]
    cos, sin = jnp.cos(ang), jnp.sin(ang)
    x1 = x[..., :half].astype(jnp.float32)
    x2 = x[..., half:].astype(jnp.float32)
    return jnp.concatenate([x1 * cos - x2 * sin, x1 * sin + x2 * cos], axis=-1).astype(x.dtype)


def retention(q, k, v):
    B, H, S, dk = q.shape
    dv = v.shape[-1]
    C = RET_CHUNK
    nc = S // C
    log_gamma = jnp.log(1.0 - 2.0 ** (-5.0 - jnp.arange(H, dtype=jnp.float32)))
    idx = jnp.arange(C, dtype=jnp.float32)
    diff = idx[:, None] - idx[None, :]
    decay_in = jnp.where(diff >= 0, jnp.exp(log_gamma[:, None, None] * jnp.maximum(diff, 0.0)), 0.0)
    qc = q.astype(jnp.float32).reshape(B, H, nc, C, dk)
    kc = k.astype(jnp.float32).reshape(B, H, nc, C, dk)
    vc = v.astype(jnp.float32).reshape(B, H, nc, C, dv)
    scores = jnp.einsum('bhnid,bhnjd->bhnij', qc, kc) * decay_in[None, :, None]
    o_inner = jnp.einsum('bhnij,bhnjv->bhniv', scores, vc)
    zeta = jnp.exp(log_gamma[:, None] * (C - 1.0 - idx)[None, :])
    u = jnp.einsum('bhnjd,bhnjv->nbhdv', kc * zeta[None, :, None, :, None], vc)
    chunk_decay = jnp.exp(log_gamma * C)[None, :, None, None]

    def step(state, u_n):
        return chunk_decay * state + u_n, state

    _, r_prev = lax.scan(step, jnp.zeros((B, H, dk, dv), jnp.float32), u)
    xi = jnp.exp(log_gamma[:, None] * (idx + 1.0)[None, :])
    o_cross = jnp.einsum('bhnid,nbhdv->bhniv', qc * xi[None, :, None, :, None], r_prev)
    return (o_inner + o_cross).reshape(B, H, S, dv)


def head_group_norm(y):
    mu = jnp.mean(y, axis=-1, keepdims=True)
    var = jnp.mean(jnp.square(y - mu), axis=-1, keepdims=True)
    yn = (y - mu) * lax.rsqrt(var + GN_EPS)
    B, H, S, dv = y.shape
    return yn.transpose(0, 2, 1, 3).reshape(B, S, H * dv)


def hierarchical_moe(h, w_rg, b_rg, w_re, b_re, w_gate, w_up, w_down):
    hf = h.astype(jnp.float32)
    g_logits = hf @ w_rg.astype(jnp.float32) + b_rg.astype(jnp.float32)
    g_prob = jax.nn.softmax(g_logits, axis=-1)
    g_onehot = jax.nn.one_hot(jnp.argmax(g_logits, axis=-1), N_GROUPS, dtype=jnp.float32)
    g_weight = jnp.sum(g_prob * g_onehot, axis=-1, keepdims=True)
    e_logits_all = jnp.einsum('td,gde->tge', hf, w_re.astype(jnp.float32)) + b_re.astype(jnp.float32)
    e_logits = jnp.einsum('tge,tg->te', e_logits_all, g_onehot)
    e_prob = jax.nn.softmax(e_logits, axis=-1)
    top_p, top_i = lax.top_k(e_prob, EXPERT_TOPK)
    top_p = top_p / jnp.sum(top_p, axis=-1, keepdims=True)
    within = jnp.sum(jax.nn.one_hot(top_i, EXPERTS_PER_GROUP, dtype=jnp.float32) * top_p[..., None], axis=1)
    comb = ((g_weight * g_onehot)[:, :, None] * within[:, None, :]).reshape(h.shape[0], N_EXPERTS).astype(h.dtype)
    y = jnp.zeros_like(h)
    for e in range(N_EXPERTS):
        act = jax.nn.silu(h @ w_gate[e]) * (h @ w_up[e])
        y = y + comb[:, e:e + 1] * (act @ w_down[e])
    return y


def setup_inputs(seed: int = 0) -> dict:
    key = jax.random.key(seed)
    ks = jax.random.split(key, 16)
    f = jnp.float32
    n = jax.random.normal
    return {
        "x": n(ks[0], (BATCH, SEQ, D_MODEL), f),
        "norm_mix_w": 1.0 + 0.05 * n(ks[1], (DEPTH, D_MODEL), f),
        "w_in": n(ks[2], (DEPTH, D_MODEL, N_IN), f) * D_MODEL ** -0.5,
        "ret_gn_w": 1.0 + 0.05 * n(ks[3], (DEPTH, W_RET_V), f),
        "w_branch_moba": n(ks[4], (DEPTH, W_MOBA, D_MODEL), f) * W_MOBA ** -0.5,
        "w_branch_ret": n(ks[5], (DEPTH, W_RET_V, D_MODEL), f) * W_RET_V ** -0.5,
        "w_out": n(ks[6], (DEPTH, D_MODEL, D_MODEL), f) * D_MODEL ** -0.5,
        "norm_ffn_w": 1.0 + 0.05 * n(ks[7], (DEPTH, D_MODEL), f),
        "w_router_group": n(ks[8], (DEPTH, D_MODEL, N_GROUPS), f) * D_MODEL ** -0.5,
        "b_router_group": 0.01 * n(ks[9], (DEPTH, N_GROUPS), f),
        "w_router_expert": n(ks[10], (DEPTH, N_GROUPS, D_MODEL, EXPERTS_PER_GROUP), f) * D_MODEL ** -0.5,
        "b_router_expert": 0.01 * n(ks[11], (DEPTH, N_GROUPS, EXPERTS_PER_GROUP), f),
        "w_expert_gate": n(ks[12], (DEPTH, N_EXPERTS, D_MODEL, D_EXPERT), f) * D_MODEL ** -0.5,
        "w_expert_up": n(ks[13], (DEPTH, N_EXPERTS, D_MODEL, D_EXPERT), f) * D_MODEL ** -0.5,
        "w_expert_down": n(ks[14], (DEPTH, N_EXPERTS, D_EXPERT, D_MODEL), f) * D_EXPERT ** -0.5,
        "norm_final_w": 1.0 + 0.05 * n(ks[15], (D_MODEL,), f),
    }


def reference(x, norm_mix_w, w_in, ret_gn_w, w_branch_moba, w_branch_ret, w_out, norm_ffn_w,
              w_router_group, b_router_group, w_router_expert, b_router_expert,
              w_expert_gate, w_expert_up, w_expert_down, norm_final_w):
    B, S, D = x.shape
    pos = jnp.arange(S)
    split_points = np.cumsum(IN_SIZES)[:-1].tolist()
    for l in range(DEPTH):
        h = rmsnorm(x, norm_mix_w[l])
        proj = h @ w_in[l]
        mq, mk, mv, rq, rk, rv, rg, ga, gr = jnp.split(proj, split_points, axis=-1)
        o_a = moba_attention(to_heads(mq, N_HEADS_MOBA), to_heads(mk, N_HEADS_MOBA), to_heads(mv, N_HEADS_MOBA))
        o_a = o_a.transpose(0, 2, 1, 3).reshape(B, S, W_MOBA)
        q_r = rotary(to_heads(rq, N_HEADS_RET), pos)
        k_r = rotary(to_heads(rk, N_HEADS_RET), pos) * (HEAD_DIM_RET_QK ** -0.5)
        y_r = retention(q_r, k_r, to_heads(rv, N_HEADS_RET))
        y_r = head_group_norm(y_r).astype(x.dtype) * ret_gn_w[l]
        y_r = jax.nn.silu(rg) * y_r
        mix = jax.nn.sigmoid(ga) * (o_a @ w_branch_moba[l]) + jax.nn.sigmoid(gr) * (y_r @ w_branch_ret[l])
        x = x + mix @ w_out[l]
        h = rmsnorm(x, norm_ffn_w[l]).reshape(B * S, D)
        y = hierarchical_moe(h, w_router_group[l], b_router_group[l], w_router_expert[l], b_router_expert[l],
                             w_expert_gate[l], w_expert_up[l], w_expert_down[l])
        x = x + y.reshape(B, S, D)
    return rmsnorm(x, norm_final_w)
```

```python
import functools

import jax
import jax.numpy as jnp
from jax import lax
from jax.experimental import pallas as pl
from jax.experimental.pallas import tpu as pltpu

F32 = jnp.float32
BF16 = jnp.bfloat16

V7X_LANES = 128
V7X_VMEM_LIMIT_BYTES = 56 * 1024 * 1024

N_HEADS_MOBA = 8
HEAD_DIM_MOBA = 128
MOBA_BLOCK = 256
MOBA_TOPK = 3
N_HEADS_RET = 8
HEAD_DIM_RET_QK = 128
HEAD_DIM_RET_V = 256
ROPE_BASE = 10000.0
N_GROUPS = 4
EXPERTS_PER_GROUP = 8
N_EXPERTS = N_GROUPS * EXPERTS_PER_GROUP
RMS_EPS = 1e-6
GN_EPS = 1e-6
NEG_BIG = -1e30

RET_CHUNK = 256
INPROJ_TM, INPROJ_TN = 1024, 512
MERGE_TM, MERGE_TN = 512, 1024
OUTPROJ_TM = 512
MOE_TM = 256
COMBINE_TM = 256


def _params(*semantics):
    return pltpu.CompilerParams(dimension_semantics=semantics,
                                vmem_limit_bytes=V7X_VMEM_LIMIT_BYTES)


def _inproj_kernel(x_ref, nw_ref, w_ref, o_ref, h_ref):
    @pl.when(pl.program_id(1) == 0)
    def _():
        x = x_ref[...]
        ms = jnp.mean(x * x, axis=-1, keepdims=True)
        h_ref[...] = ((x * lax.rsqrt(ms + RMS_EPS)) * nw_ref[...]).astype(BF16)

    o_ref[...] = jnp.dot(h_ref[...], w_ref[...].astype(BF16),
                         preferred_element_type=F32).astype(o_ref.dtype)


def _inproj(x2, norm_w, w_in):
    t, d = x2.shape
    n = w_in.shape[1]
    tm, tn = min(INPROJ_TM, t), INPROJ_TN
    return pl.pallas_call(
        _inproj_kernel,
        grid=(t // tm, n // tn),
        in_specs=[pl.BlockSpec((tm, d), lambda i, j: (i, 0)),
                  pl.BlockSpec((1, d), lambda i, j: (0, 0)),
                  pl.BlockSpec((d, tn), lambda i, j: (0, j))],
        out_specs=pl.BlockSpec((tm, tn), lambda i, j: (i, j)),
        out_shape=jax.ShapeDtypeStruct((t, n), BF16),
        scratch_shapes=[pltpu.VMEM((tm, d), BF16)],
        compiler_params=_params("parallel", "arbitrary"),
        name="inproj",
    )(x2, norm_w.reshape(1, d), w_in)


def _moba_kernel(q_ref, k_ref, v_ref, o_ref, qt_ref, vt_ref, km_ref, sel_ref, *, nb):
    blk = MOBA_BLOCK
    i = pl.program_id(2)

    @pl.when(i == 0)
    def _():
        for c in range(nb):
            rows = slice(c * blk, (c + 1) * blk)
            qt_ref[:, rows] = q_ref[rows, :].astype(F32).T.astype(BF16)
            vt_ref[:, rows] = v_ref[rows, :].astype(F32).T.astype(BF16)
            km_ref[c:c + 1, :] = jnp.sum(k_ref[rows, :].astype(F32), axis=0, keepdims=True) * (1.0 / blk)

    q0 = pl.multiple_of(i * blk, blk)
    qt = qt_ref[:, pl.ds(q0, blk)]

    gate = jnp.dot(km_ref[...], qt.astype(F32), precision=lax.Precision.HIGHEST,
                   preferred_element_type=F32)
    row = lax.broadcasted_iota(jnp.int32, (nb, blk), 0)
    gate = jnp.where(row < i, gate, -jnp.inf)
    sel = jnp.zeros((nb, blk), F32)
    for _ in range(MOBA_TOPK):
        top = jnp.max(gate, axis=0, keepdims=True)
        is_top = (gate == top) & (top > -jnp.inf)
        first = jnp.min(jnp.where(is_top, row, nb), axis=0, keepdims=True)
        pick = row == first
        sel = jnp.where(pick, 1.0, sel)
        gate = jnp.where(pick, -jnp.inf, gate)
    sel_ref[...] = sel

    scale = HEAD_DIM_MOBA ** -0.5
    s = jnp.dot(k_ref[pl.ds(q0, blk), :], qt, preferred_element_type=F32) * scale
    kpos = lax.broadcasted_iota(jnp.int32, (blk, blk), 0)
    qpos = lax.broadcasted_iota(jnp.int32, (blk, blk), 1)
    s = jnp.where(kpos <= qpos, s, NEG_BIG)
    m0 = jnp.max(s, axis=0, keepdims=True)
    p = jnp.exp(s - m0)
    l0 = jnp.sum(p, axis=0, keepdims=True)
    acc0 = jnp.dot(vt_ref[:, pl.ds(q0, blk)], p.astype(BF16), preferred_element_type=F32)

    def past_block(j, carry):
        m, l, acc = carry
        k0 = pl.multiple_of(j * blk, blk)
        s = jnp.dot(k_ref[pl.ds(k0, blk), :], qt, preferred_element_type=F32) * scale
        s = jnp.where(sel_ref[pl.ds(j, 1), :] > 0.5, s, NEG_BIG)
        m_new = jnp.maximum(m, jnp.max(s, axis=0, keepdims=True))
        alpha = jnp.exp(m - m_new)
        p = jnp.exp(s - m_new)
        l = alpha * l + jnp.sum(p, axis=0, keepdims=True)
        acc = alpha * acc + jnp.dot(vt_ref[:, pl.ds(k0, blk)], p.astype(BF16), preferred_element_type=F32)
        return m_new, l, acc

    _, l, acc = lax.fori_loop(0, i, past_block, (m0, l0, acc0))
    o_ref[...] = (acc / l).T.astype(o_ref.dtype)


def _moba(proj3):
    b, s, _ = proj3.shape
    nb = s // MOBA_BLOCK
    hd, nh = HEAD_DIM_MOBA, N_HEADS_MOBA
    return pl.pallas_call(
        functools.partial(_moba_kernel, nb=nb),
        grid=(b, nh, nb),
        in_specs=[pl.BlockSpec((None, s, hd), lambda bi, h, i: (bi, 0, h)),
                  pl.BlockSpec((None, s, hd), lambda bi, h, i: (bi, 0, nh + h)),
                  pl.BlockSpec((None, s, hd), lambda bi, h, i: (bi, 0, 2 * nh + h))],
        out_specs=pl.BlockSpec((None, MOBA_BLOCK, hd), lambda bi, h, i: (bi, i, h)),
        out_shape=jax.ShapeDtypeStruct((b, s, nh * hd), BF16),
        scratch_shapes=[pltpu.VMEM((hd, s), BF16), pltpu.VMEM((hd, s), BF16),
                        pltpu.VMEM((nb, hd), F32), pltpu.VMEM((nb, MOBA_BLOCK), F32)],
        compiler_params=_params("parallel", "parallel", "arbitrary"),
        name="moba",
    )(proj3, proj3, proj3)


def _retention_kernel(lg_ref, q_ref, k_ref, v_ref, g_ref, cos_ref, sin_ref, gnw_ref, o_ref, *, n_chunks):
    c_len = RET_CHUNK
    dk, dv = HEAD_DIM_RET_QK, HEAD_DIM_RET_V
    lg = lg_ref[pl.program_id(1)]
    ri = lax.broadcasted_iota(jnp.int32, (c_len, c_len), 0).astype(F32)
    ci = lax.broadcasted_iota(jnp.int32, (c_len, c_len), 1).astype(F32)
    diff = ri - ci
    decay = jnp.where(diff >= 0, jnp.exp(lg * jnp.maximum(diff, 0.0)), 0.0)
    idx = lax.broadcasted_iota(jnp.int32, (c_len, 1), 0).astype(F32)
    xi = jnp.exp(lg * (idx + 1.0))
    zeta = jnp.exp(lg * (c_len - 1.0 - idx))
    chunk_decay = jnp.exp(jnp.zeros((1, dv), F32) + lg * c_len)
    k_scale = dk ** -0.5

    def chunk(c, state):
        r0 = pl.multiple_of(c * c_len, c_len)
        rows = pl.ds(r0, c_len)
        cs, sn = cos_ref[rows, :], sin_ref[rows, :]
        q = q_ref[rows, :].astype(F32)
        k = k_ref[rows, :].astype(F32)
        qr = q * cs + pltpu.roll(q, dk // 2, 1) * sn
        kr = (k * cs + pltpu.roll(k, dk // 2, 1) * sn) * k_scale
        v = v_ref[rows, :]
        qb = qr.astype(BF16)
        scores = lax.dot_general(qb, kr.astype(BF16), (((1,), (1,)), ((), ())),
                                 preferred_element_type=F32) * decay
        o = jnp.dot(scores.astype(BF16), v, preferred_element_type=F32)
        o = o + jnp.dot((qr * xi).astype(BF16), state.astype(BF16), preferred_element_type=F32)
        kz = (kr * zeta).T.astype(BF16)
        state = chunk_decay * state + jnp.dot(kz, v, preferred_element_type=F32)
        mu = jnp.mean(o, axis=-1, keepdims=True)
        var = jnp.mean(jnp.square(o - mu), axis=-1, keepdims=True)
        yn = (o - mu) * lax.rsqrt(var + GN_EPS)
        g = g_ref[rows, :].astype(F32)
        y = (g * jax.nn.sigmoid(g)) * (yn * gnw_ref[...])
        o_ref[rows, :] = y.astype(o_ref.dtype)
        return state

    lax.fori_loop(0, n_chunks, chunk, jnp.zeros((dk, dv), F32))


def _retention(proj3, cos_t, sin_t, gn_w, log_gamma):
    b, s, _ = proj3.shape
    nh, dk, dv = N_HEADS_RET, HEAD_DIM_RET_QK, HEAD_DIM_RET_V
    q_blk = 3 * N_HEADS_MOBA * HEAD_DIM_MOBA // dk
    k_blk = q_blk + nh
    v_blk = (k_blk + nh) * dk // dv
    g_blk = v_blk + nh
    grid_spec = pltpu.PrefetchScalarGridSpec(
        num_scalar_prefetch=1,
        grid=(b, nh),
        in_specs=[pl.BlockSpec((None, s, dk), lambda bi, h, lg: (bi, 0, q_blk + h)),
                  pl.BlockSpec((None, s, dk), lambda bi, h, lg: (bi, 0, k_blk + h)),
                  pl.BlockSpec((None, s, dv), lambda bi, h, lg: (bi, 0, v_blk + h)),
                  pl.BlockSpec((None, s, dv), lambda bi, h, lg: (bi, 0, g_blk + h)),
                  pl.BlockSpec((s, dk), lambda bi, h, lg: (0, 0)),
                  pl.BlockSpec((s, dk), lambda bi, h, lg: (0, 0)),
                  pl.BlockSpec((1, dv), lambda bi, h, lg: (0, h))],
        out_specs=pl.BlockSpec((None, s, dv), lambda bi, h, lg: (bi, 0, h)),
    )
    return pl.pallas_call(
        functools.partial(_retention_kernel, n_chunks=s // RET_CHUNK),
        grid_spec=grid_spec,
        out_shape=jax.ShapeDtypeStruct((b, s, nh * dv), BF16),
        compiler_params=_params("parallel", "parallel"),
        name="retention",
    )(log_gamma, proj3, proj3, proj3, proj3, cos_t, sin_t, gn_w.reshape(1, nh * dv))


def _merge_kernel(oa_ref, yr_ref, ga_ref, gr_ref, wa_ref, wr_ref, o_ref):
    a = jnp.dot(oa_ref[...], wa_ref[...], preferred_element_type=F32)
    r = jnp.dot(yr_ref[...], wr_ref[...], preferred_element_type=F32)
    mix = jax.nn.sigmoid(ga_ref[...].astype(F32)) * a + jax.nn.sigmoid(gr_ref[...].astype(F32)) * r
    o_ref[...] = mix.astype(o_ref.dtype)


def _merge(o_a, y_r, proj, w_a, w_r):
    t, wa_in = o_a.shape
    wr_in = y_r.shape[1]
    d = w_a.shape[1]
    tm, tn = min(MERGE_TM, t), MERGE_TN
    ga_blk = (proj.shape[1] - 2 * d) // tn
    gr_blk = ga_blk + d // tn
    return pl.pallas_call(
        _merge_kernel,
        grid=(d // tn, t // tm),
        in_specs=[pl.BlockSpec((tm, wa_in), lambda j, i: (i, 0)),
                  pl.BlockSpec((tm, wr_in), lambda j, i: (i, 0)),
                  pl.BlockSpec((tm, tn), lambda j, i: (i, ga_blk + j)),
                  pl.BlockSpec((tm, tn), lambda j, i: (i, gr_blk + j)),
                  pl.BlockSpec((wa_in, tn), lambda j, i: (0, j)),
                  pl.BlockSpec((wr_in, tn), lambda j, i: (0, j))],
        out_specs=pl.BlockSpec((tm, tn), lambda j, i: (i, j)),
        out_shape=jax.ShapeDtypeStruct((t, d), BF16),
        compiler_params=_params("parallel", "parallel"),
        name="merge",
    )(o_a, y_r, proj, proj, w_a, w_r)


def _outproj_kernel(x_ref, mix_ref, wo_ref, nw_ref, wr_ref, br_ref, x1_ref, h_ref, route_ref):
    x1 = x_ref[...] + jnp.dot(mix_ref[...], wo_ref[...], preferred_element_type=F32)
    x1_ref[...] = x1
    ms = jnp.mean(x1 * x1, axis=-1, keepdims=True)
    h = (x1 * lax.rsqrt(ms + RMS_EPS)) * nw_ref[...]
    h_ref[...] = h

    logits = jnp.dot(h, wr_ref[...], precision=lax.Precision.HIGHEST,
                     preferred_element_type=F32) + br_ref[...]
    lane = lax.broadcasted_iota(jnp.int32, logits.shape, 1)
    ninf = -jnp.inf
    gl = jnp.where(lane < N_GROUPS, logits, ninf)
    gmax = jnp.max(gl, axis=-1, keepdims=True)
    gsum = jnp.sum(jnp.exp(gl - gmax), axis=-1, keepdims=True)
    gidx = jnp.min(jnp.where(gl == gmax, lane, V7X_LANES), axis=-1, keepdims=True)
    g_weight = 1.0 / gsum
    lo = N_GROUPS + EXPERTS_PER_GROUP * gidx
    in_group = (lane >= lo) & (lane < lo + EXPERTS_PER_GROUP)
    el = jnp.where(in_group, logits, ninf)
    emax = jnp.max(el, axis=-1, keepdims=True)
    i1 = jnp.min(jnp.where(el == emax, lane, V7X_LANES), axis=-1, keepdims=True)
    el2 = jnp.where(lane == i1, ninf, el)
    emax2 = jnp.max(el2, axis=-1, keepdims=True)
    i2 = jnp.min(jnp.where(el2 == emax2, lane, V7X_LANES), axis=-1, keepdims=True)
    esum = jnp.sum(jnp.exp(el - emax), axis=-1, keepdims=True)
    p1 = 1.0 / esum
    p2 = jnp.exp(emax2 - emax) / esum
    c1 = g_weight * (p1 / (p1 + p2))
    c2 = g_weight * (p2 / (p1 + p2))
    e1 = (i1 - N_GROUPS).astype(F32)
    e2 = (i2 - N_GROUPS).astype(F32)
    route_ref[...] = jnp.where(lane == 0, c1, jnp.where(lane == 1, c2, jnp.where(lane == 2, e1, e2)))


def _outproj(x2, mix, w_out, norm_w, w_router, b_router):
    t, d = x2.shape
    tm = min(OUTPROJ_TM, t)
    row = lambda i: (i, 0)
    const = lambda i: (0, 0)
    return pl.pallas_call(
        _outproj_kernel,
        grid=(t // tm,),
        in_specs=[pl.BlockSpec((tm, d), row),
                  pl.BlockSpec((tm, d), row),
                  pl.BlockSpec((d, d), const),
                  pl.BlockSpec((1, d), const),
                  pl.BlockSpec((d, V7X_LANES), const),
                  pl.BlockSpec((1, V7X_LANES), const)],
        out_specs=[pl.BlockSpec((tm, d), row),
                   pl.BlockSpec((tm, d), row),
                   pl.BlockSpec((tm, V7X_LANES), row)],
        out_shape=[jax.ShapeDtypeStruct((t, d), F32),
                   jax.ShapeDtypeStruct((t, d), F32),
                   jax.ShapeDtypeStruct((t, V7X_LANES), F32)],
        compiler_params=_params("parallel"),
        name="outproj",
    )(x2, mix, w_out, norm_w.reshape(1, d), w_router, b_router)


def _experts_kernel(te_ref, nt_ref, rt_ref, h_hbm, rw_ref, wg_ref, wu_ref, wd_ref, y_ref, xbuf, sem):
    del te_ref
    i = pl.program_id(0)
    tm = xbuf.shape[0]

    @pl.when(i < nt_ref[0])
    def _():
        base = i * tm

        def row_copy(r):
            tok = rt_ref[base + r]
            return pltpu.make_async_copy(h_hbm.at[pl.ds(tok, 1), :], xbuf.at[pl.ds(r, 1), :], sem)

        def start(r, c):
            row_copy(r).start()
            return c

        def wait(r, c):
            row_copy(r).wait()
            return c

        lax.fori_loop(0, tm, start, 0)
        lax.fori_loop(0, tm, wait, 0)
        x = xbuf[...].astype(BF16)
        gate = jnp.dot(x, wg_ref[...].astype(BF16), preferred_element_type=F32)
        up = jnp.dot(x, wu_ref[...].astype(BF16), preferred_element_type=F32)
        act = (gate * jax.nn.sigmoid(gate)) * up
        y = jnp.dot(act.astype(BF16), wd_ref[...].astype(BF16), preferred_element_type=F32)
        y_ref[...] = rw_ref[...] * y

    @pl.when(i >= nt_ref[0])
    def _():
        y_ref[...] = jnp.zeros_like(y_ref)


def _experts(h, tile_expert, n_tiles, row_token, row_weight, w_gate, w_up, w_down):
    t, d = h.shape
    f = w_gate.shape[2]
    tm = MOE_TM
    max_tiles = tile_expert.shape[0]
    grid_spec = pltpu.PrefetchScalarGridSpec(
        num_scalar_prefetch=3,
        grid=(max_tiles,),
        in_specs=[pl.BlockSpec(memory_space=pl.ANY),
                  pl.BlockSpec((tm, 1), lambda i, te, nt, rt: (i, 0)),
                  pl.BlockSpec((None, d, f), lambda i, te, nt, rt: (te[i], 0, 0)),
                  pl.BlockSpec((None, d, f), lambda i, te, nt, rt: (te[i], 0, 0)),
                  pl.BlockSpec((None, f, d), lambda i, te, nt, rt: (te[i], 0, 0))],
        out_specs=pl.BlockSpec((tm, d), lambda i, te, nt, rt: (i, 0)),
        scratch_shapes=[pltpu.VMEM((tm, d), F32), pltpu.SemaphoreType.DMA],
    )
    return pl.pallas_call(
        _experts_kernel,
        grid_spec=grid_spec,
        out_shape=jax.ShapeDtypeStruct((max_tiles * tm, d), F32),
        compiler_params=_params("arbitrary"),
        name="experts",
    )(tile_expert, n_tiles, row_token, h, row_weight, w_gate, w_up, w_down)


def _combine_kernel(pos_ref, x_ref, y_hbm, nw_ref, o_ref, ybuf, sem, *, final_norm):
    i = pl.program_id(0)
    tm = x_ref.shape[0]
    base = i * (2 * tm)

    def row_copy(r, slot):
        p = pos_ref[base + 2 * r + slot]
        return pltpu.make_async_copy(y_hbm.at[pl.ds(p, 1), :], ybuf.at[slot, pl.ds(r, 1), :], sem)

    def start(r, c):
        row_copy(r, 0).start()
        row_copy(r, 1).start()
        return c

    def wait(r, c):
        row_copy(r, 0).wait()
        row_copy(r, 1).wait()
        return c

    lax.fori_loop(0, tm, start, 0)
    lax.fori_loop(0, tm, wait, 0)
    x = x_ref[...] + (ybuf[0] + ybuf[1])
    if final_norm:
        ms = jnp.mean(x * x, axis=-1, keepdims=True)
        x = (x * lax.rsqrt(ms + RMS_EPS)) * nw_ref[...]
    o_ref[...] = x


def _combine(x1, y_sorted, pos, norm_w, final_norm):
    t, d = x1.shape
    tm = min(COMBINE_TM, t)
    grid_spec = pltpu.PrefetchScalarGridSpec(
        num_scalar_prefetch=1,
        grid=(t // tm,),
        in_specs=[pl.BlockSpec((tm, d), lambda i, pos: (i, 0)),
                  pl.BlockSpec(memory_space=pl.ANY),
                  pl.BlockSpec((1, d), lambda i, pos: (0, 0))],
        out_specs=pl.BlockSpec((tm, d), lambda i, pos: (i, 0)),
        scratch_shapes=[pltpu.VMEM((2, tm, d), F32), pltpu.SemaphoreType.DMA],
    )
    return pl.pallas_call(
        functools.partial(_combine_kernel, final_norm=final_norm),
        grid_spec=grid_spec,
        out_shape=jax.ShapeDtypeStruct((t, d), F32),
        compiler_params=_params("arbitrary"),
        name="combine",
    )(pos, x1, y_sorted, norm_w.reshape(1, d))


def _rope_tables(s):
    half = HEAD_DIM_RET_QK // 2
    inv = ROPE_BASE ** (-jnp.arange(half, dtype=F32) / half)
    ang = jnp.arange(s).astype(F32)[:, None] * inv[None, :]
    cos, sin = jnp.cos(ang), jnp.sin(ang)
    return jnp.concatenate([cos, cos], axis=-1), jnp.concatenate([-sin, sin], axis=-1)


def _dispatch_plan(route, tm):
    t = route.shape[0]
    n_assign = 2 * t
    max_tiles = n_assign // tm + N_EXPERTS
    expert = route[:, 2:4].astype(jnp.int32).reshape(n_assign)
    weight = route[:, 0:2].reshape(n_assign)
    onehot = (expert[:, None] == jnp.arange(N_EXPERTS, dtype=jnp.int32)[None, :]).astype(jnp.int32)
    running = jnp.cumsum(onehot, axis=0)
    counts = running[-1]
    rank = jnp.sum((running - onehot) * onehot, axis=1)
    tiles_per = (counts + tm - 1) // tm
    tile_end = jnp.cumsum(tiles_per)
    row_start = (tile_end - tiles_per) * tm
    pos = row_start[expert] + rank
    n_tiles = tile_end[-1]
    tile_id = jnp.arange(max_tiles, dtype=jnp.int32)
    tile_expert = jnp.searchsorted(tile_end, jnp.minimum(tile_id, n_tiles - 1), side="right").astype(jnp.int32)
    token = jnp.arange(n_assign, dtype=jnp.int32) // 2
    row_token = jnp.zeros((max_tiles * tm,), jnp.int32).at[pos].set(token)
    row_weight = jnp.zeros((max_tiles * tm,), F32).at[pos].set(weight)
    return (tile_expert, n_tiles.reshape(1).astype(jnp.int32), row_token,
            row_weight.reshape(max_tiles * tm, 1), pos.astype(jnp.int32))


def kernel(x, norm_mix_w, w_in, ret_gn_w, w_branch_moba, w_branch_ret, w_out, norm_ffn_w, w_router_group, b_router_group, w_router_expert, b_router_expert, w_expert_gate, w_expert_up, w_expert_down, norm_final_w):
    b, s, d = x.shape
    t = b * s
    depth = w_in.shape[0]
    assert s % MOBA_BLOCK == 0 and s % RET_CHUNK == 0 and t % MOE_TM == 0
    cos_t, sin_t = _rope_tables(s)
    log_gamma = jnp.log(1.0 - 2.0 ** (-5.0 - jnp.arange(N_HEADS_RET, dtype=F32)))
    x2 = x.reshape(t, d)
    for l in range(depth):
        proj = _inproj(x2, norm_mix_w[l], w_in[l])
        proj3 = proj.reshape(b, s, proj.shape[1])
        o_a = _moba(proj3).reshape(t, -1)
        y_r = _retention(proj3, cos_t, sin_t, ret_gn_w[l], log_gamma).reshape(t, -1)
        mix = _merge(o_a, y_r, proj, w_branch_moba[l].astype(BF16), w_branch_ret[l].astype(BF16))
        w_router = jnp.concatenate(
            [w_router_group[l], jnp.transpose(w_router_expert[l], (1, 0, 2)).reshape(d, N_EXPERTS)], axis=1)
        b_router = jnp.concatenate([b_router_group[l], b_router_expert[l].reshape(N_EXPERTS)])
        pad = V7X_LANES - w_router.shape[1]
        w_router = jnp.pad(w_router, ((0, 0), (0, pad)))
        b_router = jnp.pad(b_router, (0, pad)).reshape(1, V7X_LANES)
        x1, h, route = _outproj(x2, mix, w_out[l].astype(BF16), norm_ffn_w[l], w_router, b_router)
        tile_expert, n_tiles, row_token, row_weight, pos = _dispatch_plan(route, MOE_TM)
        y_sorted = _experts(h, tile_expert, n_tiles, row_token, row_weight,
                            w_expert_gate[l], w_expert_up[l], w_expert_down[l])
        x2 = _combine(x1, y_sorted, pos, norm_final_w, final_norm=(l == depth - 1))
    return x2.reshape(b, s, d)
```

```python
import functools

import jax
import jax.numpy as jnp
from jax import lax
from jax.experimental import pallas as pl
from jax.experimental.pallas import tpu as pltpu

F32 = jnp.float32
BF16 = jnp.bfloat16

V7X_LANES = 128
V7X_VMEM_LIMIT_BYTES = 56 * 1024 * 1024

N_HEADS_MOBA = 8
HEAD_DIM_MOBA = 128
MOBA_BLOCK = 256
MOBA_TOPK = 3
N_HEADS_RET = 8
HEAD_DIM_RET_QK = 128
HEAD_DIM_RET_V = 256
ROPE_BASE = 10000.0
N_GROUPS = 4
EXPERTS_PER_GROUP = 8
N_EXPERTS = N_GROUPS * EXPERTS_PER_GROUP
RMS_EPS = 1e-6
GN_EPS = 1e-6
NEG_BIG = -1e30

MOBA_HEADS_PER_STEP = 4
RET_CHUNK = 256
INPROJ_TM, INPROJ_TN = 1024, 512
MERGE_TM, MERGE_TN = 512, 1024
OUTPROJ_TM = 512
MOE_TM = 256
COMBINE_TM = 256


def _params(*semantics):
    return pltpu.CompilerParams(dimension_semantics=semantics,
                                vmem_limit_bytes=V7X_VMEM_LIMIT_BYTES)


def _inproj_kernel(x_ref, nw_ref, w_ref, o_ref, h_ref):
    @pl.when(pl.program_id(1) == 0)
    def _():
        x = x_ref[...]
        ms = jnp.mean(x * x, axis=-1, keepdims=True)
        h_ref[...] = ((x * lax.rsqrt(ms + RMS_EPS)) * nw_ref[...]).astype(BF16)

    o_ref[...] = jnp.dot(h_ref[...], w_ref[...].astype(BF16),
                         preferred_element_type=F32).astype(o_ref.dtype)


def _inproj(x2, norm_w, w_in):
    t, d = x2.shape
    n = w_in.shape[1]
    tm, tn = min(INPROJ_TM, t), INPROJ_TN
    return pl.pallas_call(
        _inproj_kernel,
        grid=(t // tm, n // tn),
        in_specs=[pl.BlockSpec((tm, d), lambda i, j: (i, 0)),
                  pl.BlockSpec((1, d), lambda i, j: (0, 0)),
                  pl.BlockSpec((d, tn), lambda i, j: (0, j))],
        out_specs=pl.BlockSpec((tm, tn), lambda i, j: (i, j)),
        out_shape=jax.ShapeDtypeStruct((t, n), BF16),
        scratch_shapes=[pltpu.VMEM((tm, d), BF16)],
        compiler_params=_params("parallel", "arbitrary"),
        name="inproj",
    )(x2, norm_w.reshape(1, d), w_in)


def _moba_kernel(q_ref, k_ref, v_ref, o_ref, qt_ref, vt_ref, km_ref, sel_ref, acc_ref, *, nb, hb):
    blk, hd = MOBA_BLOCK, HEAD_DIM_MOBA
    i = pl.program_id(2)

    @pl.when(i == 0)
    def _():
        for hh in range(hb):
            cols = slice(hh * hd, (hh + 1) * hd)
            for c in range(nb):
                rows = slice(c * blk, (c + 1) * blk)
                qt_ref[hh, :, rows] = q_ref[rows, cols].astype(F32).T.astype(BF16)
                vt_ref[hh, :, rows] = v_ref[rows, cols].astype(F32).T.astype(BF16)
                km_ref[hh, c:c + 1, :] = (jnp.sum(k_ref[rows, cols].astype(F32), axis=0, keepdims=True)
                                          * (1.0 / blk))

    q0 = pl.multiple_of(i * blk, blk)
    scale = hd ** -0.5
    row = lax.broadcasted_iota(jnp.int32, (nb, blk), 0)
    kpos = lax.broadcasted_iota(jnp.int32, (blk, blk), 0)
    qpos = lax.broadcasted_iota(jnp.int32, (blk, blk), 1)

    qts = [qt_ref[hh, :, pl.ds(q0, blk)] for hh in range(hb)]
    gates = [jnp.dot(km_ref[hh], qts[hh].astype(F32), precision=lax.Precision.HIGHEST,
                     preferred_element_type=F32) for hh in range(hb)]
    own = [jnp.dot(k_ref[pl.ds(q0, blk), hh * hd:(hh + 1) * hd], qts[hh],
                   preferred_element_type=F32) for hh in range(hb)]
    init, probs = [], []
    for hh in range(hb):
        gate = jnp.where(row < i, gates[hh], -jnp.inf)
        sel = jnp.zeros((nb, blk), F32)
        for _ in range(MOBA_TOPK):
            top = jnp.max(gate, axis=0, keepdims=True)
            is_top = (gate == top) & (top > -jnp.inf)
            first = jnp.min(jnp.where(is_top, row, nb), axis=0, keepdims=True)
            pick = row == first
            sel = jnp.where(pick, 1.0, sel)
            gate = jnp.where(pick, -jnp.inf, gate)
        sel_ref[hh] = sel
        s = jnp.where(kpos <= qpos, own[hh] * scale, NEG_BIG)
        m0 = jnp.max(s, axis=0, keepdims=True)
        p = jnp.exp(s - m0)
        init.append((m0, jnp.sum(p, axis=0, keepdims=True)))
        probs.append(p.astype(BF16))
    for hh in range(hb):
        acc_ref[hh] = jnp.dot(vt_ref[hh, :, pl.ds(q0, blk)], probs[hh], preferred_element_type=F32)

    def past_block(j, carry):
        k0 = pl.multiple_of(j * blk, blk)
        scores = [jnp.dot(k_ref[pl.ds(k0, blk), hh * hd:(hh + 1) * hd], qt_ref[hh, :, pl.ds(q0, blk)],
                          preferred_element_type=F32) for hh in range(hb)]
        probs, out = [], []
        for hh in range(hb):
            m, l = carry[hh]
            s = jnp.where(sel_ref[hh, pl.ds(j, 1), :] > 0.5, scores[hh] * scale, NEG_BIG)
            m_new = jnp.maximum(m, jnp.max(s, axis=0, keepdims=True))
            alpha = jnp.exp(m - m_new)
            p = jnp.exp(s - m_new)
            l = alpha * l + jnp.sum(p, axis=0, keepdims=True)
            probs.append((alpha, p.astype(BF16)))
            out.append((m_new, l))
        for hh in range(hb):
            alpha, p = probs[hh]
            acc_ref[hh] = alpha * acc_ref[hh] + jnp.dot(vt_ref[hh, :, pl.ds(k0, blk)], p,
                                                        preferred_element_type=F32)
        return tuple(out)

    final = lax.fori_loop(0, i, past_block, tuple(init))
    for hh in range(hb):
        o_ref[:, hh * hd:(hh + 1) * hd] = (acc_ref[hh] / final[hh][1]).T.astype(o_ref.dtype)


def _moba(proj3):
    b, s, _ = proj3.shape
    nb = s // MOBA_BLOCK
    hd, nh, hb = HEAD_DIM_MOBA, N_HEADS_MOBA, MOBA_HEADS_PER_STEP
    ng = nh // hb
    return pl.pallas_call(
        functools.partial(_moba_kernel, nb=nb, hb=hb),
        grid=(b, ng, nb),
        in_specs=[pl.BlockSpec((None, s, hb * hd), lambda bi, g, i: (bi, 0, g)),
                  pl.BlockSpec((None, s, hb * hd), lambda bi, g, i: (bi, 0, ng + g)),
                  pl.BlockSpec((None, s, hb * hd), lambda bi, g, i: (bi, 0, 2 * ng + g))],
        out_specs=pl.BlockSpec((None, MOBA_BLOCK, hb * hd), lambda bi, g, i: (bi, i, g)),
        out_shape=jax.ShapeDtypeStruct((b, s, nh * hd), BF16),
        scratch_shapes=[pltpu.VMEM((hb, hd, s), BF16), pltpu.VMEM((hb, hd, s), BF16),
                        pltpu.VMEM((hb, nb, hd), F32), pltpu.VMEM((hb, nb, MOBA_BLOCK), F32),
                        pltpu.VMEM((hb, hd, MOBA_BLOCK), F32)],
        compiler_params=_params("parallel", "parallel", "arbitrary"),
        name="moba",
    )(proj3, proj3, proj3)


def _retention_kernel(lg_ref, q_ref, k_ref, v_ref, g_ref, cos_ref, sin_ref, gnw_ref, o_ref, *, n_chunks):
    c_len = RET_CHUNK
    dk, dv = HEAD_DIM_RET_QK, HEAD_DIM_RET_V
    lg = lg_ref[pl.program_id(1)]
    ri = lax.broadcasted_iota(jnp.int32, (c_len, c_len), 0).astype(F32)
    ci = lax.broadcasted_iota(jnp.int32, (c_len, c_len), 1).astype(F32)
    diff = ri - ci
    decay = jnp.where(diff >= 0, jnp.exp(lg * jnp.maximum(diff, 0.0)), 0.0)
    idx = lax.broadcasted_iota(jnp.int32, (c_len, 1), 0).astype(F32)
    xi = jnp.exp(lg * (idx + 1.0))
    zeta = jnp.exp(lg * (c_len - 1.0 - idx))
    chunk_decay = jnp.exp(jnp.zeros((1, dv), F32) + lg * c_len)
    k_scale = dk ** -0.5

    def chunk(c, state):
        r0 = pl.multiple_of(c * c_len, c_len)
        rows = pl.ds(r0, c_len)
        cs, sn = cos_ref[rows, :], sin_ref[rows, :]
        q = q_ref[rows, :].astype(F32)
        k = k_ref[rows, :].astype(F32)
        qr = q * cs + pltpu.roll(q, dk // 2, 1) * sn
        kr = (k * cs + pltpu.roll(k, dk // 2, 1) * sn) * k_scale
        v = v_ref[rows, :]
        qb = qr.astype(BF16)
        scores = lax.dot_general(qb, kr.astype(BF16), (((1,), (1,)), ((), ())),
                                 preferred_element_type=F32) * decay
        o = jnp.dot(scores.astype(BF16), v, preferred_element_type=F32)
        o = o + jnp.dot((qr * xi).astype(BF16), state.astype(BF16), preferred_element_type=F32)
        kz = (kr * zeta).T.astype(BF16)
        state = chunk_decay * state + jnp.dot(kz, v, preferred_element_type=F32)
        mu = jnp.mean(o, axis=-1, keepdims=True)
        var = jnp.mean(jnp.square(o - mu), axis=-1, keepdims=True)
        yn = (o - mu) * lax.rsqrt(var + GN_EPS)
        g = g_ref[rows, :].astype(F32)
        y = (g * jax.nn.sigmoid(g)) * (yn * gnw_ref[...])
        o_ref[rows, :] = y.astype(o_ref.dtype)
        return state

    lax.fori_loop(0, n_chunks, chunk, jnp.zeros((dk, dv), F32))


def _retention(proj3, cos_t, sin_t, gn_w, log_gamma):
    b, s, _ = proj3.shape
    nh, dk, dv = N_HEADS_RET, HEAD_DIM_RET_QK, HEAD_DIM_RET_V
    q_blk = 3 * N_HEADS_MOBA * HEAD_DIM_MOBA // dk
    k_blk = q_blk + nh
    v_blk = (k_blk + nh) * dk // dv
    g_blk = v_blk + nh
    grid_spec = pltpu.PrefetchScalarGridSpec(
        num_scalar_prefetch=1,
        grid=(b, nh),
        in_specs=[pl.BlockSpec((None, s, dk), lambda bi, h, lg: (bi, 0, q_blk + h)),
                  pl.BlockSpec((None, s, dk), lambda bi, h, lg: (bi, 0, k_blk + h)),
                  pl.BlockSpec((None, s, dv), lambda bi, h, lg: (bi, 0, v_blk + h)),
                  pl.BlockSpec((None, s, dv), lambda bi, h, lg: (bi, 0, g_blk + h)),
                  pl.BlockSpec((s, dk), lambda bi, h, lg: (0, 0)),
                  pl.BlockSpec((s, dk), lambda bi, h, lg: (0, 0)),
                  pl.BlockSpec((1, dv), lambda bi, h, lg: (0, h))],
        out_specs=pl.BlockSpec((None, s, dv), lambda bi, h, lg: (bi, 0, h)),
    )
    return pl.pallas_call(
        functools.partial(_retention_kernel, n_chunks=s // RET_CHUNK),
        grid_spec=grid_spec,
        out_shape=jax.ShapeDtypeStruct((b, s, nh * dv), BF16),
        compiler_params=_params("parallel", "parallel"),
        name="retention",
    )(log_gamma, proj3, proj3, proj3, proj3, cos_t, sin_t, gn_w.reshape(1, nh * dv))


def _merge_kernel(oa_ref, yr_ref, ga_ref, gr_ref, wa_ref, wr_ref, o_ref):
    a = jnp.dot(oa_ref[...], wa_ref[...], preferred_element_type=F32)
    r = jnp.dot(yr_ref[...], wr_ref[...], preferred_element_type=F32)
    mix = jax.nn.sigmoid(ga_ref[...].astype(F32)) * a + jax.nn.sigmoid(gr_ref[...].astype(F32)) * r
    o_ref[...] = mix.astype(o_ref.dtype)


def _merge(o_a, y_r, proj, w_a, w_r):
    t, wa_in = o_a.shape
    wr_in = y_r.shape[1]
    d = w_a.shape[1]
    tm, tn = min(MERGE_TM, t), MERGE_TN
    ga_blk = (proj.shape[1] - 2 * d) // tn
    gr_blk = ga_blk + d // tn
    return pl.pallas_call(
        _merge_kernel,
        grid=(d // tn, t // tm),
        in_specs=[pl.BlockSpec((tm, wa_in), lambda j, i: (i, 0)),
                  pl.BlockSpec((tm, wr_in), lambda j, i: (i, 0)),
                  pl.BlockSpec((tm, tn), lambda j, i: (i, ga_blk + j)),
                  pl.BlockSpec((tm, tn), lambda j, i: (i, gr_blk + j)),
                  pl.BlockSpec((wa_in, tn), lambda j, i: (0, j)),
                  pl.BlockSpec((wr_in, tn), lambda j, i: (0, j))],
        out_specs=pl.BlockSpec((tm, tn), lambda j, i: (i, j)),
        out_shape=jax.ShapeDtypeStruct((t, d), BF16),
        compiler_params=_params("parallel", "parallel"),
        name="merge",
    )(o_a, y_r, proj, proj, w_a, w_r)


def _outproj_kernel(x_ref, mix_ref, wo_ref, nw_ref, wr_ref, br_ref, x1_ref, h_ref, route_ref):
    x1 = x_ref[...] + jnp.dot(mix_ref[...], wo_ref[...], preferred_element_type=F32)
    x1_ref[...] = x1
    ms = jnp.mean(x1 * x1, axis=-1, keepdims=True)
    h = (x1 * lax.rsqrt(ms + RMS_EPS)) * nw_ref[...]
    h_ref[...] = h

    logits = jnp.dot(h, wr_ref[...], precision=lax.Precision.HIGHEST,
                     preferred_element_type=F32) + br_ref[...]
    lane = lax.broadcasted_iota(jnp.int32, logits.shape, 1)
    ninf = -jnp.inf
    gl = jnp.where(lane < N_GROUPS, logits, ninf)
    gmax = jnp.max(gl, axis=-1, keepdims=True)
    gsum = jnp.sum(jnp.exp(gl - gmax), axis=-1, keepdims=True)
    gidx = jnp.min(jnp.where(gl == gmax, lane, V7X_LANES), axis=-1, keepdims=True)
    g_weight = 1.0 / gsum
    lo = N_GROUPS + EXPERTS_PER_GROUP * gidx
    in_group = (lane >= lo) & (lane < lo + EXPERTS_PER_GROUP)
    el = jnp.where(in_group, logits, ninf)
    emax = jnp.max(el, axis=-1, keepdims=True)
    i1 = jnp.min(jnp.where(el == emax, lane, V7X_LANES), axis=-1, keepdims=True)
    el2 = jnp.where(lane == i1, ninf, el)
    emax2 = jnp.max(el2, axis=-1, keepdims=True)
    i2 = jnp.min(jnp.where(el2 == emax2, lane, V7X_LANES), axis=-1, keepdims=True)
    esum = jnp.sum(jnp.exp(el - emax), axis=-1, keepdims=True)
    p1 = 1.0 / esum
    p2 = jnp.exp(emax2 - emax) / esum
    c1 = g_weight * (p1 / (p1 + p2))
    c2 = g_weight * (p2 / (p1 + p2))
    e1 = (i1 - N_GROUPS).astype(F32)
    e2 = (i2 - N_GROUPS).astype(F32)
    route_ref[...] = jnp.where(lane == 0, c1, jnp.where(lane == 1, c2, jnp.where(lane == 2, e1, e2)))


def _outproj(x2, mix, w_out, norm_w, w_router, b_router):
    t, d = x2.shape
    tm = min(OUTPROJ_TM, t)
    row = lambda i: (i, 0)
    const = lambda i: (0, 0)
    return pl.pallas_call(
        _outproj_kernel,
        grid=(t // tm,),
        in_specs=[pl.BlockSpec((tm, d), row),
                  pl.BlockSpec((tm, d), row),
                  pl.BlockSpec((d, d), const),
                  pl.BlockSpec((1, d), const),
                  pl.BlockSpec((d, V7X_LANES), const),
                  pl.BlockSpec((1, V7X_LANES), const)],
        out_specs=[pl.BlockSpec((tm, d), row),
                   pl.BlockSpec((tm, d), row),
                   pl.BlockSpec((tm, V7X_LANES), row)],
        out_shape=[jax.ShapeDtypeStruct((t, d), F32),
                   jax.ShapeDtypeStruct((t, d), F32),
                   jax.ShapeDtypeStruct((t, V7X_LANES), F32)],
        compiler_params=_params("parallel"),
        name="outproj",
    )(x2, mix, w_out, norm_w.reshape(1, d), w_router, b_router)


def _experts_kernel(te_ref, nt_ref, rt_ref, h_hbm, wg_ref, wu_ref, wd_ref, y_ref,
                    xbuf, wgb, wub, wdb, sem):
    i = pl.program_id(0)
    n_tiles = nt_ref[0]
    tm = xbuf.shape[1]
    slot = lax.rem(i, 2)

    def start_gather(tile, dst_slot):
        base = tile * tm

        def start(r, c):
            tok = rt_ref[base + r]
            pltpu.make_async_copy(h_hbm.at[pl.ds(tok, 1), :], xbuf.at[dst_slot, pl.ds(r, 1), :],
                                  sem.at[dst_slot]).start()
            return c

        lax.fori_loop(0, tm, start, 0, unroll=8)

    @pl.when(i == 0)
    def _():
        start_gather(0, 0)

    @pl.when(i < n_tiles)
    def _():
        @pl.when(i + 1 < n_tiles)
        def _():
            start_gather(i + 1, 1 - slot)

        @pl.when((i == 0) | (te_ref[i] != te_ref[jnp.maximum(i - 1, 0)]))
        def _():
            wgb[...] = wg_ref[...].astype(BF16)
            wub[...] = wu_ref[...].astype(BF16)
            wdb[...] = wd_ref[...].astype(BF16)

        pltpu.make_async_copy(h_hbm.at[pl.ds(0, tm), :], xbuf.at[slot], sem.at[slot]).wait()
        x = xbuf[slot].astype(BF16)
        gate = jnp.dot(x, wgb[...], preferred_element_type=F32)
        up = jnp.dot(x, wub[...], preferred_element_type=F32)
        act = (gate * jax.nn.sigmoid(gate)) * up
        y = jnp.dot(act.astype(BF16), wdb[...], preferred_element_type=F32)
        y_ref[...] = y

    @pl.when(i >= n_tiles)
    def _():
        y_ref[...] = jnp.zeros_like(y_ref)


def _experts(h, tile_expert, n_tiles, row_token, w_gate, w_up, w_down):
    t, d = h.shape
    f = w_gate.shape[2]
    tm = MOE_TM
    max_tiles = tile_expert.shape[0]
    grid_spec = pltpu.PrefetchScalarGridSpec(
        num_scalar_prefetch=3,
        grid=(max_tiles,),
        in_specs=[pl.BlockSpec(memory_space=pl.ANY),
                  pl.BlockSpec((None, d, f), lambda i, te, nt, rt: (te[i], 0, 0)),
                  pl.BlockSpec((None, d, f), lambda i, te, nt, rt: (te[i], 0, 0)),
                  pl.BlockSpec((None, f, d), lambda i, te, nt, rt: (te[i], 0, 0))],
        out_specs=pl.BlockSpec((tm, d), lambda i, te, nt, rt: (i, 0)),
        scratch_shapes=[pltpu.VMEM((2, tm, d), F32),
                        pltpu.VMEM((d, f), BF16), pltpu.VMEM((d, f), BF16), pltpu.VMEM((f, d), BF16),
                        pltpu.SemaphoreType.DMA((2,))],
    )
    return pl.pallas_call(
        _experts_kernel,
        grid_spec=grid_spec,
        out_shape=jax.ShapeDtypeStruct((max_tiles * tm, d), F32),
        compiler_params=_params("arbitrary"),
        name="experts",
    )(tile_expert, n_tiles, row_token, h, w_gate, w_up, w_down)


def _combine_kernel(pos_ref, x_ref, route_ref, y_hbm, nw_ref, o_ref, ybuf, sem, *, final_norm):
    i = pl.program_id(0)
    tm = x_ref.shape[0]
    slot = lax.rem(i, 2)

    def start_gather(tile, dst_slot):
        base = tile * (2 * tm)

        def start(r, c):
            for k in range(2):
                p = pos_ref[base + 2 * r + k]
                pltpu.make_async_copy(y_hbm.at[pl.ds(p, 1), :], ybuf.at[dst_slot, k, pl.ds(r, 1), :],
                                      sem.at[dst_slot]).start()
            return c

        lax.fori_loop(0, tm, start, 0, unroll=4)

    @pl.when(i == 0)
    def _():
        start_gather(0, 0)

    @pl.when(i + 1 < pl.num_programs(0))
    def _():
        start_gather(i + 1, 1 - slot)

    for k in range(2):
        pltpu.make_async_copy(y_hbm.at[pl.ds(0, tm), :], ybuf.at[slot, k], sem.at[slot]).wait()
    route = route_ref[...]
    x = x_ref[...] + (route[:, 0:1] * ybuf[slot, 0] + route[:, 1:2] * ybuf[slot, 1])
    if final_norm:
        ms = jnp.mean(x * x, axis=-1, keepdims=True)
        x = (x * lax.rsqrt(ms + RMS_EPS)) * nw_ref[...]
    o_ref[...] = x


def _combine(x1, route, y_sorted, pos, norm_w, final_norm):
    t, d = x1.shape
    tm = min(COMBINE_TM, t)
    grid_spec = pltpu.PrefetchScalarGridSpec(
        num_scalar_prefetch=1,
        grid=(t // tm,),
        in_specs=[pl.BlockSpec((tm, d), lambda i, pos: (i, 0)),
                  pl.BlockSpec((tm, V7X_LANES), lambda i, pos: (i, 0)),
                  pl.BlockSpec(memory_space=pl.ANY),
                  pl.BlockSpec((1, d), lambda i, pos: (0, 0))],
        out_specs=pl.BlockSpec((tm, d), lambda i, pos: (i, 0)),
        scratch_shapes=[pltpu.VMEM((2, 2, tm, d), F32), pltpu.SemaphoreType.DMA((2,))],
    )
    return pl.pallas_call(
        functools.partial(_combine_kernel, final_norm=final_norm),
        grid_spec=grid_spec,
        out_shape=jax.ShapeDtypeStruct((t, d), F32),
        compiler_params=_params("arbitrary"),
        name="combine",
    )(pos, x1, route, y_sorted, norm_w.reshape(1, d))


def _rope_tables(s):
    half = HEAD_DIM_RET_QK // 2
    inv = ROPE_BASE ** (-jnp.arange(half, dtype=F32) / half)
    ang = jnp.arange(s).astype(F32)[:, None] * inv[None, :]
    cos, sin = jnp.cos(ang), jnp.sin(ang)
    return jnp.concatenate([cos, cos], axis=-1), jnp.concatenate([-sin, sin], axis=-1)


def _dispatch_plan(route, tm):
    t = route.shape[0]
    n_assign = 2 * t
    max_tiles = n_assign // tm + N_EXPERTS
    expert = route[:, 2:4].astype(jnp.int32).reshape(n_assign)
    onehot = (expert[:, None] == jnp.arange(N_EXPERTS, dtype=jnp.int32)[None, :]).astype(jnp.int32)
    running = jnp.cumsum(onehot, axis=0)
    counts = running[-1]
    rank = jnp.sum((running - onehot) * onehot, axis=1)
    tiles_per = (counts + tm - 1) // tm
    tile_end = jnp.cumsum(tiles_per)
    row_start = (tile_end - tiles_per) * tm
    pos = row_start[expert] + rank
    n_tiles = tile_end[-1]
    tile_id = jnp.arange(max_tiles, dtype=jnp.int32)
    last = jnp.minimum(tile_id, n_tiles - 1)
    tile_expert = jnp.sum((tile_end[None, :] <= last[:, None]).astype(jnp.int32), axis=1)
    token = jnp.arange(n_assign, dtype=jnp.int32) // 2
    row_token = jnp.zeros((max_tiles * tm,), jnp.int32).at[pos].set(token)
    return tile_expert, n_tiles.reshape(1).astype(jnp.int32), row_token, pos.astype(jnp.int32)


def kernel(x, norm_mix_w, w_in, ret_gn_w, w_branch_moba, w_branch_ret, w_out, norm_ffn_w, w_router_group, b_router_group, w_router_expert, b_router_expert, w_expert_gate, w_expert_up, w_expert_down, norm_final_w):
    b, s, d = x.shape
    t = b * s
    depth = w_in.shape[0]
    assert s % MOBA_BLOCK == 0 and s % RET_CHUNK == 0 and t % MOE_TM == 0
    cos_t, sin_t = _rope_tables(s)
    log_gamma = jnp.log(1.0 - 2.0 ** (-5.0 - jnp.arange(N_HEADS_RET, dtype=F32)))
    x2 = x.reshape(t, d)
    for l in range(depth):
        proj = _inproj(x2, norm_mix_w[l], w_in[l])
        proj3 = proj.reshape(b, s, proj.shape[1])
        o_a = _moba(proj3).reshape(t, -1)
        y_r = _retention(proj3, cos_t, sin_t, ret_gn_w[l], log_gamma).reshape(t, -1)
        mix = _merge(o_a, y_r, proj, w_branch_moba[l].astype(BF16), w_branch_ret[l].astype(BF16))
        w_router = jnp.concatenate(
            [w_router_group[l], jnp.transpose(w_router_expert[l], (1, 0, 2)).reshape(d, N_EXPERTS)], axis=1)
        b_router = jnp.concatenate([b_router_group[l], b_router_expert[l].reshape(N_EXPERTS)])
        pad = V7X_LANES - w_router.shape[1]
        w_router = jnp.pad(w_router, ((0, 0), (0, pad)))
        b_router = jnp.pad(b_router, (0, pad)).reshape(1, V7X_LANES)
        x1, h, route = _outproj(x2, mix, w_out[l].astype(BF16), norm_ffn_w[l], w_router, b_router)
        tile_expert, n_tiles, row_token, pos = _dispatch_plan(route, MOE_TM)
        y_sorted = _experts(h, tile_expert, n_tiles, row_token,
                            w_expert_gate[l], w_expert_up[l], w_expert_down[l])
        x2 = _combine(x1, route, y_sorted, pos, norm_final_w, final_norm=(l == depth - 1))
    return x2.reshape(b, s, d)
```

```python
import functools

import jax
import jax.numpy as jnp
from jax import lax
from jax.experimental import pallas as pl
from jax.experimental.pallas import tpu as pltpu

F32 = jnp.float32
BF16 = jnp.bfloat16

V7X_LANES = 128
V7X_VMEM_LIMIT_BYTES = 56 * 1024 * 1024

N_HEADS_MOBA = 8
HEAD_DIM_MOBA = 128
MOBA_BLOCK = 256
MOBA_TOPK = 3
N_HEADS_RET = 8
HEAD_DIM_RET_QK = 128
HEAD_DIM_RET_V = 256
ROPE_BASE = 10000.0
N_GROUPS = 4
EXPERTS_PER_GROUP = 8
N_EXPERTS = N_GROUPS * EXPERTS_PER_GROUP
RMS_EPS = 1e-6
GN_EPS = 1e-6
NEG_BIG = -1e30

ROUTER_EXPERT_ROW = 8
ROUTER_ROWS = ROUTER_EXPERT_ROW + N_EXPERTS
ROUTE_OUT_ROWS = 8
MOBA_HEADS_PER_STEP = 4
RET_CHUNK = 256
RET_HEADS_PER_STEP = 2
INPROJ_TM, INPROJ_TN = 1024, 512
INPROJ_MSUB = 4
MERGE_TM, MERGE_TN = 512, 1024
OUTPROJ_TM = 512
OUTPROJ_SUB = 4
MOE_TM = 256
COMBINE_TM = 256


def _params(*semantics):
    return pltpu.CompilerParams(dimension_semantics=semantics,
                                vmem_limit_bytes=V7X_VMEM_LIMIT_BYTES)


def _inproj_kernel(x_ref, nw_ref, w_ref, o_ref, h_ref, wb_ref):
    j, m = pl.program_id(1), pl.program_id(2)

    @pl.when(j == 0)
    def _():
        x = x_ref[...]
        ms = jnp.mean(x * x, axis=-1, keepdims=True)
        h_ref[m] = ((x * lax.rsqrt(ms + RMS_EPS)) * nw_ref[...]).astype(BF16)

    @pl.when(m == 0)
    def _():
        wb_ref[...] = w_ref[...].astype(BF16)

    o_ref[...] = jnp.dot(h_ref[m], wb_ref[...], preferred_element_type=F32).astype(o_ref.dtype)


def _inproj(x2, norm_w, w_in):
    t, d = x2.shape
    n = w_in.shape[1]
    tm, tn = min(INPROJ_TM, t), INPROJ_TN
    msub = min(INPROJ_MSUB, t // tm)

    def x_map(p, j, m):
        return (jnp.where(j == 0, p * msub + m, p * msub + msub - 1), 0)

    return pl.pallas_call(
        _inproj_kernel,
        grid=(t // (tm * msub), n // tn, msub),
        in_specs=[pl.BlockSpec((tm, d), x_map),
                  pl.BlockSpec((1, d), lambda p, j, m: (0, 0)),
                  pl.BlockSpec((d, tn), lambda p, j, m: (0, j))],
        out_specs=pl.BlockSpec((tm, tn), lambda p, j, m: (p * msub + m, j)),
        out_shape=jax.ShapeDtypeStruct((t, n), BF16),
        scratch_shapes=[pltpu.VMEM((msub, tm, d), BF16), pltpu.VMEM((d, tn), BF16)],
        compiler_params=_params("parallel", "arbitrary", "arbitrary"),
        name="inproj",
    )(x2, norm_w.reshape(1, d), w_in)


def _moba_kernel(q_ref, k_ref, v_ref, o_ref, qt_ref, vt_ref, km_ref, sel_ref, acc_ref, *, nb, hb):
    blk, hd = MOBA_BLOCK, HEAD_DIM_MOBA
    i = pl.program_id(2)

    @pl.when(i == 0)
    def _():
        for hh in range(hb):
            cols = slice(hh * hd, (hh + 1) * hd)
            for c in range(nb):
                rows = slice(c * blk, (c + 1) * blk)
                qt_ref[hh, :, rows] = q_ref[rows, cols].astype(F32).T.astype(BF16)
                vt_ref[hh, :, rows] = v_ref[rows, cols].astype(F32).T.astype(BF16)
                km_ref[hh, c:c + 1, :] = (jnp.sum(k_ref[rows, cols].astype(F32), axis=0, keepdims=True)
                                          * (1.0 / blk))

    q0 = pl.multiple_of(i * blk, blk)
    scale = hd ** -0.5
    row = lax.broadcasted_iota(jnp.int32, (nb, blk), 0)
    kpos = lax.broadcasted_iota(jnp.int32, (blk, blk), 0)
    qpos = lax.broadcasted_iota(jnp.int32, (blk, blk), 1)

    qts = [qt_ref[hh, :, pl.ds(q0, blk)] for hh in range(hb)]
    gates = [jnp.dot(km_ref[hh], qts[hh].astype(F32), precision=lax.Precision.HIGHEST,
                     preferred_element_type=F32) for hh in range(hb)]
    own = [jnp.dot(k_ref[pl.ds(q0, blk), hh * hd:(hh + 1) * hd], qts[hh],
                   preferred_element_type=F32) for hh in range(hb)]
    init, probs = [], []
    for hh in range(hb):
        gate = jnp.where(row < i, gates[hh], -jnp.inf)
        sel = jnp.zeros((nb, blk), F32)
        for _ in range(MOBA_TOPK):
            top = jnp.max(gate, axis=0, keepdims=True)
            is_top = (gate == top) & (top > -jnp.inf)
            first = jnp.min(jnp.where(is_top, row, nb), axis=0, keepdims=True)
            pick = row == first
            sel = jnp.where(pick, 1.0, sel)
            gate = jnp.where(pick, -jnp.inf, gate)
        sel_ref[hh] = sel
        s = jnp.where(kpos <= qpos, own[hh] * scale, NEG_BIG)
        m0 = jnp.max(s, axis=0, keepdims=True)
        p = jnp.exp(s - m0)
        init.append((m0, jnp.sum(p, axis=0, keepdims=True)))
        probs.append(p.astype(BF16))
    for hh in range(hb):
        acc_ref[hh] = jnp.dot(vt_ref[hh, :, pl.ds(q0, blk)], probs[hh], preferred_element_type=F32)

    def past_block(j, carry):
        k0 = pl.multiple_of(j * blk, blk)
        scores = [jnp.dot(k_ref[pl.ds(k0, blk), hh * hd:(hh + 1) * hd], qt_ref[hh, :, pl.ds(q0, blk)],
                          preferred_element_type=F32) for hh in range(hb)]
        probs, out = [], []
        for hh in range(hb):
            m, l = carry[hh]
            s = jnp.where(sel_ref[hh, pl.ds(j, 1), :] > 0.5, scores[hh] * scale, NEG_BIG)
            m_new = jnp.maximum(m, jnp.max(s, axis=0, keepdims=True))
            alpha = jnp.exp(m - m_new)
            p = jnp.exp(s - m_new)
            l = alpha * l + jnp.sum(p, axis=0, keepdims=True)
            probs.append((alpha, p.astype(BF16)))
            out.append((m_new, l))
        for hh in range(hb):
            alpha, p = probs[hh]
            acc_ref[hh] = alpha * acc_ref[hh] + jnp.dot(vt_ref[hh, :, pl.ds(k0, blk)], p,
                                                        preferred_element_type=F32)
        return tuple(out)

    final = lax.fori_loop(0, i, past_block, tuple(init))
    for hh in range(hb):
        o_ref[:, hh * hd:(hh + 1) * hd] = (acc_ref[hh] / final[hh][1]).T.astype(o_ref.dtype)


def _moba(proj3):
    b, s, _ = proj3.shape
    nb = s // MOBA_BLOCK
    hd, nh, hb = HEAD_DIM_MOBA, N_HEADS_MOBA, MOBA_HEADS_PER_STEP
    ng = nh // hb
    return pl.pallas_call(
        functools.partial(_moba_kernel, nb=nb, hb=hb),
        grid=(b, ng, nb),
        in_specs=[pl.BlockSpec((None, s, hb * hd), lambda bi, g, i: (bi, 0, g)),
                  pl.BlockSpec((None, s, hb * hd), lambda bi, g, i: (bi, 0, ng + g)),
                  pl.BlockSpec((None, s, hb * hd), lambda bi, g, i: (bi, 0, 2 * ng + g))],
        out_specs=pl.BlockSpec((None, MOBA_BLOCK, hb * hd), lambda bi, g, i: (bi, i, g)),
        out_shape=jax.ShapeDtypeStruct((b, s, nh * hd), BF16),
        scratch_shapes=[pltpu.VMEM((hb, hd, s), BF16), pltpu.VMEM((hb, hd, s), BF16),
                        pltpu.VMEM((hb, nb, hd), F32), pltpu.VMEM((hb, nb, MOBA_BLOCK), F32),
                        pltpu.VMEM((hb, hd, MOBA_BLOCK), F32)],
        compiler_params=_params("parallel", "parallel", "arbitrary"),
        name="moba",
    )(proj3, proj3, proj3)


def _retention_kernel(lg_ref, q_ref, k_ref, v_ref, g_ref, cos_ref, sin_ref, gnw_ref, o_ref,
                      decay_ref, xi_ref, zeta_ref, st_ref, *, n_chunks, hb):
    c_len = RET_CHUNK
    dk, dv = HEAD_DIM_RET_QK, HEAD_DIM_RET_V
    ri = lax.broadcasted_iota(jnp.int32, (c_len, c_len), 0).astype(F32)
    ci = lax.broadcasted_iota(jnp.int32, (c_len, c_len), 1).astype(F32)
    diff = ri - ci
    idx = lax.broadcasted_iota(jnp.int32, (c_len, dk), 0).astype(F32)
    chunk_decay = []
    for hh in range(hb):
        lg = lg_ref[pl.program_id(1) * hb + hh]
        decay_ref[hh] = jnp.where(diff >= 0, jnp.exp(lg * jnp.maximum(diff, 0.0)), 0.0)
        xi_ref[hh] = jnp.exp(lg * (idx + 1.0))
        zeta_ref[hh] = jnp.exp(lg * (c_len - 1.0 - idx))
        chunk_decay.append(jnp.exp(jnp.zeros((1, dv), F32) + lg * c_len))
        st_ref[hh] = jnp.zeros((dk, dv), F32)
    k_scale = dk ** -0.5
    nt = (((1,), (1,)), ((), ()))

    def chunk(c, carry):
        r0 = pl.multiple_of(c * c_len, c_len)
        rows = pl.ds(r0, c_len)
        cs, sn = cos_ref[rows, :], sin_ref[rows, :]
        qbs, kbs, qxs, kzs = [], [], [], []
        for hh in range(hb):
            q = q_ref[rows, hh * dk:(hh + 1) * dk].astype(F32)
            k = k_ref[rows, hh * dk:(hh + 1) * dk].astype(F32)
            qr = q * cs + pltpu.roll(q, dk // 2, 1) * sn
            kr = (k * cs + pltpu.roll(k, dk // 2, 1) * sn) * k_scale
            qbs.append(qr.astype(BF16))
            kbs.append(kr.astype(BF16))
            qxs.append((qr * xi_ref[hh]).astype(BF16))
            kzs.append((kr * zeta_ref[hh]).T.astype(BF16))
        vs = [v_ref[rows, hh * dv:(hh + 1) * dv] for hh in range(hb)]
        scores = [lax.dot_general(qbs[hh], kbs[hh], nt, preferred_element_type=F32) for hh in range(hb)]
        cross = [jnp.dot(qxs[hh], st_ref[hh].astype(BF16), preferred_element_type=F32) for hh in range(hb)]
        upd = [jnp.dot(kzs[hh], vs[hh], preferred_element_type=F32) for hh in range(hb)]
        sd = [(scores[hh] * decay_ref[hh]).astype(BF16) for hh in range(hb)]
        inner = [jnp.dot(sd[hh], vs[hh], preferred_element_type=F32) for hh in range(hb)]
        for hh in range(hb):
            st_ref[hh] = chunk_decay[hh] * st_ref[hh] + upd[hh]
            o = inner[hh] + cross[hh]
            mu = jnp.mean(o, axis=-1, keepdims=True)
            var = jnp.mean(jnp.square(o - mu), axis=-1, keepdims=True)
            yn = (o - mu) * lax.rsqrt(var + GN_EPS)
            cols = slice(hh * dv, (hh + 1) * dv)
            g = g_ref[rows, cols].astype(F32)
            y = (g * jax.nn.sigmoid(g)) * (yn * gnw_ref[:, cols])
            o_ref[rows, cols] = y.astype(o_ref.dtype)
        return carry

    lax.fori_loop(0, n_chunks, chunk, 0)


def _retention(proj3, cos_t, sin_t, gn_w, log_gamma):
    b, s, _ = proj3.shape
    nh, dk, dv, hb = N_HEADS_RET, HEAD_DIM_RET_QK, HEAD_DIM_RET_V, RET_HEADS_PER_STEP
    ng = nh // hb
    q_blk = 3 * N_HEADS_MOBA * HEAD_DIM_MOBA // (hb * dk)
    k_blk = q_blk + ng
    v_blk = (k_blk + ng) * dk // dv
    g_blk = v_blk + ng
    grid_spec = pltpu.PrefetchScalarGridSpec(
        num_scalar_prefetch=1,
        grid=(b, ng),
        in_specs=[pl.BlockSpec((None, s, hb * dk), lambda bi, g, lg: (bi, 0, q_blk + g)),
                  pl.BlockSpec((None, s, hb * dk), lambda bi, g, lg: (bi, 0, k_blk + g)),
                  pl.BlockSpec((None, s, hb * dv), lambda bi, g, lg: (bi, 0, v_blk + g)),
                  pl.BlockSpec((None, s, hb * dv), lambda bi, g, lg: (bi, 0, g_blk + g)),
                  pl.BlockSpec((s, dk), lambda bi, g, lg: (0, 0)),
                  pl.BlockSpec((s, dk), lambda bi, g, lg: (0, 0)),
                  pl.BlockSpec((1, hb * dv), lambda bi, g, lg: (0, g))],
        out_specs=pl.BlockSpec((None, s, hb * dv), lambda bi, g, lg: (bi, 0, g)),
        scratch_shapes=[pltpu.VMEM((hb, RET_CHUNK, RET_CHUNK), F32), pltpu.VMEM((hb, RET_CHUNK, dk), F32),
                        pltpu.VMEM((hb, RET_CHUNK, dk), F32), pltpu.VMEM((hb, dk, dv), F32)],
    )
    return pl.pallas_call(
        functools.partial(_retention_kernel, n_chunks=s // RET_CHUNK, hb=hb),
        grid_spec=grid_spec,
        out_shape=jax.ShapeDtypeStruct((b, s, nh * dv), BF16),
        compiler_params=_params("parallel", "parallel"),
        name="retention",
    )(log_gamma, proj3, proj3, proj3, proj3, cos_t, sin_t, gn_w.reshape(1, nh * dv))


def _merge_kernel(oa_ref, yr_ref, ga_ref, gr_ref, wa_ref, wr_ref, o_ref):
    a = jnp.dot(oa_ref[...], wa_ref[...], preferred_element_type=F32)
    r = jnp.dot(yr_ref[...], wr_ref[...], preferred_element_type=F32)
    mix = jax.nn.sigmoid(ga_ref[...].astype(F32)) * a + jax.nn.sigmoid(gr_ref[...].astype(F32)) * r
    o_ref[...] = mix.astype(o_ref.dtype)


def _merge(o_a, y_r, proj, w_a, w_r):
    t, wa_in = o_a.shape
    wr_in = y_r.shape[1]
    d = w_a.shape[1]
    tm, tn = min(MERGE_TM, t), MERGE_TN
    ga_blk = (proj.shape[1] - 2 * d) // tn
    gr_blk = ga_blk + d // tn
    return pl.pallas_call(
        _merge_kernel,
        grid=(d // tn, t // tm),
        in_specs=[pl.BlockSpec((tm, wa_in), lambda j, i: (i, 0)),
                  pl.BlockSpec((tm, wr_in), lambda j, i: (i, 0)),
                  pl.BlockSpec((tm, tn), lambda j, i: (i, ga_blk + j)),
                  pl.BlockSpec((tm, tn), lambda j, i: (i, gr_blk + j)),
                  pl.BlockSpec((wa_in, tn), lambda j, i: (0, j)),
                  pl.BlockSpec((wr_in, tn), lambda j, i: (0, j))],
        out_specs=pl.BlockSpec((tm, tn), lambda j, i: (i, j)),
        out_shape=jax.ShapeDtypeStruct((t, d), BF16),
        compiler_params=_params("parallel", "parallel"),
        name="merge",
    )(o_a, y_r, proj, proj, w_a, w_r)


def _outproj_kernel(x_ref, mix_ref, wo_ref, nw_ref, wr_ref, br_ref, x1_ref, h_ref, route_ref):
    tm = x_ref.shape[0]
    sub = tm // OUTPROJ_SUB
    blocks = [pl.ds(k * sub, sub) for k in range(OUTPROJ_SUB)]
    proj = [jnp.dot(mix_ref[rows, :], wo_ref[...], preferred_element_type=F32) for rows in blocks]
    parts = []
    for rows, pr in zip(blocks, proj):
        x1 = x_ref[rows, :] + pr
        x1_ref[rows, :] = x1
        ms = jnp.mean(x1 * x1, axis=-1, keepdims=True)
        h = (x1 * lax.rsqrt(ms + RMS_EPS)) * nw_ref[...]
        h_ref[rows, :] = h
        h_hi = h.astype(BF16)
        h_lo = (h - h_hi.astype(F32)).astype(BF16)
        r = (jnp.dot(h_hi, wr_ref[...], preferred_element_type=F32)
             + jnp.dot(h_lo, wr_ref[...], preferred_element_type=F32))
        parts.append((r[:, :V7X_LANES] + r[:, V7X_LANES:]).T[:ROUTER_ROWS, :])
    logits = jnp.concatenate(parts, axis=1) + br_ref[...]
    row = lax.broadcasted_iota(jnp.int32, logits.shape, 0)
    ninf = -jnp.inf
    big = ROUTER_ROWS
    gl = jnp.where(row < N_GROUPS, logits, ninf)
    gmax = jnp.max(gl, axis=0, keepdims=True)
    gsum = jnp.sum(jnp.exp(gl - gmax), axis=0, keepdims=True)
    gidx = jnp.min(jnp.where(gl == gmax, row, big), axis=0, keepdims=True)
    g_weight = 1.0 / gsum
    lo = ROUTER_EXPERT_ROW + EXPERTS_PER_GROUP * gidx
    in_group = (row >= lo) & (row < lo + EXPERTS_PER_GROUP)
    el = jnp.where(in_group, logits, ninf)
    emax = jnp.max(el, axis=0, keepdims=True)
    i1 = jnp.min(jnp.where(el == emax, row, big), axis=0, keepdims=True)
    el2 = jnp.where(row == i1, ninf, el)
    emax2 = jnp.max(el2, axis=0, keepdims=True)
    i2 = jnp.min(jnp.where(el2 == emax2, row, big), axis=0, keepdims=True)
    esum = jnp.sum(jnp.exp(el - emax), axis=0, keepdims=True)
    p1 = 1.0 / esum
    p2 = jnp.exp(emax2 - emax) / esum
    c1 = g_weight * (p1 / (p1 + p2))
    c2 = g_weight * (p2 / (p1 + p2))
    e1 = (i1 - ROUTER_EXPERT_ROW).astype(F32)
    e2 = (i2 - ROUTER_EXPERT_ROW).astype(F32)
    orow = lax.broadcasted_iota(jnp.int32, route_ref.shape, 0)
    route_ref[...] = jnp.where(orow == 0, c1, jnp.where(orow == 1, c2, jnp.where(orow == 2, e1,
                               jnp.where(orow == 3, e2, 0.0))))


def _outproj(x2, mix, w_out, norm_w, w_router_t, b_router_t):
    t, d = x2.shape
    tm = min(OUTPROJ_TM, t)
    row = lambda i: (i, 0)
    const = lambda i: (0, 0)
    return pl.pallas_call(
        _outproj_kernel,
        grid=(t // tm,),
        in_specs=[pl.BlockSpec((tm, d), row),
                  pl.BlockSpec((tm, d), row),
                  pl.BlockSpec((d, d), const),
                  pl.BlockSpec((1, d), const),
                  pl.BlockSpec((d, 2 * V7X_LANES), const),
                  pl.BlockSpec((ROUTER_ROWS, 1), const)],
        out_specs=[pl.BlockSpec((tm, d), row),
                   pl.BlockSpec((tm, d), row),
                   pl.BlockSpec((ROUTE_OUT_ROWS, tm), lambda i: (0, i))],
        out_shape=[jax.ShapeDtypeStruct((t, d), F32),
                   jax.ShapeDtypeStruct((t, d), F32),
                   jax.ShapeDtypeStruct((ROUTE_OUT_ROWS, t), F32)],
        compiler_params=_params("parallel"),
        name="outproj",
    )(x2, mix, w_out, norm_w.reshape(1, d), w_router_t, b_router_t)


def _router_params(w_rg, b_rg, w_re, b_re):
    d = w_rg.shape[0]
    gap = ROUTER_EXPERT_ROW - N_GROUPS
    w = jnp.concatenate([w_rg, jnp.zeros((d, gap), F32),
                         jnp.transpose(w_re, (1, 0, 2)).reshape(d, N_EXPERTS),
                         jnp.zeros((d, V7X_LANES - ROUTER_ROWS), F32)], axis=1)
    w_hi = w.astype(BF16)
    w_lo = (w - w_hi.astype(F32)).astype(BF16)
    b_t = jnp.concatenate([b_rg, jnp.zeros((gap,), F32), b_re.reshape(N_EXPERTS)])
    return jnp.concatenate([w_hi, w_lo], axis=1), b_t.reshape(ROUTER_ROWS, 1)


def _experts_kernel(te_ref, nx_ref, nt_ref, rt_ref, h_hbm, wg_hbm, wu_hbm, wd_hbm, y_ref,
                    xbuf, wgf, wuf, wdf, wgb, wub, wdb, sem, wsem):
    i = pl.program_id(0)
    n_tiles = nt_ref[0]
    tm = xbuf.shape[1]
    slot = lax.rem(i, 2)

    def weight_copies(e):
        return (pltpu.make_async_copy(wg_hbm.at[e], wgf, wsem.at[0]),
                pltpu.make_async_copy(wu_hbm.at[e], wuf, wsem.at[1]),
                pltpu.make_async_copy(wd_hbm.at[e], wdf, wsem.at[2]))

    def start_gather(tile, dst_slot, unroll):
        base = tile * tm

        def start(r, c):
            tok = rt_ref[base + r]
            pltpu.make_async_copy(h_hbm.at[pl.ds(tok, 1), :], xbuf.at[dst_slot, pl.ds(r, 1), :],
                                  sem.at[dst_slot]).start()
            return c

        lax.fori_loop(0, tm, start, 0, unroll=unroll)

    @pl.when(i == 0)
    def _():
        for c in weight_copies(te_ref[0]):
            c.start()
        start_gather(0, 0, unroll=8)

    @pl.when(i < n_tiles)
    def _():
        @pl.when(i + 1 < n_tiles)
        def _():
            start_gather(i + 1, 1 - slot, unroll=True)

        @pl.when((i == 0) | (te_ref[i] != te_ref[jnp.maximum(i - 1, 0)]))
        def _():
            for c in weight_copies(te_ref[i]):
                c.wait()
            wgb[...] = wgf[...].astype(BF16)
            wub[...] = wuf[...].astype(BF16)
            wdb[...] = wdf[...].astype(BF16)

            @pl.when(nx_ref[i] >= 0)
            def _():
                for c in weight_copies(nx_ref[i]):
                    c.start()

        pltpu.make_async_copy(h_hbm.at[pl.ds(0, tm), :], xbuf.at[slot], sem.at[slot]).wait()
        x = xbuf[slot].astype(BF16)
        gate = jnp.dot(x, wgb[...], preferred_element_type=F32)
        up = jnp.dot(x, wub[...], preferred_element_type=F32)
        act = (gate * jax.nn.sigmoid(gate)) * up
        y = jnp.dot(act.astype(BF16), wdb[...], preferred_element_type=F32)
        y_ref[...] = y

    @pl.when(i >= n_tiles)
    def _():
        y_ref[...] = jnp.zeros_like(y_ref)


def _experts(h, tile_expert, next_expert, n_tiles, row_token, w_gate, w_up, w_down):
    t, d = h.shape
    f = w_gate.shape[2]
    tm = MOE_TM
    max_tiles = tile_expert.shape[0]
    any_spec = pl.BlockSpec(memory_space=pl.ANY)
    grid_spec = pltpu.PrefetchScalarGridSpec(
        num_scalar_prefetch=4,
        grid=(max_tiles,),
        in_specs=[any_spec, any_spec, any_spec, any_spec],
        out_specs=pl.BlockSpec((tm, d), lambda i, te, nx, nt, rt: (i, 0)),
        scratch_shapes=[pltpu.VMEM((2, tm, d), F32),
                        pltpu.VMEM((d, f), F32), pltpu.VMEM((d, f), F32), pltpu.VMEM((f, d), F32),
                        pltpu.VMEM((d, f), BF16), pltpu.VMEM((d, f), BF16), pltpu.VMEM((f, d), BF16),
                        pltpu.SemaphoreType.DMA((2,)), pltpu.SemaphoreType.DMA((3,))],
    )
    return pl.pallas_call(
        _experts_kernel,
        grid_spec=grid_spec,
        out_shape=jax.ShapeDtypeStruct((max_tiles * tm, d), F32),
        compiler_params=_params("arbitrary"),
        name="experts",
    )(tile_expert, next_expert, n_tiles, row_token, h, w_gate, w_up, w_down)


def _combine_kernel(pos_ref, x_ref, cw_ref, y_hbm, nw_ref, o_ref, ybuf, sem, *, final_norm):
    i = pl.program_id(0)
    tm = x_ref.shape[0]
    slot = lax.rem(i, 2)

    def start_gather(tile, dst_slot, unroll):
        base = tile * (2 * tm)

        def start(r, c):
            for k in range(2):
                p = pos_ref[base + 2 * r + k]
                pltpu.make_async_copy(y_hbm.at[pl.ds(p, 1), :], ybuf.at[dst_slot, k, pl.ds(r, 1), :],
                                      sem.at[dst_slot]).start()
            return c

        lax.fori_loop(0, tm, start, 0, unroll=unroll)

    @pl.when(i == 0)
    def _():
        start_gather(0, 0, unroll=8)

    @pl.when(i + 1 < pl.num_programs(0))
    def _():
        start_gather(i + 1, 1 - slot, unroll=True)

    for k in range(2):
        pltpu.make_async_copy(y_hbm.at[pl.ds(0, tm), :], ybuf.at[slot, k], sem.at[slot]).wait()
    cw = cw_ref[...]
    x = x_ref[...] + (cw[:, 0:1] * ybuf[slot, 0] + cw[:, 1:2] * ybuf[slot, 1])
    if final_norm:
        ms = jnp.mean(x * x, axis=-1, keepdims=True)
        x = (x * lax.rsqrt(ms + RMS_EPS)) * nw_ref[...]
    o_ref[...] = x


def _combine(x1, cw, y_sorted, pos, norm_w, final_norm):
    t, d = x1.shape
    tm = min(COMBINE_TM, t)
    grid_spec = pltpu.PrefetchScalarGridSpec(
        num_scalar_prefetch=1,
        grid=(t // tm,),
        in_specs=[pl.BlockSpec((tm, d), lambda i, pos: (i, 0)),
                  pl.BlockSpec((tm, 2), lambda i, pos: (i, 0)),
                  pl.BlockSpec(memory_space=pl.ANY),
                  pl.BlockSpec((1, d), lambda i, pos: (0, 0))],
        out_specs=pl.BlockSpec((tm, d), lambda i, pos: (i, 0)),
        scratch_shapes=[pltpu.VMEM((2, 2, tm, d), F32), pltpu.SemaphoreType.DMA((2,))],
    )
    return pl.pallas_call(
        functools.partial(_combine_kernel, final_norm=final_norm),
        grid_spec=grid_spec,
        out_shape=jax.ShapeDtypeStruct((t, d), F32),
        compiler_params=_params("arbitrary"),
        name="combine",
    )(pos, x1, cw, y_sorted, norm_w.reshape(1, d))


def _rope_tables(s):
    half = HEAD_DIM_RET_QK // 2
    inv = ROPE_BASE ** (-jnp.arange(half, dtype=F32) / half)
    ang = jnp.arange(s).astype(F32)[:, None] * inv[None, :]
    cos, sin = jnp.cos(ang), jnp.sin(ang)
    return jnp.concatenate([cos, cos], axis=-1), jnp.concatenate([-sin, sin], axis=-1)


def _dispatch_plan(expert_ids, tm):
    t = expert_ids.shape[0]
    n_assign = 2 * t
    max_tiles = n_assign // tm + N_EXPERTS
    expert = expert_ids.astype(jnp.int32).reshape(n_assign)
    onehot = (expert[:, None] == jnp.arange(N_EXPERTS, dtype=jnp.int32)[None, :]).astype(jnp.int32)
    running = jnp.cumsum(onehot, axis=0)
    counts = running[-1]
    rank = jnp.sum((running - onehot) * onehot, axis=1)
    tiles_per = (counts + tm - 1) // tm
    tile_end = jnp.cumsum(tiles_per)
    row_start = (tile_end - tiles_per) * tm
    pos = row_start[expert] + rank
    n_tiles = tile_end[-1]
    tile_id = jnp.arange(max_tiles, dtype=jnp.int32)
    last = jnp.minimum(tile_id, n_tiles - 1)
    tile_expert = jnp.sum((tile_end[None, :] <= last[:, None]).astype(jnp.int32), axis=1)
    token = jnp.arange(n_assign, dtype=jnp.int32) // 2
    row_token = jnp.zeros((max_tiles * tm,), jnp.int32).at[pos].set(token)
    after = tile_end[tile_expert]
    next_expert = jnp.where(after < n_tiles, tile_expert[jnp.minimum(after, max_tiles - 1)], -1)
    return (tile_expert, next_expert.astype(jnp.int32), n_tiles.reshape(1).astype(jnp.int32), row_token,
            pos.astype(jnp.int32))


def kernel(x, norm_mix_w, w_in, ret_gn_w, w_branch_moba, w_branch_ret, w_out, norm_ffn_w, w_router_group, b_router_group, w_router_expert, b_router_expert, w_expert_gate, w_expert_up, w_expert_down, norm_final_w):
    b, s, d = x.shape
    t = b * s
    depth = w_in.shape[0]
    assert s % MOBA_BLOCK == 0 and s % RET_CHUNK == 0 and t % MOE_TM == 0
    cos_t, sin_t = _rope_tables(s)
    log_gamma = jnp.log(1.0 - 2.0 ** (-5.0 - jnp.arange(N_HEADS_RET, dtype=F32)))
    x2 = x.reshape(t, d)
    for l in range(depth):
        proj = _inproj(x2, norm_mix_w[l], w_in[l])
        proj3 = proj.reshape(b, s, proj.shape[1])
        o_a = _moba(proj3).reshape(t, -1)
        y_r = _retention(proj3, cos_t, sin_t, ret_gn_w[l], log_gamma).reshape(t, -1)
        mix = _merge(o_a, y_r, proj, w_branch_moba[l].astype(BF16), w_branch_ret[l].astype(BF16))
        w_router_t, b_router_t = _router_params(w_router_group[l], b_router_group[l],
                                                w_router_expert[l], b_router_expert[l])
        x1, h, route_t = _outproj(x2, mix, w_out[l].astype(BF16), norm_ffn_w[l], w_router_t, b_router_t)
        tile_expert, next_expert, n_tiles, row_token, pos = _dispatch_plan(route_t[2:4].T, MOE_TM)
        y_sorted = _experts(h, tile_expert, next_expert, n_tiles, row_token,
                            w_expert_gate[l], w_expert_up[l], w_expert_down[l])
        x2 = _combine(x1, route_t[0:2].T, y_sorted, pos, norm_final_w, final_norm=(l == depth - 1))
    return x2.reshape(b, s, d)
```

```python
import functools

import jax
import jax.numpy as jnp
from jax import lax
from jax.experimental import pallas as pl
from jax.experimental.pallas import tpu as pltpu

F32 = jnp.float32
BF16 = jnp.bfloat16

V7X_LANES = 128
V7X_VMEM_LIMIT_BYTES = 56 * 1024 * 1024

N_HEADS_MOBA = 8
HEAD_DIM_MOBA = 128
MOBA_BLOCK = 256
MOBA_TOPK = 3
N_HEADS_RET = 8
HEAD_DIM_RET_QK = 128
HEAD_DIM_RET_V = 256
ROPE_BASE = 10000.0
N_GROUPS = 4
EXPERTS_PER_GROUP = 8
N_EXPERTS = N_GROUPS * EXPERTS_PER_GROUP
RMS_EPS = 1e-6
GN_EPS = 1e-6
NEG_BIG = -1e30

ROUTER_EXPERT_ROW = 8
ROUTER_ROWS = ROUTER_EXPERT_ROW + N_EXPERTS
ROUTE_OUT_ROWS = 8
MOBA_HEADS_PER_STEP = 4
MOBA_BIAS_ROWS = 16
MOBA_SUM_ROWS = 16
RET_CHUNK = 256
RET_HEADS_PER_STEP = 2
INPROJ_TM, INPROJ_TN = 1024, 512
INPROJ_MSUB = 4
MERGE_TM, MERGE_TN = 512, 1024
OUTPROJ_TM = 512
OUTPROJ_SUB = 4
MOE_TM = 256
WEIGHT_DMA_PRIORITY = 1
COMBINE_TM = 256


def _params(*semantics):
    return pltpu.CompilerParams(dimension_semantics=semantics,
                                vmem_limit_bytes=V7X_VMEM_LIMIT_BYTES)


def _inproj_kernel(x_ref, nw_ref, w_ref, o_ref, h_ref, wb_ref):
    j, m = pl.program_id(1), pl.program_id(2)

    @pl.when(j == 0)
    def _():
        x = x_ref[...]
        ms = jnp.mean(x * x, axis=-1, keepdims=True)
        h_ref[m] = ((x * lax.rsqrt(ms + RMS_EPS)) * nw_ref[...]).astype(BF16)

    @pl.when(m == 0)
    def _():
        wb_ref[...] = w_ref[...].astype(BF16)

    o_ref[...] = jnp.dot(h_ref[m], wb_ref[...], preferred_element_type=F32).astype(o_ref.dtype)


def _inproj(x2, norm_w, w_in):
    t, d = x2.shape
    n = w_in.shape[1]
    tm, tn = min(INPROJ_TM, t), INPROJ_TN
    msub = min(INPROJ_MSUB, t // tm)

    def x_map(p, j, m):
        return (jnp.where(j == 0, p * msub + m, p * msub + msub - 1), 0)

    return pl.pallas_call(
        _inproj_kernel,
        grid=(t // (tm * msub), n // tn, msub),
        in_specs=[pl.BlockSpec((tm, d), x_map),
                  pl.BlockSpec((1, d), lambda p, j, m: (0, 0)),
                  pl.BlockSpec((d, tn), lambda p, j, m: (0, j))],
        out_specs=pl.BlockSpec((tm, tn), lambda p, j, m: (p * msub + m, j)),
        out_shape=jax.ShapeDtypeStruct((t, n), BF16),
        scratch_shapes=[pltpu.VMEM((msub, tm, d), BF16), pltpu.VMEM((d, tn), BF16)],
        compiler_params=_params("parallel", "arbitrary", "arbitrary"),
        name="inproj",
    )(x2, norm_w.reshape(1, d), w_in)


def _moba_kernel(q_ref, k_ref, v_ref, o_ref, qta_ref, ka_ref, vta_ref, km_ref, acc_ref, sc_ref, *, nb, hb):
    blk, hd = MOBA_BLOCK, HEAD_DIM_MOBA
    i = pl.program_id(2)

    @pl.when(i == 0)
    def _():
        lane = lax.broadcasted_iota(jnp.int32, (blk, hd), 1)
        srow = lax.broadcasted_iota(jnp.int32, (MOBA_SUM_ROWS, blk), 0)
        for hh in range(hb):
            cols = slice(hh * hd, (hh + 1) * hd)
            for c in range(nb):
                rows = slice(c * blk, (c + 1) * blk)
                qta_ref[hh, 0:hd, rows] = q_ref[rows, cols].astype(F32).T.astype(BF16)
                qta_ref[hh, hd:2 * hd, rows] = jnp.zeros((hd, blk), BF16)
                ka_ref[hh, rows, 0:hd] = k_ref[rows, cols]
                ka_ref[hh, rows, hd:2 * hd] = (lane == c).astype(BF16)
                vta_ref[hh, 0:hd, rows] = v_ref[rows, cols].astype(F32).T.astype(BF16)
                vta_ref[hh, hd:hd + MOBA_SUM_ROWS, rows] = (srow == 0).astype(BF16)
                km_ref[hh, c:c + 1, :] = (jnp.sum(k_ref[rows, cols].astype(F32), axis=0, keepdims=True)
                                          * (1.0 / blk))

    q0 = pl.multiple_of(i * blk, blk)
    qcols = pl.ds(q0, blk)
    c_exp = (hd ** -0.5) * 1.4426950408889634
    row = lax.broadcasted_iota(jnp.int32, (MOBA_BIAS_ROWS, blk), 0)
    kpos = lax.broadcasted_iota(jnp.int32, (blk, blk), 0)
    qpos = lax.broadcasted_iota(jnp.int32, (blk, blk), 1)

    qts = [qta_ref[hh, 0:hd, qcols] for hh in range(hb)]
    gates = [jnp.dot(km_ref[hh], qts[hh].astype(F32), precision=lax.Precision.HIGHEST,
                     preferred_element_type=F32) for hh in range(hb)]
    own = [jnp.dot(ka_ref[hh, qcols, 0:hd], qts[hh], preferred_element_type=F32)
           for hh in range(hb)]
    init, probs = [], []
    for hh in range(hb):
        gate = jnp.where(row < i, gates[hh], -jnp.inf)
        bias = jnp.full((MOBA_BIAS_ROWS, blk), NEG_BIG, F32)
        for _ in range(MOBA_TOPK):
            top = jnp.max(gate, axis=0, keepdims=True)
            is_top = (gate == top) & (top > -jnp.inf)
            first = jnp.min(jnp.where(is_top, row, MOBA_BIAS_ROWS), axis=0, keepdims=True)
            pick = row == first
            bias = jnp.where(pick, 0.0, bias)
            gate = jnp.where(pick, -jnp.inf, gate)
        qta_ref[hh, hd:hd + MOBA_BIAS_ROWS, qcols] = bias.astype(BF16)
        s = jnp.where(kpos <= qpos, own[hh] * c_exp, NEG_BIG)
        m0 = jnp.max(s, axis=0, keepdims=True)
        init.append(m0)
        probs.append(jnp.exp2(s - m0).astype(BF16))
    for hh in range(hb):
        acc_ref[hh] = jnp.dot(vta_ref[hh, :, qcols], probs[hh], preferred_element_type=F32)

    n_trips = (i + 1) // 2

    def key_rows(t):
        return pl.ds(pl.multiple_of(t * (2 * blk), 2 * blk), 2 * blk)

    def score_dots(t):
        return [jnp.dot(ka_ref[hh, key_rows(t), :], qta_ref[hh, :, qcols], preferred_element_type=F32)
                for hh in range(hb)]

    def fold(t, slot, ms):
        probs, out = [], []
        for hh in range(hb):
            s = sc_ref[slot, hh] * c_exp
            m_new = jnp.maximum(ms[hh], jnp.max(s, axis=0, keepdims=True))
            probs.append((jnp.exp2(ms[hh] - m_new), jnp.exp2(s - m_new).astype(BF16)))
            out.append(m_new)
        for hh in range(hb):
            alpha, p = probs[hh]
            acc_ref[hh] = alpha * acc_ref[hh] + jnp.dot(vta_ref[hh, :, key_rows(t)], p,
                                                        preferred_element_type=F32)
        return tuple(out)

    def scores_to(t, slot):
        for hh, s in enumerate(score_dots(t)):
            sc_ref[slot, hh] = s

    @pl.when(n_trips > 0)
    def _():
        scores_to(0, 0)

    def two_trips(u, ms):
        t = 2 * u
        scores_to(t + 1, 1)
        ms = fold(t, 0, ms)
        scores_to(t + 2, 0)
        return fold(t + 1, 1, ms)

    n_loop = jnp.maximum(n_trips - 1, 0) // 2
    ms = lax.fori_loop(0, n_loop, two_trips, tuple(init))
    t_tail = 2 * n_loop

    @pl.when(n_trips - t_tail == 2)
    def _():
        scores_to(t_tail + 1, 1)
        fold(t_tail + 1, 1, fold(t_tail, 0, ms))

    @pl.when(n_trips - t_tail == 1)
    def _():
        fold(t_tail, 0, ms)

    for hh in range(hb):
        acc = acc_ref[hh]
        o_ref[:, hh * hd:(hh + 1) * hd] = (acc[0:hd, :] / acc[hd:hd + 1, :]).T.astype(o_ref.dtype)


def _moba(proj3):
    b, s, _ = proj3.shape
    nb = s // MOBA_BLOCK
    assert nb % 2 == 0 and nb <= MOBA_BIAS_ROWS
    hd, nh, hb = HEAD_DIM_MOBA, N_HEADS_MOBA, MOBA_HEADS_PER_STEP
    ng = nh // hb
    return pl.pallas_call(
        functools.partial(_moba_kernel, nb=nb, hb=hb),
        grid=(b, ng, nb),
        in_specs=[pl.BlockSpec((None, s, hb * hd), lambda bi, g, i: (bi, 0, g)),
                  pl.BlockSpec((None, s, hb * hd), lambda bi, g, i: (bi, 0, ng + g)),
                  pl.BlockSpec((None, s, hb * hd), lambda bi, g, i: (bi, 0, 2 * ng + g))],
        out_specs=pl.BlockSpec((None, MOBA_BLOCK, hb * hd), lambda bi, g, i: (bi, i, g)),
        out_shape=jax.ShapeDtypeStruct((b, s, nh * hd), BF16),
        scratch_shapes=[pltpu.VMEM((hb, 2 * hd, s), BF16), pltpu.VMEM((hb, s, 2 * hd), BF16),
                        pltpu.VMEM((hb, hd + MOBA_SUM_ROWS, s), BF16),
                        pltpu.VMEM((hb, MOBA_BIAS_ROWS, hd), F32),
                        pltpu.VMEM((hb, hd + MOBA_SUM_ROWS, MOBA_BLOCK), F32),
                        pltpu.VMEM((2, hb, 2 * MOBA_BLOCK, MOBA_BLOCK), F32)],
        compiler_params=_params("parallel", "parallel", "arbitrary"),
        name="moba",
    )(proj3, proj3, proj3)


def _retention_kernel(lg_ref, q_ref, k_ref, v_ref, g_ref, cos_ref, sin_ref, gnw_ref, o_ref,
                      decay_ref, xi_ref, zeta_ref, st_ref, *, n_chunks, hb):
    c_len = RET_CHUNK
    dk, dv = HEAD_DIM_RET_QK, HEAD_DIM_RET_V
    ri = lax.broadcasted_iota(jnp.int32, (c_len, c_len), 0).astype(F32)
    ci = lax.broadcasted_iota(jnp.int32, (c_len, c_len), 1).astype(F32)
    diff = ri - ci
    idx = lax.broadcasted_iota(jnp.int32, (c_len, dk), 0).astype(F32)
    chunk_decay = []
    for hh in range(hb):
        lg = lg_ref[pl.program_id(1) * hb + hh]
        decay_ref[hh] = jnp.where(diff >= 0, jnp.exp(lg * jnp.maximum(diff, 0.0)), 0.0)
        xi_ref[hh] = jnp.exp(lg * (idx + 1.0))
        zeta_ref[hh] = jnp.exp(lg * (c_len - 1.0 - idx))
        chunk_decay.append(jnp.exp(jnp.zeros((1, dv), F32) + lg * c_len))
        st_ref[hh] = jnp.zeros((dk, dv), F32)
    k_scale = dk ** -0.5
    nt = (((1,), (1,)), ((), ()))

    def chunk(c, carry):
        r0 = pl.multiple_of(c * c_len, c_len)
        rows = pl.ds(r0, c_len)
        cs, sn = cos_ref[rows, :], sin_ref[rows, :]
        qbs, kbs, qxs, kzs = [], [], [], []
        for hh in range(hb):
            q = q_ref[rows, hh * dk:(hh + 1) * dk].astype(F32)
            k = k_ref[rows, hh * dk:(hh + 1) * dk].astype(F32)
            qr = q * cs + pltpu.roll(q, dk // 2, 1) * sn
            kr = (k * cs + pltpu.roll(k, dk // 2, 1) * sn) * k_scale
            qbs.append(qr.astype(BF16))
            kbs.append(kr.astype(BF16))
            qxs.append((qr * xi_ref[hh]).astype(BF16))
            kzs.append((kr * zeta_ref[hh]).T.astype(BF16))
        vs = [v_ref[rows, hh * dv:(hh + 1) * dv] for hh in range(hb)]
        scores = [lax.dot_general(qbs[hh], kbs[hh], nt, preferred_element_type=F32) for hh in range(hb)]
        cross = [jnp.dot(qxs[hh], st_ref[hh].astype(BF16), preferred_element_type=F32) for hh in range(hb)]
        upd = [jnp.dot(kzs[hh], vs[hh], preferred_element_type=F32) for hh in range(hb)]
        sd = [(scores[hh] * decay_ref[hh]).astype(BF16) for hh in range(hb)]
        inner = [jnp.dot(sd[hh], vs[hh], preferred_element_type=F32) for hh in range(hb)]
        for hh in range(hb):
            st_ref[hh] = chunk_decay[hh] * st_ref[hh] + upd[hh]
            o = inner[hh] + cross[hh]
            mu = jnp.mean(o, axis=-1, keepdims=True)
            var = jnp.mean(jnp.square(o - mu), axis=-1, keepdims=True)
            yn = (o - mu) * lax.rsqrt(var + GN_EPS)
            cols = slice(hh * dv, (hh + 1) * dv)
            g = g_ref[rows, cols].astype(F32)
            y = (g * jax.nn.sigmoid(g)) * (yn * gnw_ref[:, cols])
            o_ref[rows, cols] = y.astype(o_ref.dtype)
        return carry

    lax.fori_loop(0, n_chunks, chunk, 0)


def _retention(proj3, cos_t, sin_t, gn_w, log_gamma):
    b, s, _ = proj3.shape
    nh, dk, dv, hb = N_HEADS_RET, HEAD_DIM_RET_QK, HEAD_DIM_RET_V, RET_HEADS_PER_STEP
    ng = nh // hb
    q_blk = 3 * N_HEADS_MOBA * HEAD_DIM_MOBA // (hb * dk)
    k_blk = q_blk + ng
    v_blk = (k_blk + ng) * dk // dv
    g_blk = v_blk + ng
    grid_spec = pltpu.PrefetchScalarGridSpec(
        num_scalar_prefetch=1,
        grid=(b, ng),
        in_specs=[pl.BlockSpec((None, s, hb * dk), lambda bi, g, lg: (bi, 0, q_blk + g)),
                  pl.BlockSpec((None, s, hb * dk), lambda bi, g, lg: (bi, 0, k_blk + g)),
                  pl.BlockSpec((None, s, hb * dv), lambda bi, g, lg: (bi, 0, v_blk + g)),
                  pl.BlockSpec((None, s, hb * dv), lambda bi, g, lg: (bi, 0, g_blk + g)),
                  pl.BlockSpec((s, dk), lambda bi, g, lg: (0, 0)),
                  pl.BlockSpec((s, dk), lambda bi, g, lg: (0, 0)),
                  pl.BlockSpec((1, hb * dv), lambda bi, g, lg: (0, g))],
        out_specs=pl.BlockSpec((None, s, hb * dv), lambda bi, g, lg: (bi, 0, g)),
        scratch_shapes=[pltpu.VMEM((hb, RET_CHUNK, RET_CHUNK), F32), pltpu.VMEM((hb, RET_CHUNK, dk), F32),
                        pltpu.VMEM((hb, RET_CHUNK, dk), F32), pltpu.VMEM((hb, dk, dv), F32)],
    )
    return pl.pallas_call(
        functools.partial(_retention_kernel, n_chunks=s // RET_CHUNK, hb=hb),
        grid_spec=grid_spec,
        out_shape=jax.ShapeDtypeStruct((b, s, nh * dv), BF16),
        compiler_params=_params("parallel", "parallel"),
        name="retention",
    )(log_gamma, proj3, proj3, proj3, proj3, cos_t, sin_t, gn_w.reshape(1, nh * dv))


def _merge_kernel(oa_ref, yr_ref, ga_ref, gr_ref, wa_ref, wr_ref, o_ref):
    a = jnp.dot(oa_ref[...], wa_ref[...], preferred_element_type=F32)
    r = jnp.dot(yr_ref[...], wr_ref[...], preferred_element_type=F32)
    mix = jax.nn.sigmoid(ga_ref[...].astype(F32)) * a + jax.nn.sigmoid(gr_ref[...].astype(F32)) * r
    o_ref[...] = mix.astype(o_ref.dtype)


def _merge(o_a, y_r, proj, w_a, w_r):
    t, wa_in = o_a.shape
    wr_in = y_r.shape[1]
    d = w_a.shape[1]
    tm, tn = min(MERGE_TM, t), MERGE_TN
    ga_blk = (proj.shape[1] - 2 * d) // tn
    gr_blk = ga_blk + d // tn
    return pl.pallas_call(
        _merge_kernel,
        grid=(d // tn, t // tm),
        in_specs=[pl.BlockSpec((tm, wa_in), lambda j, i: (i, 0)),
                  pl.BlockSpec((tm, wr_in), lambda j, i: (i, 0)),
                  pl.BlockSpec((tm, tn), lambda j, i: (i, ga_blk + j)),
                  pl.BlockSpec((tm, tn), lambda j, i: (i, gr_blk + j)),
                  pl.BlockSpec((wa_in, tn), lambda j, i: (0, j)),
                  pl.BlockSpec((wr_in, tn), lambda j, i: (0, j))],
        out_specs=pl.BlockSpec((tm, tn), lambda j, i: (i, j)),
        out_shape=jax.ShapeDtypeStruct((t, d), BF16),
        compiler_params=_params("parallel", "parallel"),
        name="merge",
    )(o_a, y_r, proj, proj, w_a, w_r)


def _outproj_kernel(x_ref, mix_ref, wo_ref, nw_ref, wr_ref, br_ref, x1_ref, h_ref, route_ref):
    tm = x_ref.shape[0]
    sub = tm // OUTPROJ_SUB
    blocks = [pl.ds(k * sub, sub) for k in range(OUTPROJ_SUB)]
    proj = [jnp.dot(mix_ref[rows, :], wo_ref[...], preferred_element_type=F32) for rows in blocks]
    parts = []
    for rows, pr in zip(blocks, proj):
        x1 = x_ref[rows, :] + pr
        x1_ref[rows, :] = x1
        ms = jnp.mean(x1 * x1, axis=-1, keepdims=True)
        h = (x1 * lax.rsqrt(ms + RMS_EPS)) * nw_ref[...]
        h_ref[rows, :] = h
        h_hi = h.astype(BF16)
        h_lo = (h - h_hi.astype(F32)).astype(BF16)
        r = (jnp.dot(h_hi, wr_ref[...], preferred_element_type=F32)
             + jnp.dot(h_lo, wr_ref[...], preferred_element_type=F32))
        parts.append((r[:, :V7X_LANES] + r[:, V7X_LANES:]).T[:ROUTER_ROWS, :])
    logits = jnp.concatenate(parts, axis=1) + br_ref[...]
    row = lax.broadcasted_iota(jnp.int32, logits.shape, 0)
    ninf = -jnp.inf
    big = ROUTER_ROWS
    gl = jnp.where(row < N_GROUPS, logits, ninf)
    gmax = jnp.max(gl, axis=0, keepdims=True)
    gsum = jnp.sum(jnp.exp(gl - gmax), axis=0, keepdims=True)
    gidx = jnp.min(jnp.where(gl == gmax, row, big), axis=0, keepdims=True)
    g_weight = 1.0 / gsum
    lo = ROUTER_EXPERT_ROW + EXPERTS_PER_GROUP * gidx
    in_group = (row >= lo) & (row < lo + EXPERTS_PER_GROUP)
    el = jnp.where(in_group, logits, ninf)
    emax = jnp.max(el, axis=0, keepdims=True)
    i1 = jnp.min(jnp.where(el == emax, row, big), axis=0, keepdims=True)
    el2 = jnp.where(row == i1, ninf, el)
    emax2 = jnp.max(el2, axis=0, keepdims=True)
    i2 = jnp.min(jnp.where(el2 == emax2, row, big), axis=0, keepdims=True)
    esum = jnp.sum(jnp.exp(el - emax), axis=0, keepdims=True)
    p1 = 1.0 / esum
    p2 = jnp.exp(emax2 - emax) / esum
    c1 = g_weight * (p1 / (p1 + p2))
    c2 = g_weight * (p2 / (p1 + p2))
    e1 = (i1 - ROUTER_EXPERT_ROW).astype(F32)
    e2 = (i2 - ROUTER_EXPERT_ROW).astype(F32)
    orow = lax.broadcasted_iota(jnp.int32, route_ref.shape, 0)
    route_ref[...] = jnp.where(orow == 0, c1, jnp.where(orow == 1, c2, jnp.where(orow == 2, e1,
                               jnp.where(orow == 3, e2, 0.0))))


def _outproj(x2, mix, w_out, norm_w, w_router_t, b_router_t):
    t, d = x2.shape
    tm = min(OUTPROJ_TM, t)
    row = lambda i: (i, 0)
    const = lambda i: (0, 0)
    return pl.pallas_call(
        _outproj_kernel,
        grid=(t // tm,),
        in_specs=[pl.BlockSpec((tm, d), row),
                  pl.BlockSpec((tm, d), row),
                  pl.BlockSpec((d, d), const),
                  pl.BlockSpec((1, d), const),
                  pl.BlockSpec((d, 2 * V7X_LANES), const),
                  pl.BlockSpec((ROUTER_ROWS, 1), const)],
        out_specs=[pl.BlockSpec((tm, d), row),
                   pl.BlockSpec((tm, d), row),
                   pl.BlockSpec((ROUTE_OUT_ROWS, tm), lambda i: (0, i))],
        out_shape=[jax.ShapeDtypeStruct((t, d), F32),
                   jax.ShapeDtypeStruct((t, d), F32),
                   jax.ShapeDtypeStruct((ROUTE_OUT_ROWS, t), F32)],
        compiler_params=_params("parallel"),
        name="outproj",
    )(x2, mix, w_out, norm_w.reshape(1, d), w_router_t, b_router_t)


def _router_params(w_rg, b_rg, w_re, b_re):
    d = w_rg.shape[0]
    gap = ROUTER_EXPERT_ROW - N_GROUPS
    w = jnp.concatenate([w_rg, jnp.zeros((d, gap), F32),
                         jnp.transpose(w_re, (1, 0, 2)).reshape(d, N_EXPERTS),
                         jnp.zeros((d, V7X_LANES - ROUTER_ROWS), F32)], axis=1)
    w_hi = w.astype(BF16)
    w_lo = (w - w_hi.astype(F32)).astype(BF16)
    b_t = jnp.concatenate([b_rg, jnp.zeros((gap,), F32), b_re.reshape(N_EXPERTS)])
    return jnp.concatenate([w_hi, w_lo], axis=1), b_t.reshape(ROUTER_ROWS, 1)


def _experts_kernel(te_ref, nx_ref, nt_ref, rt_ref, h_hbm, wg_hbm, wu_hbm, wd_hbm, y_ref,
                    xbuf, wgf, wuf, wdf, wgb, wub, wdb, sem, wsem):
    i = pl.program_id(0)
    n_tiles = nt_ref[0]
    tm = xbuf.shape[1]
    slot = lax.rem(i, 2)

    def weight_copies(e):
        return (pltpu.make_async_copy(wg_hbm.at[e], wgf, wsem.at[0]),
                pltpu.make_async_copy(wu_hbm.at[e], wuf, wsem.at[1]),
                pltpu.make_async_copy(wd_hbm.at[e], wdf, wsem.at[2]))

    def start_gather(tile, dst_slot, unroll):
        base = tile * tm

        def start(r, c):
            tok = rt_ref[base + r]
            pltpu.make_async_copy(h_hbm.at[pl.ds(tok, 1), :], xbuf.at[dst_slot, pl.ds(r, 1), :],
                                  sem.at[dst_slot]).start()
            return c

        lax.fori_loop(0, tm, start, 0, unroll=unroll)

    @pl.when(i == 0)
    def _():
        for c in weight_copies(te_ref[0]):
            c.start(priority=WEIGHT_DMA_PRIORITY)
        start_gather(0, 0, unroll=8)

    @pl.when(i < n_tiles)
    def _():
        @pl.when(i + 1 < n_tiles)
        def _():
            start_gather(i + 1, 1 - slot, unroll=True)

        @pl.when((i == 0) | (te_ref[i] != te_ref[jnp.maximum(i - 1, 0)]))
        def _():
            for c in weight_copies(te_ref[i]):
                c.wait()
            wgb[...] = wgf[...].astype(BF16)
            wub[...] = wuf[...].astype(BF16)
            wdb[...] = wdf[...].astype(BF16)

            @pl.when(nx_ref[i] >= 0)
            def _():
                for c in weight_copies(nx_ref[i]):
                    c.start(priority=WEIGHT_DMA_PRIORITY)

        pltpu.make_async_copy(h_hbm.at[pl.ds(0, tm), :], xbuf.at[slot], sem.at[slot]).wait()
        x = xbuf[slot].astype(BF16)
        gate = jnp.dot(x, wgb[...], preferred_element_type=F32)
        up = jnp.dot(x, wub[...], preferred_element_type=F32)
        act = (gate * jax.nn.sigmoid(gate)) * up
        y = jnp.dot(act.astype(BF16), wdb[...], preferred_element_type=F32)
        y_ref[...] = y

    @pl.when(i >= n_tiles)
    def _():
        y_ref[...] = jnp.zeros_like(y_ref)


def _experts(h, tile_expert, next_expert, n_tiles, row_token, w_gate, w_up, w_down):
    t, d = h.shape
    f = w_gate.shape[2]
    tm = MOE_TM
    max_tiles = tile_expert.shape[0]
    any_spec = pl.BlockSpec(memory_space=pl.ANY)
    grid_spec = pltpu.PrefetchScalarGridSpec(
        num_scalar_prefetch=4,
        grid=(max_tiles,),
        in_specs=[any_spec, any_spec, any_spec, any_spec],
        out_specs=pl.BlockSpec((tm, d), lambda i, te, nx, nt, rt: (i, 0)),
        scratch_shapes=[pltpu.VMEM((2, tm, d), F32),
                        pltpu.VMEM((d, f), F32), pltpu.VMEM((d, f), F32), pltpu.VMEM((f, d), F32),
                        pltpu.VMEM((d, f), BF16), pltpu.VMEM((d, f), BF16), pltpu.VMEM((f, d), BF16),
                        pltpu.SemaphoreType.DMA((2,)), pltpu.SemaphoreType.DMA((3,))],
    )
    return pl.pallas_call(
        _experts_kernel,
        grid_spec=grid_spec,
        out_shape=jax.ShapeDtypeStruct((max_tiles * tm, d), F32),
        compiler_params=_params("arbitrary"),
        name="experts",
    )(tile_expert, next_expert, n_tiles, row_token, h, w_gate, w_up, w_down)


def _combine_kernel(pos_ref, x_ref, cw_ref, y_hbm, nw_ref, o_ref, ybuf, sem, *, final_norm):
    i = pl.program_id(0)
    tm = x_ref.shape[0]
    slot = lax.rem(i, 2)

    def start_gather(tile, dst_slot, unroll):
        base = tile * (2 * tm)

        def start(r, c):
            for k in range(2):
                p = pos_ref[base + 2 * r + k]
                pltpu.make_async_copy(y_hbm.at[pl.ds(p, 1), :], ybuf.at[dst_slot, k, pl.ds(r, 1), :],
                                      sem.at[dst_slot]).start()
            return c

        lax.fori_loop(0, tm, start, 0, unroll=unroll)

    @pl.when(i == 0)
    def _():
        start_gather(0, 0, unroll=8)

    @pl.when(i + 1 < pl.num_programs(0))
    def _():
        start_gather(i + 1, 1 - slot, unroll=True)

    for k in range(2):
        pltpu.make_async_copy(y_hbm.at[pl.ds(0, tm), :], ybuf.at[slot, k], sem.at[slot]).wait()
    cw = cw_ref[...]
    x = x_ref[...] + (cw[:, 0:1] * ybuf[slot, 0] + cw[:, 1:2] * ybuf[slot, 1])
    if final_norm:
        ms = jnp.mean(x * x, axis=-1, keepdims=True)
        x = (x * lax.rsqrt(ms + RMS_EPS)) * nw_ref[...]
    o_ref[...] = x


def _combine(x1, cw, y_sorted, pos, norm_w, final_norm):
    t, d = x1.shape
    tm = min(COMBINE_TM, t)
    grid_spec = pltpu.PrefetchScalarGridSpec(
        num_scalar_prefetch=1,
        grid=(t // tm,),
        in_specs=[pl.BlockSpec((tm, d), lambda i, pos: (i, 0)),
                  pl.BlockSpec((tm, 2), lambda i, pos: (i, 0)),
                  pl.BlockSpec(memory_space=pl.ANY),
                  pl.BlockSpec((1, d), lambda i, pos: (0, 0))],
        out_specs=pl.BlockSpec((tm, d), lambda i, pos: (i, 0)),
        scratch_shapes=[pltpu.VMEM((2, 2, tm, d), F32), pltpu.SemaphoreType.DMA((2,))],
    )
    return pl.pallas_call(
        functools.partial(_combine_kernel, final_norm=final_norm),
        grid_spec=grid_spec,
        out_shape=jax.ShapeDtypeStruct((t, d), F32),
        compiler_params=_params("arbitrary"),
        name="combine",
    )(pos, x1, cw, y_sorted, norm_w.reshape(1, d))


def _rope_tables(s):
    half = HEAD_DIM_RET_QK // 2
    inv = ROPE_BASE ** (-jnp.arange(half, dtype=F32) / half)
    ang = jnp.arange(s).astype(F32)[:, None] * inv[None, :]
    cos, sin = jnp.cos(ang), jnp.sin(ang)
    return jnp.concatenate([cos, cos], axis=-1), jnp.concatenate([-sin, sin], axis=-1)


def _dispatch_plan(expert_ids, tm):
    t = expert_ids.shape[0]
    n_assign = 2 * t
    max_tiles = n_assign // tm + N_EXPERTS
    expert = expert_ids.astype(jnp.int32).reshape(n_assign)
    onehot = (expert[:, None] == jnp.arange(N_EXPERTS, dtype=jnp.int32)[None, :]).astype(jnp.int32)
    running = jnp.cumsum(onehot, axis=0)
    counts = running[-1]
    rank = jnp.sum((running - onehot) * onehot, axis=1)
    tiles_per = (counts + tm - 1) // tm
    tile_end = jnp.cumsum(tiles_per)
    row_start = (tile_end - tiles_per) * tm
    pos = row_start[expert] + rank
    n_tiles = tile_end[-1]
    tile_id = jnp.arange(max_tiles, dtype=jnp.int32)
    last = jnp.minimum(tile_id, n_tiles - 1)
    tile_expert = jnp.sum((tile_end[None, :] <= last[:, None]).astype(jnp.int32), axis=1)
    token = jnp.arange(n_assign, dtype=jnp.int32) // 2
    row_token = jnp.zeros((max_tiles * tm,), jnp.int32).at[pos].set(token)
    after = tile_end[tile_expert]
    next_expert = jnp.where(after < n_tiles, tile_expert[jnp.minimum(after, max_tiles - 1)], -1)
    return (tile_expert, next_expert.astype(jnp.int32), n_tiles.reshape(1).astype(jnp.int32), row_token,
            pos.astype(jnp.int32))


def kernel(x, norm_mix_w, w_in, ret_gn_w, w_branch_moba, w_branch_ret, w_out, norm_ffn_w, w_router_group, b_router_group, w_router_expert, b_router_expert, w_expert_gate, w_expert_up, w_expert_down, norm_final_w):
    b, s, d = x.shape
    t = b * s
    depth = w_in.shape[0]
    assert s % MOBA_BLOCK == 0 and s % RET_CHUNK == 0 and t % MOE_TM == 0
    cos_t, sin_t = _rope_tables(s)
    log_gamma = jnp.log(1.0 - 2.0 ** (-5.0 - jnp.arange(N_HEADS_RET, dtype=F32)))
    x2 = x.reshape(t, d)
    for l in range(depth):
        proj = _inproj(x2, norm_mix_w[l], w_in[l])
        proj3 = proj.reshape(b, s, proj.shape[1])
        o_a = _moba(proj3).reshape(t, -1)
        y_r = _retention(proj3, cos_t, sin_t, ret_gn_w[l], log_gamma).reshape(t, -1)
        mix = _merge(o_a, y_r, proj, w_branch_moba[l].astype(BF16), w_branch_ret[l].astype(BF16))
        w_router_t, b_router_t = _router_params(w_router_group[l], b_router_group[l],
                                                w_router_expert[l], b_router_expert[l])
        x1, h, route_t = _outproj(x2, mix, w_out[l].astype(BF16), norm_ffn_w[l], w_router_t, b_router_t)
        tile_expert, next_expert, n_tiles, row_token, pos = _dispatch_plan(route_t[2:4].T, MOE_TM)
        y_sorted = _experts(h, tile_expert, next_expert, n_tiles, row_token,
                            w_expert_gate[l], w_expert_up[l], w_expert_down[l])
        x2 = _combine(x1, route_t[0:2].T, y_sorted, pos, norm_final_w, final_norm=(l == depth - 1))
    return x2.reshape(b, s, d)
```

```python
import functools

import jax
import jax.numpy as jnp
from jax import lax
from jax.experimental import pallas as pl
from jax.experimental.pallas import tpu as pltpu

F32 = jnp.float32
BF16 = jnp.bfloat16

V7X_LANES = 128
V7X_VMEM_LIMIT_BYTES = 56 * 1024 * 1024

N_HEADS_MOBA = 8
HEAD_DIM_MOBA = 128
MOBA_BLOCK = 256
MOBA_TOPK = 3
N_HEADS_RET = 8
HEAD_DIM_RET_QK = 128
HEAD_DIM_RET_V = 256
ROPE_BASE = 10000.0
N_GROUPS = 4
EXPERTS_PER_GROUP = 8
N_EXPERTS = N_GROUPS * EXPERTS_PER_GROUP
RMS_EPS = 1e-6
GN_EPS = 1e-6
NEG_BIG = -1e30

ROUTER_EXPERT_ROW = 8
ROUTER_ROWS = ROUTER_EXPERT_ROW + N_EXPERTS
ROUTE_OUT_ROWS = 8
MOBA_HEADS_PER_STEP = 4
MOBA_BIAS_ROWS = 16
MOBA_SUM_ROWS = 16
RET_CHUNK = 256
RET_HEADS_PER_STEP = 2
INPROJ_TM, INPROJ_TN = 1024, 512
INPROJ_MSUB = 4
MERGE_TM, MERGE_TN = 512, 1024
OUTPROJ_TM = 512
OUTPROJ_SUB = 4
MOE_TM = 256
COMBINE_TM = 256


def _params(*semantics):
    return pltpu.CompilerParams(dimension_semantics=semantics,
                                vmem_limit_bytes=V7X_VMEM_LIMIT_BYTES)


def _inproj_kernel(x_ref, nw_ref, w_ref, o_ref, h_ref, wb_ref):
    j, m = pl.program_id(1), pl.program_id(2)

    @pl.when(j == 0)
    def _():
        x = x_ref[...]
        ms = jnp.mean(x * x, axis=-1, keepdims=True)
        h_ref[m] = ((x * lax.rsqrt(ms + RMS_EPS)) * nw_ref[...]).astype(BF16)

    @pl.when(m == 0)
    def _():
        wb_ref[...] = w_ref[...].astype(BF16)

    o_ref[...] = jnp.dot(h_ref[m], wb_ref[...], preferred_element_type=F32).astype(o_ref.dtype)


def _inproj(x2, norm_w, w_in):
    t, d = x2.shape
    n = w_in.shape[1]
    tm, tn = min(INPROJ_TM, t), INPROJ_TN
    msub = min(INPROJ_MSUB, t // tm)

    def x_map(p, j, m):
        return (jnp.where(j == 0, p * msub + m, p * msub + msub - 1), 0)

    return pl.pallas_call(
        _inproj_kernel,
        grid=(t // (tm * msub), n // tn, msub),
        in_specs=[pl.BlockSpec((tm, d), x_map),
                  pl.BlockSpec((1, d), lambda p, j, m: (0, 0)),
                  pl.BlockSpec((d, tn), lambda p, j, m: (0, j))],
        out_specs=pl.BlockSpec((tm, tn), lambda p, j, m: (p * msub + m, j)),
        out_shape=jax.ShapeDtypeStruct((t, n), BF16),
        scratch_shapes=[pltpu.VMEM((msub, tm, d), BF16), pltpu.VMEM((d, tn), BF16)],
        compiler_params=_params("parallel", "arbitrary", "arbitrary"),
        name="inproj",
    )(x2, norm_w.reshape(1, d), w_in)


def _moba_kernel(q_ref, k_ref, v_ref, o_ref, qta_ref, ka_ref, vta_ref, km_ref, acc_ref, sc_ref, *, nb, hb):
    blk, hd = MOBA_BLOCK, HEAD_DIM_MOBA
    i = pl.program_id(2)

    @pl.when(i == 0)
    def _():
        lane = lax.broadcasted_iota(jnp.int32, (blk, hd), 1)
        srow = lax.broadcasted_iota(jnp.int32, (MOBA_SUM_ROWS, blk), 0)
        for hh in range(hb):
            cols = slice(hh * hd, (hh + 1) * hd)
            for c in range(nb):
                rows = slice(c * blk, (c + 1) * blk)
                qta_ref[hh, 0:hd, rows] = q_ref[rows, cols].astype(F32).T.astype(BF16)
                qta_ref[hh, hd:2 * hd, rows] = jnp.zeros((hd, blk), BF16)
                ka_ref[hh, rows, 0:hd] = k_ref[rows, cols]
                ka_ref[hh, rows, hd:2 * hd] = (lane == c).astype(BF16)
                vta_ref[hh, 0:hd, rows] = v_ref[rows, cols].astype(F32).T.astype(BF16)
                vta_ref[hh, hd:hd + MOBA_SUM_ROWS, rows] = (srow == 0).astype(BF16)
                km_ref[hh, c:c + 1, :] = (jnp.sum(k_ref[rows, cols].astype(F32), axis=0, keepdims=True)
                                          * (1.0 / blk))

    q0 = pl.multiple_of(i * blk, blk)
    qcols = pl.ds(q0, blk)
    c_exp = (hd ** -0.5) * 1.4426950408889634
    row = lax.broadcasted_iota(jnp.int32, (MOBA_BIAS_ROWS, blk), 0)
    kpos = lax.broadcasted_iota(jnp.int32, (blk, blk), 0)
    qpos = lax.broadcasted_iota(jnp.int32, (blk, blk), 1)

    qts = [qta_ref[hh, 0:hd, qcols] for hh in range(hb)]
    gates = [jnp.dot(km_ref[hh], qts[hh].astype(F32), precision=lax.Precision.HIGHEST,
                     preferred_element_type=F32) for hh in range(hb)]
    own = [jnp.dot(ka_ref[hh, qcols, 0:hd], qts[hh], preferred_element_type=F32)
           for hh in range(hb)]
    init, probs = [], []
    for hh in range(hb):
        gate = jnp.where(row < i, gates[hh], -jnp.inf)
        bias = jnp.full((MOBA_BIAS_ROWS, blk), NEG_BIG, F32)
        for _ in range(MOBA_TOPK):
            top = jnp.max(gate, axis=0, keepdims=True)
            is_top = (gate == top) & (top > -jnp.inf)
            first = jnp.min(jnp.where(is_top, row, MOBA_BIAS_ROWS), axis=0, keepdims=True)
            pick = row == first
            bias = jnp.where(pick, 0.0, bias)
            gate = jnp.where(pick, -jnp.inf, gate)
        qta_ref[hh, hd:hd + MOBA_BIAS_ROWS, qcols] = bias.astype(BF16)
        s = jnp.where(kpos <= qpos, own[hh] * c_exp, NEG_BIG)
        m0 = jnp.max(s, axis=0, keepdims=True)
        init.append(m0)
        probs.append(jnp.exp2(s - m0).astype(BF16))
    for hh in range(hb):
        acc_ref[hh] = jnp.dot(vta_ref[hh, :, qcols], probs[hh], preferred_element_type=F32)

    n_trips = (i + 1) // 2

    def key_rows(t):
        return pl.ds(pl.multiple_of(t * (2 * blk), 2 * blk), 2 * blk)

    def score_dots(t):
        return [jnp.dot(ka_ref[hh, key_rows(t), :], qta_ref[hh, :, qcols], preferred_element_type=F32)
                for hh in range(hb)]

    def fold(t, slot, ms):
        probs, out = [], []
        for hh in range(hb):
            s = sc_ref[slot, hh] * c_exp
            m_new = jnp.maximum(ms[hh], jnp.max(s, axis=0, keepdims=True))
            probs.append((jnp.exp2(ms[hh] - m_new), jnp.exp2(s - m_new).astype(BF16)))
            out.append(m_new)
        for hh in range(hb):
            alpha, p = probs[hh]
            acc_ref[hh] = alpha * acc_ref[hh] + jnp.dot(vta_ref[hh, :, key_rows(t)], p,
                                                        preferred_element_type=F32)
        return tuple(out)

    def scores_to(t, slot):
        for hh, s in enumerate(score_dots(t)):
            sc_ref[slot, hh] = s

    @pl.when(n_trips > 0)
    def _():
        scores_to(0, 0)

    def two_trips(u, ms):
        t = 2 * u
        scores_to(t + 1, 1)
        ms = fold(t, 0, ms)
        scores_to(t + 2, 0)
        return fold(t + 1, 1, ms)

    n_loop = jnp.maximum(n_trips - 1, 0) // 2
    ms = lax.fori_loop(0, n_loop, two_trips, tuple(init))
    t_tail = 2 * n_loop

    @pl.when(n_trips - t_tail == 2)
    def _():
        scores_to(t_tail + 1, 1)
        fold(t_tail + 1, 1, fold(t_tail, 0, ms))

    @pl.when(n_trips - t_tail == 1)
    def _():
        fold(t_tail, 0, ms)

    for hh in range(hb):
        acc = acc_ref[hh]
        o_ref[:, hh * hd:(hh + 1) * hd] = (acc[0:hd, :] / acc[hd:hd + 1, :]).T.astype(o_ref.dtype)


def _moba(proj3):
    b, s, _ = proj3.shape
    nb = s // MOBA_BLOCK
    assert nb % 2 == 0 and nb <= MOBA_BIAS_ROWS
    hd, nh, hb = HEAD_DIM_MOBA, N_HEADS_MOBA, MOBA_HEADS_PER_STEP
    ng = nh // hb
    return pl.pallas_call(
        functools.partial(_moba_kernel, nb=nb, hb=hb),
        grid=(b, ng, nb),
        in_specs=[pl.BlockSpec((None, s, hb * hd), lambda bi, g, i: (bi, 0, g)),
                  pl.BlockSpec((None, s, hb * hd), lambda bi, g, i: (bi, 0, ng + g)),
                  pl.BlockSpec((None, s, hb * hd), lambda bi, g, i: (bi, 0, 2 * ng + g))],
        out_specs=pl.BlockSpec((None, MOBA_BLOCK, hb * hd), lambda bi, g, i: (bi, i, g)),
        out_shape=jax.ShapeDtypeStruct((b, s, nh * hd), BF16),
        scratch_shapes=[pltpu.VMEM((hb, 2 * hd, s), BF16), pltpu.VMEM((hb, s, 2 * hd), BF16),
                        pltpu.VMEM((hb, hd + MOBA_SUM_ROWS, s), BF16),
                        pltpu.VMEM((hb, MOBA_BIAS_ROWS, hd), F32),
                        pltpu.VMEM((hb, hd + MOBA_SUM_ROWS, MOBA_BLOCK), F32),
                        pltpu.VMEM((2, hb, 2 * MOBA_BLOCK, MOBA_BLOCK), F32)],
        compiler_params=_params("parallel", "parallel", "arbitrary"),
        name="moba",
    )(proj3, proj3, proj3)


def _retention_kernel(lg_ref, q_ref, k_ref, v_ref, g_ref, cos_ref, sin_ref, gnw_ref, o_ref,
                      decay_ref, xi_ref, zeta_ref, st_ref, *, n_chunks, hb):
    c_len = RET_CHUNK
    dk, dv = HEAD_DIM_RET_QK, HEAD_DIM_RET_V
    ri = lax.broadcasted_iota(jnp.int32, (c_len, c_len), 0).astype(F32)
    ci = lax.broadcasted_iota(jnp.int32, (c_len, c_len), 1).astype(F32)
    diff = ri - ci
    idx = lax.broadcasted_iota(jnp.int32, (c_len, dk), 0).astype(F32)
    chunk_decay = []
    for hh in range(hb):
        lg = lg_ref[pl.program_id(1) * hb + hh]
        decay_ref[hh] = jnp.where(diff >= 0, jnp.exp(lg * jnp.maximum(diff, 0.0)), 0.0)
        xi_ref[hh] = jnp.exp(lg * (idx + 1.0))
        zeta_ref[hh] = jnp.exp(lg * (c_len - 1.0 - idx))
        chunk_decay.append(jnp.exp(jnp.zeros((1, dv), F32) + lg * c_len))
        st_ref[hh] = jnp.zeros((dk, dv), F32)
    k_scale = dk ** -0.5
    nt = (((1,), (1,)), ((), ()))

    def chunk(c, carry):
        r0 = pl.multiple_of(c * c_len, c_len)
        rows = pl.ds(r0, c_len)
        cs, sn = cos_ref[rows, :], sin_ref[rows, :]
        qbs, kbs, qxs, kzs = [], [], [], []
        for hh in range(hb):
            q = q_ref[rows, hh * dk:(hh + 1) * dk].astype(F32)
            k = k_ref[rows, hh * dk:(hh + 1) * dk].astype(F32)
            qr = q * cs + pltpu.roll(q, dk // 2, 1) * sn
            kr = (k * cs + pltpu.roll(k, dk // 2, 1) * sn) * k_scale
            qbs.append(qr.astype(BF16))
            kbs.append(kr.astype(BF16))
            qxs.append((qr * xi_ref[hh]).astype(BF16))
            kzs.append((kr * zeta_ref[hh]).T.astype(BF16))
        vs = [v_ref[rows, hh * dv:(hh + 1) * dv] for hh in range(hb)]
        scores = [lax.dot_general(qbs[hh], kbs[hh], nt, preferred_element_type=F32) for hh in range(hb)]
        cross = [jnp.dot(qxs[hh], st_ref[hh].astype(BF16), preferred_element_type=F32) for hh in range(hb)]
        upd = [jnp.dot(kzs[hh], vs[hh], preferred_element_type=F32) for hh in range(hb)]
        sd = [(scores[hh] * decay_ref[hh]).astype(BF16) for hh in range(hb)]
        inner = [jnp.dot(sd[hh], vs[hh], preferred_element_type=F32) for hh in range(hb)]
        for hh in range(hb):
            st_ref[hh] = chunk_decay[hh] * st_ref[hh] + upd[hh]
            o = inner[hh] + cross[hh]
            mu = jnp.mean(o, axis=-1, keepdims=True)
            var = jnp.mean(jnp.square(o - mu), axis=-1, keepdims=True)
            yn = (o - mu) * lax.rsqrt(var + GN_EPS)
            cols = slice(hh * dv, (hh + 1) * dv)
            g = g_ref[rows, cols].astype(F32)
            y = (g * jax.nn.sigmoid(g)) * (yn * gnw_ref[:, cols])
            o_ref[rows, cols] = y.astype(o_ref.dtype)
        return carry

    lax.fori_loop(0, n_chunks, chunk, 0)


def _retention(proj3, cos_t, sin_t, gn_w, log_gamma):
    b, s, _ = proj3.shape
    nh, dk, dv, hb = N_HEADS_RET, HEAD_DIM_RET_QK, HEAD_DIM_RET_V, RET_HEADS_PER_STEP
    ng = nh // hb
    q_blk = 3 * N_HEADS_MOBA * HEAD_DIM_MOBA // (hb * dk)
    k_blk = q_blk + ng
    v_blk = (k_blk + ng) * dk // dv
    g_blk = v_blk + ng
    grid_spec = pltpu.PrefetchScalarGridSpec(
        num_scalar_prefetch=1,
        grid=(b, ng),
        in_specs=[pl.BlockSpec((None, s, hb * dk), lambda bi, g, lg: (bi, 0, q_blk + g)),
                  pl.BlockSpec((None, s, hb * dk), lambda bi, g, lg: (bi, 0, k_blk + g)),
                  pl.BlockSpec((None, s, hb * dv), lambda bi, g, lg: (bi, 0, v_blk + g)),
                  pl.BlockSpec((None, s, hb * dv), lambda bi, g, lg: (bi, 0, g_blk + g)),
                  pl.BlockSpec((s, dk), lambda bi, g, lg: (0, 0)),
                  pl.BlockSpec((s, dk), lambda bi, g, lg: (0, 0)),
                  pl.BlockSpec((1, hb * dv), lambda bi, g, lg: (0, g))],
        out_specs=pl.BlockSpec((None, s, hb * dv), lambda bi, g, lg: (bi, 0, g)),
        scratch_shapes=[pltpu.VMEM((hb, RET_CHUNK, RET_CHUNK), F32), pltpu.VMEM((hb, RET_CHUNK, dk), F32),
                        pltpu.VMEM((hb, RET_CHUNK, dk), F32), pltpu.VMEM((hb, dk, dv), F32)],
    )
    return pl.pallas_call(
        functools.partial(_retention_kernel, n_chunks=s // RET_CHUNK, hb=hb),
        grid_spec=grid_spec,
        out_shape=jax.ShapeDtypeStruct((b, s, nh * dv), BF16),
        compiler_params=_params("parallel", "parallel"),
        name="retention",
    )(log_gamma, proj3, proj3, proj3, proj3, cos_t, sin_t, gn_w.reshape(1, nh * dv))


def _merge_kernel(oa_ref, yr_ref, ga_ref, gr_ref, wa_ref, wr_ref, o_ref):
    a = jnp.dot(oa_ref[...], wa_ref[...], preferred_element_type=F32)
    r = jnp.dot(yr_ref[...], wr_ref[...], preferred_element_type=F32)
    mix = jax.nn.sigmoid(ga_ref[...].astype(F32)) * a + jax.nn.sigmoid(gr_ref[...].astype(F32)) * r
    o_ref[...] = mix.astype(o_ref.dtype)


def _merge(o_a, y_r, proj, w_a, w_r):
    t, wa_in = o_a.shape
    wr_in = y_r.shape[1]
    d = w_a.shape[1]
    tm, tn = min(MERGE_TM, t), MERGE_TN
    ga_blk = (proj.shape[1] - 2 * d) // tn
    gr_blk = ga_blk + d // tn
    return pl.pallas_call(
        _merge_kernel,
        grid=(d // tn, t // tm),
        in_specs=[pl.BlockSpec((tm, wa_in), lambda j, i: (i, 0)),
                  pl.BlockSpec((tm, wr_in), lambda j, i: (i, 0)),
                  pl.BlockSpec((tm, tn), lambda j, i: (i, ga_blk + j)),
                  pl.BlockSpec((tm, tn), lambda j, i: (i, gr_blk + j)),
                  pl.BlockSpec((wa_in, tn), lambda j, i: (0, j)),
                  pl.BlockSpec((wr_in, tn), lambda j, i: (0, j))],
        out_specs=pl.BlockSpec((tm, tn), lambda j, i: (i, j)),
        out_shape=jax.ShapeDtypeStruct((t, d), BF16),
        compiler_params=_params("parallel", "parallel"),
        name="merge",
    )(o_a, y_r, proj, proj, w_a, w_r)


def _outproj_kernel(x_ref, mix_ref, wo_ref, nw_ref, wr_ref, br_ref, x1_ref, h_ref, route_ref):
    tm = x_ref.shape[0]
    sub = tm // OUTPROJ_SUB
    blocks = [pl.ds(k * sub, sub) for k in range(OUTPROJ_SUB)]
    proj = [jnp.dot(mix_ref[rows, :], wo_ref[...], preferred_element_type=F32) for rows in blocks]
    parts = []
    for rows, pr in zip(blocks, proj):
        x1 = x_ref[rows, :] + pr
        x1_ref[rows, :] = x1
        ms = jnp.mean(x1 * x1, axis=-1, keepdims=True)
        h = (x1 * lax.rsqrt(ms + RMS_EPS)) * nw_ref[...]
        h_ref[rows, :] = h
        h_hi = h.astype(BF16)
        h_lo = (h - h_hi.astype(F32)).astype(BF16)
        r = (jnp.dot(h_hi, wr_ref[...], preferred_element_type=F32)
             + jnp.dot(h_lo, wr_ref[...], preferred_element_type=F32))
        parts.append((r[:, :V7X_LANES] + r[:, V7X_LANES:]).T[:ROUTER_ROWS, :])
    logits = jnp.concatenate(parts, axis=1) + br_ref[...]
    row = lax.broadcasted_iota(jnp.int32, logits.shape, 0)
    ninf = -jnp.inf
    big = ROUTER_ROWS
    gl = jnp.where(row < N_GROUPS, logits, ninf)
    gmax = jnp.max(gl, axis=0, keepdims=True)
    gsum = jnp.sum(jnp.exp(gl - gmax), axis=0, keepdims=True)
    gidx = jnp.min(jnp.where(gl == gmax, row, big), axis=0, keepdims=True)
    g_weight = 1.0 / gsum
    lo = ROUTER_EXPERT_ROW + EXPERTS_PER_GROUP * gidx
    in_group = (row >= lo) & (row < lo + EXPERTS_PER_GROUP)
    el = jnp.where(in_group, logits, ninf)
    emax = jnp.max(el, axis=0, keepdims=True)
    i1 = jnp.min(jnp.where(el == emax, row, big), axis=0, keepdims=True)
    el2 = jnp.where(row == i1, ninf, el)
    emax2 = jnp.max(el2, axis=0, keepdims=True)
    i2 = jnp.min(jnp.where(el2 == emax2, row, big), axis=0, keepdims=True)
    esum = jnp.sum(jnp.exp(el - emax), axis=0, keepdims=True)
    p1 = 1.0 / esum
    p2 = jnp.exp(emax2 - emax) / esum
    c1 = g_weight * (p1 / (p1 + p2))
    c2 = g_weight * (p2 / (p1 + p2))
    e1 = (i1 - ROUTER_EXPERT_ROW).astype(F32)
    e2 = (i2 - ROUTER_EXPERT_ROW).astype(F32)
    orow = lax.broadcasted_iota(jnp.int32, route_ref.shape, 0)
    route_ref[...] = jnp.where(orow == 0, c1, jnp.where(orow == 1, c2, jnp.where(orow == 2, e1,
                               jnp.where(orow == 3, e2, 0.0))))


def _outproj(x2, mix, w_out, norm_w, w_router_t, b_router_t):
    t, d = x2.shape
    tm = min(OUTPROJ_TM, t)
    row = lambda i: (i, 0)
    const = lambda i: (0, 0)
    return pl.pallas_call(
        _outproj_kernel,
        grid=(t // tm,),
        in_specs=[pl.BlockSpec((tm, d), row),
                  pl.BlockSpec((tm, d), row),
                  pl.BlockSpec((d, d), const),
                  pl.BlockSpec((1, d), const),
                  pl.BlockSpec((d, 2 * V7X_LANES), const),
                  pl.BlockSpec((ROUTER_ROWS, 1), const)],
        out_specs=[pl.BlockSpec((tm, d), row),
                   pl.BlockSpec((tm, d), row),
                   pl.BlockSpec((ROUTE_OUT_ROWS, tm), lambda i: (0, i))],
        out_shape=[jax.ShapeDtypeStruct((t, d), F32),
                   jax.ShapeDtypeStruct((t, d), F32),
                   jax.ShapeDtypeStruct((ROUTE_OUT_ROWS, t), F32)],
        compiler_params=_params("parallel"),
        name="outproj",
    )(x2, mix, w_out, norm_w.reshape(1, d), w_router_t, b_router_t)


def _router_params(w_rg, b_rg, w_re, b_re):
    d = w_rg.shape[0]
    gap = ROUTER_EXPERT_ROW - N_GROUPS
    w = jnp.concatenate([w_rg, jnp.zeros((d, gap), F32),
                         jnp.transpose(w_re, (1, 0, 2)).reshape(d, N_EXPERTS),
                         jnp.zeros((d, V7X_LANES - ROUTER_ROWS), F32)], axis=1)
    w_hi = w.astype(BF16)
    w_lo = (w - w_hi.astype(F32)).astype(BF16)
    b_t = jnp.concatenate([b_rg, jnp.zeros((gap,), F32), b_re.reshape(N_EXPERTS)])
    return jnp.concatenate([w_hi, w_lo], axis=1), b_t.reshape(ROUTER_ROWS, 1)


def _dispatch_kernel(nt_ref, rt_ref, h_hbm, x_ref, xbuf, sem):
    i = pl.program_id(0)
    n_tiles = nt_ref[0]
    tm = xbuf.shape[1]
    slot = lax.rem(i, 2)

    def start_gather(tile, dst_slot, unroll):
        base = tile * tm

        def start(r, c):
            tok = rt_ref[base + r]
            pltpu.make_async_copy(h_hbm.at[pl.ds(tok, 1), :], xbuf.at[dst_slot, pl.ds(r, 1), :],
                                  sem.at[dst_slot]).start()
            return c

        lax.fori_loop(0, tm, start, 0, unroll=unroll)

    @pl.when(i == 0)
    def _():
        start_gather(0, 0, unroll=8)

    @pl.when(i < n_tiles)
    def _():
        @pl.when(i + 1 < n_tiles)
        def _():
            start_gather(i + 1, 1 - slot, unroll=True)

        pltpu.make_async_copy(h_hbm.at[pl.ds(0, tm), :], xbuf.at[slot], sem.at[slot]).wait()
        x_ref[...] = xbuf[slot].astype(x_ref.dtype)

    @pl.when(i >= n_tiles)
    def _():
        x_ref[...] = jnp.zeros_like(x_ref)


def _dispatch(h, n_tiles, row_token, max_tiles):
    t, d = h.shape
    tm = MOE_TM
    grid_spec = pltpu.PrefetchScalarGridSpec(
        num_scalar_prefetch=2,
        grid=(max_tiles,),
        in_specs=[pl.BlockSpec(memory_space=pl.ANY)],
        out_specs=pl.BlockSpec((tm, d), lambda i, nt, rt: (i, 0)),
        scratch_shapes=[pltpu.VMEM((2, tm, d), F32), pltpu.SemaphoreType.DMA((2,))],
    )
    return pl.pallas_call(
        _dispatch_kernel,
        grid_spec=grid_spec,
        out_shape=jax.ShapeDtypeStruct((max_tiles * tm, d), BF16),
        compiler_params=_params("arbitrary"),
        name="dispatch",
    )(n_tiles, row_token, h)


def _experts_kernel(te_ref, nx_ref, nt_ref, x_ref, wg_hbm, wu_hbm, wd_hbm, y_ref,
                    wgf, wuf, wdf, wgb, wub, wdb, wsem):
    i = pl.program_id(0)
    n_tiles = nt_ref[0]

    def weight_copies(e):
        return (pltpu.make_async_copy(wg_hbm.at[e], wgf, wsem.at[0]),
                pltpu.make_async_copy(wu_hbm.at[e], wuf, wsem.at[1]),
                pltpu.make_async_copy(wd_hbm.at[e], wdf, wsem.at[2]))

    @pl.when(i == 0)
    def _():
        for c in weight_copies(te_ref[0]):
            c.start()

    @pl.when(i < n_tiles)
    def _():
        @pl.when((i == 0) | (te_ref[i] != te_ref[jnp.maximum(i - 1, 0)]))
        def _():
            for c in weight_copies(te_ref[i]):
                c.wait()
            wgb[...] = wgf[...].astype(BF16)
            wub[...] = wuf[...].astype(BF16)
            wdb[...] = wdf[...].astype(BF16)

            @pl.when(nx_ref[i] >= 0)
            def _():
                for c in weight_copies(nx_ref[i]):
                    c.start()

        x = x_ref[...]
        gate = jnp.dot(x, wgb[...], preferred_element_type=F32)
        up = jnp.dot(x, wub[...], preferred_element_type=F32)
        act = (gate * jax.nn.sigmoid(gate)) * up
        y_ref[...] = jnp.dot(act.astype(BF16), wdb[...], preferred_element_type=F32)

    @pl.when(i >= n_tiles)
    def _():
        y_ref[...] = jnp.zeros_like(y_ref)


def _experts(x_sorted, tile_expert, next_expert, n_tiles, w_gate, w_up, w_down):
    d = x_sorted.shape[1]
    f = w_gate.shape[2]
    tm = MOE_TM
    max_tiles = tile_expert.shape[0]
    any_spec = pl.BlockSpec(memory_space=pl.ANY)
    grid_spec = pltpu.PrefetchScalarGridSpec(
        num_scalar_prefetch=3,
        grid=(max_tiles,),
        in_specs=[pl.BlockSpec((tm, d), lambda i, te, nx, nt: (jnp.minimum(i, nt[0] - 1), 0)),
                  any_spec, any_spec, any_spec],
        out_specs=pl.BlockSpec((tm, d), lambda i, te, nx, nt: (i, 0)),
        scratch_shapes=[pltpu.VMEM((d, f), F32), pltpu.VMEM((d, f), F32), pltpu.VMEM((f, d), F32),
                        pltpu.VMEM((d, f), BF16), pltpu.VMEM((d, f), BF16), pltpu.VMEM((f, d), BF16),
                        pltpu.SemaphoreType.DMA((3,))],
    )
    return pl.pallas_call(
        _experts_kernel,
        grid_spec=grid_spec,
        out_shape=jax.ShapeDtypeStruct((max_tiles * tm, d), F32),
        compiler_params=_params("arbitrary"),
        name="experts",
    )(tile_expert, next_expert, n_tiles, x_sorted, w_gate, w_up, w_down)


def _combine_kernel(pos_ref, x_ref, cw_ref, y_hbm, nw_ref, o_ref, ybuf, sem, *, final_norm):
    i = pl.program_id(0)
    tm = x_ref.shape[0]
    slot = lax.rem(i, 2)

    def start_gather(tile, dst_slot, unroll):
        base = tile * (2 * tm)

        def start(r, c):
            for k in range(2):
                p = pos_ref[base + 2 * r + k]
                pltpu.make_async_copy(y_hbm.at[pl.ds(p, 1), :], ybuf.at[dst_slot, k, pl.ds(r, 1), :],
                                      sem.at[dst_slot]).start()
            return c

        lax.fori_loop(0, tm, start, 0, unroll=unroll)

    @pl.when(i == 0)
    def _():
        start_gather(0, 0, unroll=8)

    @pl.when(i + 1 < pl.num_programs(0))
    def _():
        start_gather(i + 1, 1 - slot, unroll=True)

    for k in range(2):
        pltpu.make_async_copy(y_hbm.at[pl.ds(0, tm), :], ybuf.at[slot, k], sem.at[slot]).wait()
    cw = cw_ref[...]
    x = x_ref[...] + (cw[:, 0:1] * ybuf[slot, 0] + cw[:, 1:2] * ybuf[slot, 1])
    if final_norm:
        ms = jnp.mean(x * x, axis=-1, keepdims=True)
        x = (x * lax.rsqrt(ms + RMS_EPS)) * nw_ref[...]
    o_ref[...] = x


def _combine(x1, cw, y_sorted, pos, norm_w, final_norm):
    t, d = x1.shape
    tm = min(COMBINE_TM, t)
    grid_spec = pltpu.PrefetchScalarGridSpec(
        num_scalar_prefetch=1,
        grid=(t // tm,),
        in_specs=[pl.BlockSpec((tm, d), lambda i, pos: (i, 0)),
                  pl.BlockSpec((tm, 2), lambda i, pos: (i, 0)),
                  pl.BlockSpec(memory_space=pl.ANY),
                  pl.BlockSpec((1, d), lambda i, pos: (0, 0))],
        out_specs=pl.BlockSpec((tm, d), lambda i, pos: (i, 0)),
        scratch_shapes=[pltpu.VMEM((2, 2, tm, d), F32), pltpu.SemaphoreType.DMA((2,))],
    )
    return pl.pallas_call(
        functools.partial(_combine_kernel, final_norm=final_norm),
        grid_spec=grid_spec,
        out_shape=jax.ShapeDtypeStruct((t, d), F32),
        compiler_params=_params("arbitrary"),
        name="combine",
    )(pos, x1, cw, y_sorted, norm_w.reshape(1, d))


def _rope_tables(s):
    half = HEAD_DIM_RET_QK // 2
    inv = ROPE_BASE ** (-jnp.arange(half, dtype=F32) / half)
    ang = jnp.arange(s).astype(F32)[:, None] * inv[None, :]
    cos, sin = jnp.cos(ang), jnp.sin(ang)
    return jnp.concatenate([cos, cos], axis=-1), jnp.concatenate([-sin, sin], axis=-1)


def _dispatch_plan(expert_ids, tm):
    t = expert_ids.shape[0]
    n_assign = 2 * t
    max_tiles = n_assign // tm + N_EXPERTS
    expert = expert_ids.astype(jnp.int32).reshape(n_assign)
    onehot = (expert[:, None] == jnp.arange(N_EXPERTS, dtype=jnp.int32)[None, :]).astype(jnp.int32)
    running = jnp.cumsum(onehot, axis=0)
    counts = running[-1]
    rank = jnp.sum((running - onehot) * onehot, axis=1)
    tiles_per = (counts + tm - 1) // tm
    tile_end = jnp.cumsum(tiles_per)
    row_start = (tile_end - tiles_per) * tm
    pos = row_start[expert] + rank
    n_tiles = tile_end[-1]
    tile_id = jnp.arange(max_tiles, dtype=jnp.int32)
    last = jnp.minimum(tile_id, n_tiles - 1)
    tile_expert = jnp.sum((tile_end[None, :] <= last[:, None]).astype(jnp.int32), axis=1)
    token = jnp.arange(n_assign, dtype=jnp.int32) // 2
    row_token = jnp.zeros((max_tiles * tm,), jnp.int32).at[pos].set(token)
    after = tile_end[tile_expert]
    next_expert = jnp.where(after < n_tiles, tile_expert[jnp.minimum(after, max_tiles - 1)], -1)
    return (tile_expert, next_expert.astype(jnp.int32), n_tiles.reshape(1).astype(jnp.int32), row_token,
            pos.astype(jnp.int32))


def kernel(x, norm_mix_w, w_in, ret_gn_w, w_branch_moba, w_branch_ret, w_out, norm_ffn_w, w_router_group, b_router_group, w_router_expert, b_router_expert, w_expert_gate, w_expert_up, w_expert_down, norm_final_w):
    b, s, d = x.shape
    t = b * s
    depth = w_in.shape[0]
    assert s % MOBA_BLOCK == 0 and s % RET_CHUNK == 0 and t % MOE_TM == 0
    cos_t, sin_t = _rope_tables(s)
    log_gamma = jnp.log(1.0 - 2.0 ** (-5.0 - jnp.arange(N_HEADS_RET, dtype=F32)))
    x2 = x.reshape(t, d)
    for l in range(depth):
        proj = _inproj(x2, norm_mix_w[l], w_in[l])
        proj3 = proj.reshape(b, s, proj.shape[1])
        o_a = _moba(proj3).reshape(t, -1)
        y_r = _retention(proj3, cos_t, sin_t, ret_gn_w[l], log_gamma).reshape(t, -1)
        mix = _merge(o_a, y_r, proj, w_branch_moba[l].astype(BF16), w_branch_ret[l].astype(BF16))
        w_router_t, b_router_t = _router_params(w_router_group[l], b_router_group[l],
                                                w_router_expert[l], b_router_expert[l])
        x1, h, route_t = _outproj(x2, mix, w_out[l].astype(BF16), norm_ffn_w[l], w_router_t, b_router_t)
        tile_expert, next_expert, n_tiles, row_token, pos = _dispatch_plan(route_t[2:4].T, MOE_TM)
        x_sorted = _dispatch(h, n_tiles, row_token, tile_expert.shape[0])
        y_sorted = _experts(x_sorted, tile_expert, next_expert, n_tiles,
                            w_expert_gate[l], w_expert_up[l], w_expert_down[l])
        x2 = _combine(x1, route_t[0:2].T, y_sorted, pos, norm_final_w, final_norm=(l == depth - 1))
    return x2.reshape(b, s, d)
```

```python
import functools

import jax
import jax.numpy as jnp
from jax import lax
from jax.experimental import pallas as pl
from jax.experimental.pallas import tpu as pltpu

F32 = jnp.float32
BF16 = jnp.bfloat16

V7X_LANES = 128
V7X_VMEM_LIMIT_BYTES = 56 * 1024 * 1024

N_HEADS_MOBA = 8
HEAD_DIM_MOBA = 128
MOBA_BLOCK = 256
MOBA_TOPK = 3
N_HEADS_RET = 8
HEAD_DIM_RET_QK = 128
HEAD_DIM_RET_V = 256
ROPE_BASE = 10000.0
N_GROUPS = 4
EXPERTS_PER_GROUP = 8
N_EXPERTS = N_GROUPS * EXPERTS_PER_GROUP
RMS_EPS = 1e-6
GN_EPS = 1e-6
NEG_BIG = -1e30

ROUTER_EXPERT_ROW = 8
ROUTER_ROWS = ROUTER_EXPERT_ROW + N_EXPERTS
ROUTE_OUT_ROWS = 8
MOBA_HEADS_PER_STEP = 4
MOBA_BIAS_ROWS = 16
MOBA_SUM_ROWS = 16
RET_CHUNK = 256
RET_HEADS_PER_STEP = 2
INPROJ_TM, INPROJ_TN = 1024, 1024
INPROJ_MSUB = 2
MERGE_TM, MERGE_TN = 512, 1024
OUTPROJ_TM = 512
OUTPROJ_SUB = 4
MOE_TM = 256
COMBINE_TM = 256


def _params(*semantics):
    return pltpu.CompilerParams(dimension_semantics=semantics,
                                vmem_limit_bytes=V7X_VMEM_LIMIT_BYTES)


def _inproj_kernel(x_ref, nw_ref, w_ref, o_ref, h_ref, wb_ref):
    j, m = pl.program_id(1), pl.program_id(2)

    @pl.when(j == 0)
    def _():
        x = x_ref[...]
        ms = jnp.mean(x * x, axis=-1, keepdims=True)
        h_ref[m] = ((x * lax.rsqrt(ms + RMS_EPS)) * nw_ref[...]).astype(BF16)

    @pl.when(m == 0)
    def _():
        wb_ref[...] = w_ref[...].astype(BF16)

    o_ref[...] = jnp.dot(h_ref[m], wb_ref[...], preferred_element_type=F32).astype(o_ref.dtype)


def _inproj(x2, norm_w, w_in):
    t, d = x2.shape
    n = w_in.shape[1]
    tm, tn = min(INPROJ_TM, t), INPROJ_TN
    msub = min(INPROJ_MSUB, t // tm)

    def x_map(p, j, m):
        return (jnp.where(j == 0, p * msub + m, p * msub + msub - 1), 0)

    return pl.pallas_call(
        _inproj_kernel,
        grid=(t // (tm * msub), n // tn, msub),
        in_specs=[pl.BlockSpec((tm, d), x_map),
                  pl.BlockSpec((1, d), lambda p, j, m: (0, 0)),
                  pl.BlockSpec((d, tn), lambda p, j, m: (0, j))],
        out_specs=pl.BlockSpec((tm, tn), lambda p, j, m: (p * msub + m, j)),
        out_shape=jax.ShapeDtypeStruct((t, n), BF16),
        scratch_shapes=[pltpu.VMEM((msub, tm, d), BF16), pltpu.VMEM((d, tn), BF16)],
        compiler_params=_params("parallel", "arbitrary", "arbitrary"),
        name="inproj",
    )(x2, norm_w.reshape(1, d), w_in)


def _moba_kernel(q_ref, k_ref, v_ref, o_ref, qta_ref, ka_ref, vta_ref, km_ref, acc_ref, sc_ref, *, nb, hb):
    blk, hd = MOBA_BLOCK, HEAD_DIM_MOBA
    i = pl.program_id(2)

    @pl.when(i == 0)
    def _():
        lane = lax.broadcasted_iota(jnp.int32, (blk, hd), 1)
        srow = lax.broadcasted_iota(jnp.int32, (MOBA_SUM_ROWS, blk), 0)
        for hh in range(hb):
            cols = slice(hh * hd, (hh + 1) * hd)
            for c in range(nb):
                rows = slice(c * blk, (c + 1) * blk)
                qta_ref[hh, 0:hd, rows] = q_ref[rows, cols].astype(F32).T.astype(BF16)
                qta_ref[hh, hd:2 * hd, rows] = jnp.zeros((hd, blk), BF16)
                ka_ref[hh, rows, 0:hd] = k_ref[rows, cols]
                ka_ref[hh, rows, hd:2 * hd] = (lane == c).astype(BF16)
                vta_ref[hh, 0:hd, rows] = v_ref[rows, cols].astype(F32).T.astype(BF16)
                vta_ref[hh, hd:hd + MOBA_SUM_ROWS, rows] = (srow == 0).astype(BF16)
                km_ref[hh, c:c + 1, :] = (jnp.sum(k_ref[rows, cols].astype(F32), axis=0, keepdims=True)
                                          * (1.0 / blk))

    q0 = pl.multiple_of(i * blk, blk)
    qcols = pl.ds(q0, blk)
    c_exp = (hd ** -0.5) * 1.4426950408889634
    row = lax.broadcasted_iota(jnp.int32, (MOBA_BIAS_ROWS, blk), 0)
    kpos = lax.broadcasted_iota(jnp.int32, (blk, blk), 0)
    qpos = lax.broadcasted_iota(jnp.int32, (blk, blk), 1)

    qts = [qta_ref[hh, 0:hd, qcols] for hh in range(hb)]
    gates = [jnp.dot(km_ref[hh], qts[hh].astype(F32), precision=lax.Precision.HIGHEST,
                     preferred_element_type=F32) for hh in range(hb)]
    own = [jnp.dot(ka_ref[hh, qcols, 0:hd], qts[hh], preferred_element_type=F32)
           for hh in range(hb)]
    init, probs = [], []
    for hh in range(hb):
        gate = jnp.where(row < i, gates[hh], -jnp.inf)
        bias = jnp.full((MOBA_BIAS_ROWS, blk), NEG_BIG, F32)
        for _ in range(MOBA_TOPK):
            top = jnp.max(gate, axis=0, keepdims=True)
            is_top = (gate == top) & (top > -jnp.inf)
            first = jnp.min(jnp.where(is_top, row, MOBA_BIAS_ROWS), axis=0, keepdims=True)
            pick = row == first
            bias = jnp.where(pick, 0.0, bias)
            gate = jnp.where(pick, -jnp.inf, gate)
        qta_ref[hh, hd:hd + MOBA_BIAS_ROWS, qcols] = bias.astype(BF16)
        s = jnp.where(kpos <= qpos, own[hh] * c_exp, NEG_BIG)
        m0 = jnp.max(s, axis=0, keepdims=True)
        init.append(m0)
        probs.append(jnp.exp2(s - m0).astype(BF16))
    for hh in range(hb):
        acc_ref[hh] = jnp.dot(vta_ref[hh, :, qcols], probs[hh], preferred_element_type=F32)

    n_trips = (i + 1) // 2

    def key_rows(t):
        return pl.ds(pl.multiple_of(t * (2 * blk), 2 * blk), 2 * blk)

    def score_dots(t):
        return [jnp.dot(ka_ref[hh, key_rows(t), :], qta_ref[hh, :, qcols], preferred_element_type=F32)
                for hh in range(hb)]

    def fold(t, slot, ms):
        probs, out = [], []
        for hh in range(hb):
            s = sc_ref[slot, hh] * c_exp
            m_new = jnp.maximum(ms[hh], jnp.max(s, axis=0, keepdims=True))
            probs.append((jnp.exp2(ms[hh] - m_new), jnp.exp2(s - m_new).astype(BF16)))
            out.append(m_new)
        for hh in range(hb):
            alpha, p = probs[hh]
            acc_ref[hh] = alpha * acc_ref[hh] + jnp.dot(vta_ref[hh, :, key_rows(t)], p,
                                                        preferred_element_type=F32)
        return tuple(out)

    def scores_to(t, slot):
        for hh, s in enumerate(score_dots(t)):
            sc_ref[slot, hh] = s

    @pl.when(n_trips > 0)
    def _():
        scores_to(0, 0)

    def two_trips(u, ms):
        t = 2 * u
        scores_to(t + 1, 1)
        ms = fold(t, 0, ms)
        scores_to(t + 2, 0)
        return fold(t + 1, 1, ms)

    n_loop = jnp.maximum(n_trips - 1, 0) // 2
    ms = lax.fori_loop(0, n_loop, two_trips, tuple(init))
    t_tail = 2 * n_loop

    @pl.when(n_trips - t_tail == 2)
    def _():
        scores_to(t_tail + 1, 1)
        fold(t_tail + 1, 1, fold(t_tail, 0, ms))

    @pl.when(n_trips - t_tail == 1)
    def _():
        fold(t_tail, 0, ms)

    for hh in range(hb):
        acc = acc_ref[hh]
        o_ref[:, hh * hd:(hh + 1) * hd] = (acc[0:hd, :] / acc[hd:hd + 1, :]).T.astype(o_ref.dtype)


def _moba(proj3):
    b, s, _ = proj3.shape
    nb = s // MOBA_BLOCK
    assert nb % 2 == 0 and nb <= MOBA_BIAS_ROWS
    hd, nh, hb = HEAD_DIM_MOBA, N_HEADS_MOBA, MOBA_HEADS_PER_STEP
    ng = nh // hb
    return pl.pallas_call(
        functools.partial(_moba_kernel, nb=nb, hb=hb),
        grid=(b, ng, nb),
        in_specs=[pl.BlockSpec((None, s, hb * hd), lambda bi, g, i: (bi, 0, g)),
                  pl.BlockSpec((None, s, hb * hd), lambda bi, g, i: (bi, 0, ng + g)),
                  pl.BlockSpec((None, s, hb * hd), lambda bi, g, i: (bi, 0, 2 * ng + g))],
        out_specs=pl.BlockSpec((None, MOBA_BLOCK, hb * hd), lambda bi, g, i: (bi, i, g)),
        out_shape=jax.ShapeDtypeStruct((b, s, nh * hd), BF16),
        scratch_shapes=[pltpu.VMEM((hb, 2 * hd, s), BF16), pltpu.VMEM((hb, s, 2 * hd), BF16),
                        pltpu.VMEM((hb, hd + MOBA_SUM_ROWS, s), BF16),
                        pltpu.VMEM((hb, MOBA_BIAS_ROWS, hd), F32),
                        pltpu.VMEM((hb, hd + MOBA_SUM_ROWS, MOBA_BLOCK), F32),
                        pltpu.VMEM((2, hb, 2 * MOBA_BLOCK, MOBA_BLOCK), F32)],
        compiler_params=_params("parallel", "parallel", "arbitrary"),
        name="moba",
    )(proj3, proj3, proj3)


def _retention_kernel(lg_ref, q_ref, k_ref, v_ref, g_ref, cos_ref, sin_ref, gnw_ref, o_ref,
                      decay_ref, xi_ref, zeta_ref, st_ref, *, n_chunks, hb):
    c_len = RET_CHUNK
    dk, dv = HEAD_DIM_RET_QK, HEAD_DIM_RET_V
    ri = lax.broadcasted_iota(jnp.int32, (c_len, c_len), 0).astype(F32)
    ci = lax.broadcasted_iota(jnp.int32, (c_len, c_len), 1).astype(F32)
    diff = ri - ci
    idx = lax.broadcasted_iota(jnp.int32, (c_len, dk), 0).astype(F32)
    chunk_decay = []
    for hh in range(hb):
        lg = lg_ref[pl.program_id(1) * hb + hh]
        decay_ref[hh] = jnp.where(diff >= 0, jnp.exp(lg * jnp.maximum(diff, 0.0)), 0.0)
        xi_ref[hh] = jnp.exp(lg * (idx + 1.0))
        zeta_ref[hh] = jnp.exp(lg * (c_len - 1.0 - idx))
        chunk_decay.append(jnp.exp(jnp.zeros((1, dv), F32) + lg * c_len))
        st_ref[hh] = jnp.zeros((dk, dv), F32)
    k_scale = dk ** -0.5
    nt = (((1,), (1,)), ((), ()))

    def chunk(c, carry):
        r0 = pl.multiple_of(c * c_len, c_len)
        rows = pl.ds(r0, c_len)
        cs, sn = cos_ref[rows, :], sin_ref[rows, :]
        qbs, kbs, qxs, kzs = [], [], [], []
        for hh in range(hb):
            q = q_ref[rows, hh * dk:(hh + 1) * dk].astype(F32)
            k = k_ref[rows, hh * dk:(hh + 1) * dk].astype(F32)
            qr = q * cs + pltpu.roll(q, dk // 2, 1) * sn
            kr = (k * cs + pltpu.roll(k, dk // 2, 1) * sn) * k_scale
            qbs.append(qr.astype(BF16))
            kbs.append(kr.astype(BF16))
            qxs.append((qr * xi_ref[hh]).astype(BF16))
            kzs.append((kr * zeta_ref[hh]).T.astype(BF16))
        vs = [v_ref[rows, hh * dv:(hh + 1) * dv] for hh in range(hb)]
        scores = [lax.dot_general(qbs[hh], kbs[hh], nt, preferred_element_type=F32) for hh in range(hb)]
        cross = [jnp.dot(qxs[hh], st_ref[hh].astype(BF16), preferred_element_type=F32) for hh in range(hb)]
        upd = [jnp.dot(kzs[hh], vs[hh], preferred_element_type=F32) for hh in range(hb)]
        sd = [(scores[hh] * decay_ref[hh]).astype(BF16) for hh in range(hb)]
        inner = [jnp.dot(sd[hh], vs[hh], preferred_element_type=F32) for hh in range(hb)]
        for hh in range(hb):
            st_ref[hh] = chunk_decay[hh] * st_ref[hh] + upd[hh]
            o = inner[hh] + cross[hh]
            mu = jnp.mean(o, axis=-1, keepdims=True)
            var = jnp.mean(jnp.square(o - mu), axis=-1, keepdims=True)
            yn = (o - mu) * lax.rsqrt(var + GN_EPS)
            cols = slice(hh * dv, (hh + 1) * dv)
            g = g_ref[rows, cols].astype(F32)
            y = (g * jax.nn.sigmoid(g)) * (yn * gnw_ref[:, cols])
            o_ref[rows, cols] = y.astype(o_ref.dtype)
        return carry

    lax.fori_loop(0, n_chunks, chunk, 0)


def _retention(proj3, cos_t, sin_t, gn_w, log_gamma):
    b, s, _ = proj3.shape
    nh, dk, dv, hb = N_HEADS_RET, HEAD_DIM_RET_QK, HEAD_DIM_RET_V, RET_HEADS_PER_STEP
    ng = nh // hb
    q_blk = 3 * N_HEADS_MOBA * HEAD_DIM_MOBA // (hb * dk)
    k_blk = q_blk + ng
    v_blk = (k_blk + ng) * dk // dv
    g_blk = v_blk + ng
    grid_spec = pltpu.PrefetchScalarGridSpec(
        num_scalar_prefetch=1,
        grid=(b, ng),
        in_specs=[pl.BlockSpec((None, s, hb * dk), lambda bi, g, lg: (bi, 0, q_blk + g)),
                  pl.BlockSpec((None, s, hb * dk), lambda bi, g, lg: (bi, 0, k_blk + g)),
                  pl.BlockSpec((None, s, hb * dv), lambda bi, g, lg: (bi, 0, v_blk + g)),
                  pl.BlockSpec((None, s, hb * dv), lambda bi, g, lg: (bi, 0, g_blk + g)),
                  pl.BlockSpec((s, dk), lambda bi, g, lg: (0, 0)),
                  pl.BlockSpec((s, dk), lambda bi, g, lg: (0, 0)),
                  pl.BlockSpec((1, hb * dv), lambda bi, g, lg: (0, g))],
        out_specs=pl.BlockSpec((None, s, hb * dv), lambda bi, g, lg: (bi, 0, g)),
        scratch_shapes=[pltpu.VMEM((hb, RET_CHUNK, RET_CHUNK), F32), pltpu.VMEM((hb, RET_CHUNK, dk), F32),
                        pltpu.VMEM((hb, RET_CHUNK, dk), F32), pltpu.VMEM((hb, dk, dv), F32)],
    )
    return pl.pallas_call(
        functools.partial(_retention_kernel, n_chunks=s // RET_CHUNK, hb=hb),
        grid_spec=grid_spec,
        out_shape=jax.ShapeDtypeStruct((b, s, nh * dv), BF16),
        compiler_params=_params("parallel", "parallel"),
        name="retention",
    )(log_gamma, proj3, proj3, proj3, proj3, cos_t, sin_t, gn_w.reshape(1, nh * dv))


def _merge_kernel(oa_ref, yr_ref, ga_ref, gr_ref, wa_ref, wr_ref, o_ref):
    a = jnp.dot(oa_ref[...], wa_ref[...], preferred_element_type=F32)
    r = jnp.dot(yr_ref[...], wr_ref[...], preferred_element_type=F32)
    mix = jax.nn.sigmoid(ga_ref[...].astype(F32)) * a + jax.nn.sigmoid(gr_ref[...].astype(F32)) * r
    o_ref[...] = mix.astype(o_ref.dtype)


def _merge(o_a, y_r, proj, w_a, w_r):
    t, wa_in = o_a.shape
    wr_in = y_r.shape[1]
    d = w_a.shape[1]
    tm, tn = min(MERGE_TM, t), MERGE_TN
    ga_blk = (proj.shape[1] - 2 * d) // tn
    gr_blk = ga_blk + d // tn
    return pl.pallas_call(
        _merge_kernel,
        grid=(d // tn, t // tm),
        in_specs=[pl.BlockSpec((tm, wa_in), lambda j, i: (i, 0)),
                  pl.BlockSpec((tm, wr_in), lambda j, i: (i, 0)),
                  pl.BlockSpec((tm, tn), lambda j, i: (i, ga_blk + j)),
                  pl.BlockSpec((tm, tn), lambda j, i: (i, gr_blk + j)),
                  pl.BlockSpec((wa_in, tn), lambda j, i: (0, j)),
                  pl.BlockSpec((wr_in, tn), lambda j, i: (0, j))],
        out_specs=pl.BlockSpec((tm, tn), lambda j, i: (i, j)),
        out_shape=jax.ShapeDtypeStruct((t, d), BF16),
        compiler_params=_params("parallel", "parallel"),
        name="merge",
    )(o_a, y_r, proj, proj, w_a, w_r)


def _outproj_kernel(x_ref, mix_ref, wo_ref, nw_ref, wr_ref, br_ref, x1_ref, h_ref, route_ref):
    tm = x_ref.shape[0]
    sub = tm // OUTPROJ_SUB
    blocks = [pl.ds(k * sub, sub) for k in range(OUTPROJ_SUB)]
    proj = [jnp.dot(mix_ref[rows, :], wo_ref[...], preferred_element_type=F32) for rows in blocks]
    parts = []
    for rows, pr in zip(blocks, proj):
        x1 = x_ref[rows, :] + pr
        x1_ref[rows, :] = x1
        ms = jnp.mean(x1 * x1, axis=-1, keepdims=True)
        h = (x1 * lax.rsqrt(ms + RMS_EPS)) * nw_ref[...]
        h_ref[rows, :] = h
        h_hi = h.astype(BF16)
        h_lo = (h - h_hi.astype(F32)).astype(BF16)
        r = (jnp.dot(h_hi, wr_ref[...], preferred_element_type=F32)
             + jnp.dot(h_lo, wr_ref[...], preferred_element_type=F32))
        parts.append((r[:, :V7X_LANES] + r[:, V7X_LANES:]).T[:ROUTER_ROWS, :])
    logits = jnp.concatenate(parts, axis=1) + br_ref[...]
    row = lax.broadcasted_iota(jnp.int32, logits.shape, 0)
    ninf = -jnp.inf
    big = ROUTER_ROWS
    gl = jnp.where(row < N_GROUPS, logits, ninf)
    gmax = jnp.max(gl, axis=0, keepdims=True)
    gsum = jnp.sum(jnp.exp(gl - gmax), axis=0, keepdims=True)
    gidx = jnp.min(jnp.where(gl == gmax, row, big), axis=0, keepdims=True)
    g_weight = 1.0 / gsum
    lo = ROUTER_EXPERT_ROW + EXPERTS_PER_GROUP * gidx
    in_group = (row >= lo) & (row < lo + EXPERTS_PER_GROUP)
    el = jnp.where(in_group, logits, ninf)
    emax = jnp.max(el, axis=0, keepdims=True)
    i1 = jnp.min(jnp.where(el == emax, row, big), axis=0, keepdims=True)
    el2 = jnp.where(row == i1, ninf, el)
    emax2 = jnp.max(el2, axis=0, keepdims=True)
    i2 = jnp.min(jnp.where(el2 == emax2, row, big), axis=0, keepdims=True)
    esum = jnp.sum(jnp.exp(el - emax), axis=0, keepdims=True)
    p1 = 1.0 / esum
    p2 = jnp.exp(emax2 - emax) / esum
    c1 = g_weight * (p1 / (p1 + p2))
    c2 = g_weight * (p2 / (p1 + p2))
    e1 = (i1 - ROUTER_EXPERT_ROW).astype(F32)
    e2 = (i2 - ROUTER_EXPERT_ROW).astype(F32)
    orow = lax.broadcasted_iota(jnp.int32, route_ref.shape, 0)
    route_ref[...] = jnp.where(orow == 0, c1, jnp.where(orow == 1, c2, jnp.where(orow == 2, e1,
                               jnp.where(orow == 3, e2, 0.0))))


def _outproj(x2, mix, w_out, norm_w, w_router_t, b_router_t):
    t, d = x2.shape
    tm = min(OUTPROJ_TM, t)
    row = lambda i: (i, 0)
    const = lambda i: (0, 0)
    return pl.pallas_call(
        _outproj_kernel,
        grid=(t // tm,),
        in_specs=[pl.BlockSpec((tm, d), row),
                  pl.BlockSpec((tm, d), row),
                  pl.BlockSpec((d, d), const),
                  pl.BlockSpec((1, d), const),
                  pl.BlockSpec((d, 2 * V7X_LANES), const),
                  pl.BlockSpec((ROUTER_ROWS, 1), const)],
        out_specs=[pl.BlockSpec((tm, d), row),
                   pl.BlockSpec((tm, d), row),
                   pl.BlockSpec((ROUTE_OUT_ROWS, tm), lambda i: (0, i))],
        out_shape=[jax.ShapeDtypeStruct((t, d), F32),
                   jax.ShapeDtypeStruct((t, d), F32),
                   jax.ShapeDtypeStruct((ROUTE_OUT_ROWS, t), F32)],
        compiler_params=_params("parallel"),
        name="outproj",
    )(x2, mix, w_out, norm_w.reshape(1, d), w_router_t, b_router_t)


def _router_params(w_rg, b_rg, w_re, b_re):
    d = w_rg.shape[0]
    gap = ROUTER_EXPERT_ROW - N_GROUPS
    w = jnp.concatenate([w_rg, jnp.zeros((d, gap), F32),
                         jnp.transpose(w_re, (1, 0, 2)).reshape(d, N_EXPERTS),
                         jnp.zeros((d, V7X_LANES - ROUTER_ROWS), F32)], axis=1)
    w_hi = w.astype(BF16)
    w_lo = (w - w_hi.astype(F32)).astype(BF16)
    b_t = jnp.concatenate([b_rg, jnp.zeros((gap,), F32), b_re.reshape(N_EXPERTS)])
    return jnp.concatenate([w_hi, w_lo], axis=1), b_t.reshape(ROUTER_ROWS, 1)


def _dispatch_kernel(nt_ref, rt_ref, h_hbm, x_ref, xbuf, sem):
    i = pl.program_id(0)
    n_tiles = nt_ref[0]
    tm = xbuf.shape[1]
    slot = lax.rem(i, 2)

    def start_gather(tile, dst_slot, unroll):
        base = tile * tm

        def start(r, c):
            tok = rt_ref[base + r]
            pltpu.make_async_copy(h_hbm.at[pl.ds(tok, 1), :], xbuf.at[dst_slot, pl.ds(r, 1), :],
                                  sem.at[dst_slot]).start()
            return c

        lax.fori_loop(0, tm, start, 0, unroll=unroll)

    @pl.when(i == 0)
    def _():
        start_gather(0, 0, unroll=8)

    @pl.when(i < n_tiles)
    def _():
        @pl.when(i + 1 < n_tiles)
        def _():
            start_gather(i + 1, 1 - slot, unroll=True)

        pltpu.make_async_copy(h_hbm.at[pl.ds(0, tm), :], xbuf.at[slot], sem.at[slot]).wait()
        x_ref[...] = xbuf[slot].astype(x_ref.dtype)

    @pl.when(i >= n_tiles)
    def _():
        x_ref[...] = jnp.zeros_like(x_ref)


def _dispatch(h, n_tiles, row_token, max_tiles):
    t, d = h.shape
    tm = MOE_TM
    grid_spec = pltpu.PrefetchScalarGridSpec(
        num_scalar_prefetch=2,
        grid=(max_tiles,),
        in_specs=[pl.BlockSpec(memory_space=pl.ANY)],
        out_specs=pl.BlockSpec((tm, d), lambda i, nt, rt: (i, 0)),
        scratch_shapes=[pltpu.VMEM((2, tm, d), F32), pltpu.SemaphoreType.DMA((2,))],
    )
    return pl.pallas_call(
        _dispatch_kernel,
        grid_spec=grid_spec,
        out_shape=jax.ShapeDtypeStruct((max_tiles * tm, d), BF16),
        compiler_params=_params("arbitrary"),
        name="dispatch",
    )(n_tiles, row_token, h)


def _experts_kernel(te_ref, nx_ref, nt_ref, x_ref, wg_hbm, wu_hbm, wd_hbm, y_ref,
                    wgf, wuf, wdf, wgb, wub, wdb, wsem):
    i = pl.program_id(0)
    n_tiles = nt_ref[0]

    def weight_copies(e):
        return (pltpu.make_async_copy(wg_hbm.at[e], wgf, wsem.at[0]),
                pltpu.make_async_copy(wu_hbm.at[e], wuf, wsem.at[1]),
                pltpu.make_async_copy(wd_hbm.at[e], wdf, wsem.at[2]))

    @pl.when(i == 0)
    def _():
        for c in weight_copies(te_ref[0]):
            c.start()

    @pl.when(i < n_tiles)
    def _():
        @pl.when((i == 0) | (te_ref[i] != te_ref[jnp.maximum(i - 1, 0)]))
        def _():
            for c in weight_copies(te_ref[i]):
                c.wait()
            wgb[...] = wgf[...].astype(BF16)
            wub[...] = wuf[...].astype(BF16)
            wdb[...] = wdf[...].astype(BF16)

            @pl.when(nx_ref[i] >= 0)
            def _():
                for c in weight_copies(nx_ref[i]):
                    c.start()

        x = x_ref[...]
        gate = jnp.dot(x, wgb[...], preferred_element_type=F32)
        up = jnp.dot(x, wub[...], preferred_element_type=F32)
        act = (gate * jax.nn.sigmoid(gate)) * up
        y_ref[...] = jnp.dot(act.astype(BF16), wdb[...], preferred_element_type=F32)

    @pl.when(i >= n_tiles)
    def _():
        y_ref[...] = jnp.zeros_like(y_ref)


def _experts(x_sorted, tile_expert, next_expert, n_tiles, w_gate, w_up, w_down):
    d = x_sorted.shape[1]
    f = w_gate.shape[2]
    tm = MOE_TM
    max_tiles = tile_expert.shape[0]
    any_spec = pl.BlockSpec(memory_space=pl.ANY)
    grid_spec = pltpu.PrefetchScalarGridSpec(
        num_scalar_prefetch=3,
        grid=(max_tiles,),
        in_specs=[pl.BlockSpec((tm, d), lambda i, te, nx, nt: (jnp.minimum(i, nt[0] - 1), 0)),
                  any_spec, any_spec, any_spec],
        out_specs=pl.BlockSpec((tm, d), lambda i, te, nx, nt: (i, 0)),
        scratch_shapes=[pltpu.VMEM((d, f), F32), pltpu.VMEM((d, f), F32), pltpu.VMEM((f, d), F32),
                        pltpu.VMEM((d, f), BF16), pltpu.VMEM((d, f), BF16), pltpu.VMEM((f, d), BF16),
                        pltpu.SemaphoreType.DMA((3,))],
    )
    return pl.pallas_call(
        _experts_kernel,
        grid_spec=grid_spec,
        out_shape=jax.ShapeDtypeStruct((max_tiles * tm, d), F32),
        compiler_params=_params("arbitrary"),
        name="experts",
    )(tile_expert, next_expert, n_tiles, x_sorted, w_gate, w_up, w_down)


def _combine_kernel(pos_ref, x_ref, cw_ref, y_hbm, nw_ref, o_ref, ybuf, sem, *, final_norm):
    i = pl.program_id(0)
    tm = x_ref.shape[0]
    slot = lax.rem(i, 2)

    def start_gather(tile, dst_slot, unroll):
        base = tile * (2 * tm)

        def start(r, c):
            for k in range(2):
                p = pos_ref[base + 2 * r + k]
                pltpu.make_async_copy(y_hbm.at[pl.ds(p, 1), :], ybuf.at[dst_slot, k, pl.ds(r, 1), :],
                                      sem.at[dst_slot]).start()
            return c

        lax.fori_loop(0, tm, start, 0, unroll=unroll)

    @pl.when(i == 0)
    def _():
        start_gather(0, 0, unroll=8)

    @pl.when(i + 1 < pl.num_programs(0))
    def _():
        start_gather(i + 1, 1 - slot, unroll=True)

    for k in range(2):
        pltpu.make_async_copy(y_hbm.at[pl.ds(0, tm), :], ybuf.at[slot, k], sem.at[slot]).wait()
    cw = cw_ref[...]
    x = x_ref[...] + (cw[:, 0:1] * ybuf[slot, 0] + cw[:, 1:2] * ybuf[slot, 1])
    if final_norm:
        ms = jnp.mean(x * x, axis=-1, keepdims=True)
        x = (x * lax.rsqrt(ms + RMS_EPS)) * nw_ref[...]
    o_ref[...] = x


def _combine(x1, cw, y_sorted, pos, norm_w, final_norm):
    t, d = x1.shape
    tm = min(COMBINE_TM, t)
    grid_spec = pltpu.PrefetchScalarGridSpec(
        num_scalar_prefetch=1,
        grid=(t // tm,),
        in_specs=[pl.BlockSpec((tm, d), lambda i, pos: (i, 0)),
                  pl.BlockSpec((tm, 2), lambda i, pos: (i, 0)),
                  pl.BlockSpec(memory_space=pl.ANY),
                  pl.BlockSpec((1, d), lambda i, pos: (0, 0))],
        out_specs=pl.BlockSpec((tm, d), lambda i, pos: (i, 0)),
        scratch_shapes=[pltpu.VMEM((2, 2, tm, d), F32), pltpu.SemaphoreType.DMA((2,))],
    )
    return pl.pallas_call(
        functools.partial(_combine_kernel, final_norm=final_norm),
        grid_spec=grid_spec,
        out_shape=jax.ShapeDtypeStruct((t, d), F32),
        compiler_params=_params("arbitrary"),
        name="combine",
    )(pos, x1, cw, y_sorted, norm_w.reshape(1, d))


def _rope_tables(s):
    half = HEAD_DIM_RET_QK // 2
    inv = ROPE_BASE ** (-jnp.arange(half, dtype=F32) / half)
    ang = jnp.arange(s).astype(F32)[:, None] * inv[None, :]
    cos, sin = jnp.cos(ang), jnp.sin(ang)
    return jnp.concatenate([cos, cos], axis=-1), jnp.concatenate([-sin, sin], axis=-1)


def _dispatch_plan(expert_ids, tm):
    t = expert_ids.shape[0]
    n_assign = 2 * t
    max_tiles = n_assign // tm + N_EXPERTS
    expert = expert_ids.astype(jnp.int32).reshape(n_assign)
    onehot = (expert[:, None] == jnp.arange(N_EXPERTS, dtype=jnp.int32)[None, :]).astype(jnp.int32)
    running = jnp.cumsum(onehot, axis=0)
    counts = running[-1]
    rank = jnp.sum((running - onehot) * onehot, axis=1)
    tiles_per = (counts + tm - 1) // tm
    tile_end = jnp.cumsum(tiles_per)
    row_start = (tile_end - tiles_per) * tm
    pos = row_start[expert] + rank
    n_tiles = tile_end[-1]
    tile_id = jnp.arange(max_tiles, dtype=jnp.int32)
    last = jnp.minimum(tile_id, n_tiles - 1)
    tile_expert = jnp.sum((tile_end[None, :] <= last[:, None]).astype(jnp.int32), axis=1)
    token = jnp.arange(n_assign, dtype=jnp.int32) // 2
    filler = jnp.arange(max_tiles * tm, dtype=jnp.int32) % t
    row_token = filler.at[pos].set(token)
    after = tile_end[tile_expert]
    next_expert = jnp.where(after < n_tiles, tile_expert[jnp.minimum(after, max_tiles - 1)], -1)
    return (tile_expert, next_expert.astype(jnp.int32), n_tiles.reshape(1).astype(jnp.int32), row_token,
            pos.astype(jnp.int32))


def kernel(x, norm_mix_w, w_in, ret_gn_w, w_branch_moba, w_branch_ret, w_out, norm_ffn_w, w_router_group, b_router_group, w_router_expert, b_router_expert, w_expert_gate, w_expert_up, w_expert_down, norm_final_w):
    b, s, d = x.shape
    t = b * s
    depth = w_in.shape[0]
    assert s % MOBA_BLOCK == 0 and s % RET_CHUNK == 0 and t % MOE_TM == 0
    cos_t, sin_t = _rope_tables(s)
    log_gamma = jnp.log(1.0 - 2.0 ** (-5.0 - jnp.arange(N_HEADS_RET, dtype=F32)))
    x2 = x.reshape(t, d)
    for l in range(depth):
        proj = _inproj(x2, norm_mix_w[l], w_in[l])
        proj3 = proj.reshape(b, s, proj.shape[1])
        o_a = _moba(proj3).reshape(t, -1)
        y_r = _retention(proj3, cos_t, sin_t, ret_gn_w[l], log_gamma).reshape(t, -1)
        mix = _merge(o_a, y_r, proj, w_branch_moba[l].astype(BF16), w_branch_ret[l].astype(BF16))
        w_router_t, b_router_t = _router_params(w_router_group[l], b_router_group[l],
                                                w_router_expert[l], b_router_expert[l])
        x1, h, route_t = _outproj(x2, mix, w_out[l].astype(BF16), norm_ffn_w[l], w_router_t, b_router_t)
        tile_expert, next_expert, n_tiles, row_token, pos = _dispatch_plan(route_t[2:4].T, MOE_TM)
        x_sorted = _dispatch(h, n_tiles, row_token, tile_expert.shape[0])
        y_sorted = _experts(x_sorted, tile_expert, next_expert, n_tiles,
                            w_expert_gate[l], w_expert_up[l], w_expert_down[l])
        x2 = _combine(x1, route_t[0:2].T, y_sorted, pos, norm_final_w, final_norm=(l == depth - 1))
    return x2.reshape(b, s, d)
```

```python
import functools

import jax
import jax.numpy as jnp
from jax import lax
from jax.experimental import pallas as pl
from jax.experimental.pallas import tpu as pltpu

F32 = jnp.float32
BF16 = jnp.bfloat16

V7X_LANES = 128
V7X_VMEM_LIMIT_BYTES = 56 * 1024 * 1024

N_HEADS_MOBA = 8
HEAD_DIM_MOBA = 128
MOBA_BLOCK = 256
MOBA_TOPK = 3
N_HEADS_RET = 8
HEAD_DIM_RET_QK = 128
HEAD_DIM_RET_V = 256
ROPE_BASE = 10000.0
N_GROUPS = 4
EXPERTS_PER_GROUP = 8
N_EXPERTS = N_GROUPS * EXPERTS_PER_GROUP
RMS_EPS = 1e-6
GN_EPS = 1e-6
NEG_BIG = -1e30

ROUTER_EXPERT_ROW = 8
ROUTER_ROWS = ROUTER_EXPERT_ROW + N_EXPERTS
ROUTE_OUT_ROWS = 8
MOBA_HEADS_PER_STEP = 4
MOBA_BIAS_ROWS = 16
MOBA_SUM_ROWS = 16
RET_CHUNK = 256
RET_HEADS_PER_STEP = 2
INPROJ_TM, INPROJ_TN = 1024, 1024
INPROJ_MSUB = 2
MERGE_TM, MERGE_TN = 512, 1024
OUTPROJ_TM = 512
OUTPROJ_SUB = 4
MOE_TM = 256
DISPATCH_CHUNK = 512
DISPATCH_UNROLL = 8
COMBINE_TM = 256


def _params(*semantics):
    return pltpu.CompilerParams(dimension_semantics=semantics,
                                vmem_limit_bytes=V7X_VMEM_LIMIT_BYTES)


def _inproj_kernel(x_ref, nw_ref, w_ref, o_ref, h_ref, wb_ref):
    j, m = pl.program_id(1), pl.program_id(2)

    @pl.when(j == 0)
    def _():
        x = x_ref[...]
        ms = jnp.mean(x * x, axis=-1, keepdims=True)
        h_ref[m] = ((x * lax.rsqrt(ms + RMS_EPS)) * nw_ref[...]).astype(BF16)

    @pl.when(m == 0)
    def _():
        wb_ref[...] = w_ref[...].astype(BF16)

    o_ref[...] = jnp.dot(h_ref[m], wb_ref[...], preferred_element_type=F32).astype(o_ref.dtype)


def _inproj(x2, norm_w, w_in):
    t, d = x2.shape
    n = w_in.shape[1]
    tm, tn = min(INPROJ_TM, t), INPROJ_TN
    msub = min(INPROJ_MSUB, t // tm)

    def x_map(p, j, m):
        return (jnp.where(j == 0, p * msub + m, p * msub + msub - 1), 0)

    return pl.pallas_call(
        _inproj_kernel,
        grid=(t // (tm * msub), n // tn, msub),
        in_specs=[pl.BlockSpec((tm, d), x_map),
                  pl.BlockSpec((1, d), lambda p, j, m: (0, 0)),
                  pl.BlockSpec((d, tn), lambda p, j, m: (0, j))],
        out_specs=pl.BlockSpec((tm, tn), lambda p, j, m: (p * msub + m, j)),
        out_shape=jax.ShapeDtypeStruct((t, n), BF16),
        scratch_shapes=[pltpu.VMEM((msub, tm, d), BF16), pltpu.VMEM((d, tn), BF16)],
        compiler_params=_params("parallel", "arbitrary", "arbitrary"),
        name="inproj",
    )(x2, norm_w.reshape(1, d), w_in)


def _moba_kernel(q_ref, k_ref, v_ref, o_ref, qta_ref, ka_ref, vta_ref, km_ref, acc_ref, sc_ref, *, nb, hb):
    blk, hd = MOBA_BLOCK, HEAD_DIM_MOBA
    i = pl.program_id(2)

    @pl.when(i == 0)
    def _():
        lane = lax.broadcasted_iota(jnp.int32, (blk, hd), 1)
        srow = lax.broadcasted_iota(jnp.int32, (MOBA_SUM_ROWS, blk), 0)
        for hh in range(hb):
            cols = slice(hh * hd, (hh + 1) * hd)
            for c in range(nb):
                rows = slice(c * blk, (c + 1) * blk)
                qta_ref[hh, 0:hd, rows] = q_ref[rows, cols].astype(F32).T.astype(BF16)
                qta_ref[hh, hd:2 * hd, rows] = jnp.zeros((hd, blk), BF16)
                ka_ref[hh, rows, 0:hd] = k_ref[rows, cols]
                ka_ref[hh, rows, hd:2 * hd] = (lane == c).astype(BF16)
                vta_ref[hh, 0:hd, rows] = v_ref[rows, cols].astype(F32).T.astype(BF16)
                vta_ref[hh, hd:hd + MOBA_SUM_ROWS, rows] = (srow == 0).astype(BF16)
                km_ref[hh, c:c + 1, :] = (jnp.sum(k_ref[rows, cols].astype(F32), axis=0, keepdims=True)
                                          * (1.0 / blk))

    q0 = pl.multiple_of(i * blk, blk)
    qcols = pl.ds(q0, blk)
    c_exp = (hd ** -0.5) * 1.4426950408889634
    row = lax.broadcasted_iota(jnp.int32, (MOBA_BIAS_ROWS, blk), 0)
    kpos = lax.broadcasted_iota(jnp.int32, (blk, blk), 0)
    qpos = lax.broadcasted_iota(jnp.int32, (blk, blk), 1)

    qts = [qta_ref[hh, 0:hd, qcols] for hh in range(hb)]
    gates = [jnp.dot(km_ref[hh], qts[hh].astype(F32), precision=lax.Precision.HIGHEST,
                     preferred_element_type=F32) for hh in range(hb)]
    own = [jnp.dot(ka_ref[hh, qcols, 0:hd], qts[hh], preferred_element_type=F32)
           for hh in range(hb)]
    init, probs = [], []
    for hh in range(hb):
        gate = jnp.where(row < i, gates[hh], -jnp.inf)
        bias = jnp.full((MOBA_BIAS_ROWS, blk), NEG_BIG, F32)
        for _ in range(MOBA_TOPK):
            top = jnp.max(gate, axis=0, keepdims=True)
            is_top = (gate == top) & (top > -jnp.inf)
            first = jnp.min(jnp.where(is_top, row, MOBA_BIAS_ROWS), axis=0, keepdims=True)
            pick = row == first
            bias = jnp.where(pick, 0.0, bias)
            gate = jnp.where(pick, -jnp.inf, gate)
        qta_ref[hh, hd:hd + MOBA_BIAS_ROWS, qcols] = bias.astype(BF16)
        s = jnp.where(kpos <= qpos, own[hh] * c_exp, NEG_BIG)
        m0 = jnp.max(s, axis=0, keepdims=True)
        init.append(m0)
        probs.append(jnp.exp2(s - m0).astype(BF16))
    for hh in range(hb):
        acc_ref[hh] = jnp.dot(vta_ref[hh, :, qcols], probs[hh], preferred_element_type=F32)

    n_trips = (i + 1) // 2

    def key_rows(t):
        return pl.ds(pl.multiple_of(t * (2 * blk), 2 * blk), 2 * blk)

    def score_dots(t):
        return [jnp.dot(ka_ref[hh, key_rows(t), :], qta_ref[hh, :, qcols], preferred_element_type=F32)
                for hh in range(hb)]

    def fold(t, slot, ms):
        probs, out = [], []
        for hh in range(hb):
            s = sc_ref[slot, hh] * c_exp
            m_new = jnp.maximum(ms[hh], jnp.max(s, axis=0, keepdims=True))
            probs.append((jnp.exp2(ms[hh] - m_new), jnp.exp2(s - m_new).astype(BF16)))
            out.append(m_new)
        for hh in range(hb):
            alpha, p = probs[hh]
            acc_ref[hh] = alpha * acc_ref[hh] + jnp.dot(vta_ref[hh, :, key_rows(t)], p,
                                                        preferred_element_type=F32)
        return tuple(out)

    def scores_to(t, slot):
        for hh, s in enumerate(score_dots(t)):
            sc_ref[slot, hh] = s

    @pl.when(n_trips > 0)
    def _():
        scores_to(0, 0)

    def two_trips(u, ms):
        t = 2 * u
        scores_to(t + 1, 1)
        ms = fold(t, 0, ms)
        scores_to(t + 2, 0)
        return fold(t + 1, 1, ms)

    n_loop = jnp.maximum(n_trips - 1, 0) // 2
    ms = lax.fori_loop(0, n_loop, two_trips, tuple(init))
    t_tail = 2 * n_loop

    @pl.when(n_trips - t_tail == 2)
    def _():
        scores_to(t_tail + 1, 1)
        fold(t_tail + 1, 1, fold(t_tail, 0, ms))

    @pl.when(n_trips - t_tail == 1)
    def _():
        fold(t_tail, 0, ms)

    for hh in range(hb):
        acc = acc_ref[hh]
        o_ref[:, hh * hd:(hh + 1) * hd] = (acc[0:hd, :] / acc[hd:hd + 1, :]).T.astype(o_ref.dtype)


def _moba(proj3):
    b, s, _ = proj3.shape
    nb = s // MOBA_BLOCK
    assert nb % 2 == 0 and nb <= MOBA_BIAS_ROWS
    hd, nh, hb = HEAD_DIM_MOBA, N_HEADS_MOBA, MOBA_HEADS_PER_STEP
    ng = nh // hb
    return pl.pallas_call(
        functools.partial(_moba_kernel, nb=nb, hb=hb),
        grid=(b, ng, nb),
        in_specs=[pl.BlockSpec((None, s, hb * hd), lambda bi, g, i: (bi, 0, g)),
                  pl.BlockSpec((None, s, hb * hd), lambda bi, g, i: (bi, 0, ng + g)),
                  pl.BlockSpec((None, s, hb * hd), lambda bi, g, i: (bi, 0, 2 * ng + g))],
        out_specs=pl.BlockSpec((None, MOBA_BLOCK, hb * hd), lambda bi, g, i: (bi, i, g)),
        out_shape=jax.ShapeDtypeStruct((b, s, nh * hd), BF16),
        scratch_shapes=[pltpu.VMEM((hb, 2 * hd, s), BF16), pltpu.VMEM((hb, s, 2 * hd), BF16),
                        pltpu.VMEM((hb, hd + MOBA_SUM_ROWS, s), BF16),
                        pltpu.VMEM((hb, MOBA_BIAS_ROWS, hd), F32),
                        pltpu.VMEM((hb, hd + MOBA_SUM_ROWS, MOBA_BLOCK), F32),
                        pltpu.VMEM((2, hb, 2 * MOBA_BLOCK, MOBA_BLOCK), F32)],
        compiler_params=_params("parallel", "parallel", "arbitrary"),
        name="moba",
    )(proj3, proj3, proj3)


def _retention_kernel(lg_ref, q_ref, k_ref, v_ref, g_ref, cos_ref, sin_ref, gnw_ref, o_ref,
                      decay_ref, xi_ref, zeta_ref, st_ref, *, n_chunks, hb):
    c_len = RET_CHUNK
    dk, dv = HEAD_DIM_RET_QK, HEAD_DIM_RET_V
    ri = lax.broadcasted_iota(jnp.int32, (c_len, c_len), 0).astype(F32)
    ci = lax.broadcasted_iota(jnp.int32, (c_len, c_len), 1).astype(F32)
    diff = ri - ci
    idx = lax.broadcasted_iota(jnp.int32, (c_len, dk), 0).astype(F32)
    chunk_decay = []
    for hh in range(hb):
        lg = lg_ref[pl.program_id(1) * hb + hh]
        decay_ref[hh] = jnp.where(diff >= 0, jnp.exp(lg * jnp.maximum(diff, 0.0)), 0.0)
        xi_ref[hh] = jnp.exp(lg * (idx + 1.0))
        zeta_ref[hh] = jnp.exp(lg * (c_len - 1.0 - idx))
        chunk_decay.append(jnp.exp(jnp.zeros((1, dv), F32) + lg * c_len))
        st_ref[hh] = jnp.zeros((dk, dv), F32)
    k_scale = dk ** -0.5
    nt = (((1,), (1,)), ((), ()))

    def chunk(c, carry):
        r0 = pl.multiple_of(c * c_len, c_len)
        rows = pl.ds(r0, c_len)
        cs, sn = cos_ref[rows, :], sin_ref[rows, :]
        qbs, kbs, qxs, kzs = [], [], [], []
        for hh in range(hb):
            q = q_ref[rows, hh * dk:(hh + 1) * dk].astype(F32)
            k = k_ref[rows, hh * dk:(hh + 1) * dk].astype(F32)
            qr = q * cs + pltpu.roll(q, dk // 2, 1) * sn
            kr = (k * cs + pltpu.roll(k, dk // 2, 1) * sn) * k_scale
            qbs.append(qr.astype(BF16))
            kbs.append(kr.astype(BF16))
            qxs.append((qr * xi_ref[hh]).astype(BF16))
            kzs.append((kr * zeta_ref[hh]).T.astype(BF16))
        vs = [v_ref[rows, hh * dv:(hh + 1) * dv] for hh in range(hb)]
        scores = [lax.dot_general(qbs[hh], kbs[hh], nt, preferred_element_type=F32) for hh in range(hb)]
        cross = [jnp.dot(qxs[hh], st_ref[hh].astype(BF16), preferred_element_type=F32) for hh in range(hb)]
        upd = [jnp.dot(kzs[hh], vs[hh], preferred_element_type=F32) for hh in range(hb)]
        sd = [(scores[hh] * decay_ref[hh]).astype(BF16) for hh in range(hb)]
        inner = [jnp.dot(sd[hh], vs[hh], preferred_element_type=F32) for hh in range(hb)]
        for hh in range(hb):
            st_ref[hh] = chunk_decay[hh] * st_ref[hh] + upd[hh]
            o = inner[hh] + cross[hh]
            mu = jnp.mean(o, axis=-1, keepdims=True)
            var = jnp.mean(jnp.square(o - mu), axis=-1, keepdims=True)
            yn = (o - mu) * lax.rsqrt(var + GN_EPS)
            cols = slice(hh * dv, (hh + 1) * dv)
            g = g_ref[rows, cols].astype(F32)
            y = (g * jax.nn.sigmoid(g)) * (yn * gnw_ref[:, cols])
            o_ref[rows, cols] = y.astype(o_ref.dtype)
        return carry

    lax.fori_loop(0, n_chunks, chunk, 0)


def _retention(proj3, cos_t, sin_t, gn_w, log_gamma):
    b, s, _ = proj3.shape
    nh, dk, dv, hb = N_HEADS_RET, HEAD_DIM_RET_QK, HEAD_DIM_RET_V, RET_HEADS_PER_STEP
    ng = nh // hb
    q_blk = 3 * N_HEADS_MOBA * HEAD_DIM_MOBA // (hb * dk)
    k_blk = q_blk + ng
    v_blk = (k_blk + ng) * dk // dv
    g_blk = v_blk + ng
    grid_spec = pltpu.PrefetchScalarGridSpec(
        num_scalar_prefetch=1,
        grid=(b, ng),
        in_specs=[pl.BlockSpec((None, s, hb * dk), lambda bi, g, lg: (bi, 0, q_blk + g)),
                  pl.BlockSpec((None, s, hb * dk), lambda bi, g, lg: (bi, 0, k_blk + g)),
                  pl.BlockSpec((None, s, hb * dv), lambda bi, g, lg: (bi, 0, v_blk + g)),
                  pl.BlockSpec((None, s, hb * dv), lambda bi, g, lg: (bi, 0, g_blk + g)),
                  pl.BlockSpec((s, dk), lambda bi, g, lg: (0, 0)),
                  pl.BlockSpec((s, dk), lambda bi, g, lg: (0, 0)),
                  pl.BlockSpec((1, hb * dv), lambda bi, g, lg: (0, g))],
        out_specs=pl.BlockSpec((None, s, hb * dv), lambda bi, g, lg: (bi, 0, g)),
        scratch_shapes=[pltpu.VMEM((hb, RET_CHUNK, RET_CHUNK), F32), pltpu.VMEM((hb, RET_CHUNK, dk), F32),
                        pltpu.VMEM((hb, RET_CHUNK, dk), F32), pltpu.VMEM((hb, dk, dv), F32)],
    )
    return pl.pallas_call(
        functools.partial(_retention_kernel, n_chunks=s // RET_CHUNK, hb=hb),
        grid_spec=grid_spec,
        out_shape=jax.ShapeDtypeStruct((b, s, nh * dv), BF16),
        compiler_params=_params("parallel", "parallel"),
        name="retention",
    )(log_gamma, proj3, proj3, proj3, proj3, cos_t, sin_t, gn_w.reshape(1, nh * dv))


def _merge_kernel(oa_ref, yr_ref, ga_ref, gr_ref, wa_ref, wr_ref, o_ref):
    a = jnp.dot(oa_ref[...], wa_ref[...], preferred_element_type=F32)
    r = jnp.dot(yr_ref[...], wr_ref[...], preferred_element_type=F32)
    mix = jax.nn.sigmoid(ga_ref[...].astype(F32)) * a + jax.nn.sigmoid(gr_ref[...].astype(F32)) * r
    o_ref[...] = mix.astype(o_ref.dtype)


def _merge(o_a, y_r, proj, w_a, w_r):
    t, wa_in = o_a.shape
    wr_in = y_r.shape[1]
    d = w_a.shape[1]
    tm, tn = min(MERGE_TM, t), MERGE_TN
    ga_blk = (proj.shape[1] - 2 * d) // tn
    gr_blk = ga_blk + d // tn
    return pl.pallas_call(
        _merge_kernel,
        grid=(d // tn, t // tm),
        in_specs=[pl.BlockSpec((tm, wa_in), lambda j, i: (i, 0)),
                  pl.BlockSpec((tm, wr_in), lambda j, i: (i, 0)),
                  pl.BlockSpec((tm, tn), lambda j, i: (i, ga_blk + j)),
                  pl.BlockSpec((tm, tn), lambda j, i: (i, gr_blk + j)),
                  pl.BlockSpec((wa_in, tn), lambda j, i: (0, j)),
                  pl.BlockSpec((wr_in, tn), lambda j, i: (0, j))],
        out_specs=pl.BlockSpec((tm, tn), lambda j, i: (i, j)),
        out_shape=jax.ShapeDtypeStruct((t, d), BF16),
        compiler_params=_params("parallel", "parallel"),
        name="merge",
    )(o_a, y_r, proj, proj, w_a, w_r)


def _outproj_kernel(x_ref, mix_ref, wo_ref, nw_ref, wr_ref, br_ref, x1_ref, h_ref, route_ref):
    tm = x_ref.shape[0]
    sub = tm // OUTPROJ_SUB
    blocks = [pl.ds(k * sub, sub) for k in range(OUTPROJ_SUB)]
    proj = [jnp.dot(mix_ref[rows, :], wo_ref[...], preferred_element_type=F32) for rows in blocks]
    parts = []
    for rows, pr in zip(blocks, proj):
        x1 = x_ref[rows, :] + pr
        x1_ref[rows, :] = x1
        ms = jnp.mean(x1 * x1, axis=-1, keepdims=True)
        h = (x1 * lax.rsqrt(ms + RMS_EPS)) * nw_ref[...]
        h_ref[rows, :] = h
        h_hi = h.astype(BF16)
        h_lo = (h - h_hi.astype(F32)).astype(BF16)
        r = (jnp.dot(h_hi, wr_ref[...], preferred_element_type=F32)
             + jnp.dot(h_lo, wr_ref[...], preferred_element_type=F32))
        parts.append((r[:, :V7X_LANES] + r[:, V7X_LANES:]).T[:ROUTER_ROWS, :])
    logits = jnp.concatenate(parts, axis=1) + br_ref[...]
    row = lax.broadcasted_iota(jnp.int32, logits.shape, 0)
    ninf = -jnp.inf
    big = ROUTER_ROWS
    gl = jnp.where(row < N_GROUPS, logits, ninf)
    gmax = jnp.max(gl, axis=0, keepdims=True)
    gsum = jnp.sum(jnp.exp(gl - gmax), axis=0, keepdims=True)
    gidx = jnp.min(jnp.where(gl == gmax, row, big), axis=0, keepdims=True)
    g_weight = 1.0 / gsum
    lo = ROUTER_EXPERT_ROW + EXPERTS_PER_GROUP * gidx
    in_group = (row >= lo) & (row < lo + EXPERTS_PER_GROUP)
    el = jnp.where(in_group, logits, ninf)
    emax = jnp.max(el, axis=0, keepdims=True)
    i1 = jnp.min(jnp.where(el == emax, row, big), axis=0, keepdims=True)
    el2 = jnp.where(row == i1, ninf, el)
    emax2 = jnp.max(el2, axis=0, keepdims=True)
    i2 = jnp.min(jnp.where(el2 == emax2, row, big), axis=0, keepdims=True)
    esum = jnp.sum(jnp.exp(el - emax), axis=0, keepdims=True)
    p1 = 1.0 / esum
    p2 = jnp.exp(emax2 - emax) / esum
    c1 = g_weight * (p1 / (p1 + p2))
    c2 = g_weight * (p2 / (p1 + p2))
    e1 = (i1 - ROUTER_EXPERT_ROW).astype(F32)
    e2 = (i2 - ROUTER_EXPERT_ROW).astype(F32)
    orow = lax.broadcasted_iota(jnp.int32, route_ref.shape, 0)
    route_ref[...] = jnp.where(orow == 0, c1, jnp.where(orow == 1, c2, jnp.where(orow == 2, e1,
                               jnp.where(orow == 3, e2, 0.0))))


def _outproj(x2, mix, w_out, norm_w, w_router_t, b_router_t):
    t, d = x2.shape
    tm = min(OUTPROJ_TM, t)
    row = lambda i: (i, 0)
    const = lambda i: (0, 0)
    return pl.pallas_call(
        _outproj_kernel,
        grid=(t // tm,),
        in_specs=[pl.BlockSpec((tm, d), row),
                  pl.BlockSpec((tm, d), row),
                  pl.BlockSpec((d, d), const),
                  pl.BlockSpec((1, d), const),
                  pl.BlockSpec((d, 2 * V7X_LANES), const),
                  pl.BlockSpec((ROUTER_ROWS, 1), const)],
        out_specs=[pl.BlockSpec((tm, d), row),
                   pl.BlockSpec((tm, d), row),
                   pl.BlockSpec((ROUTE_OUT_ROWS, tm), lambda i: (0, i))],
        out_shape=[jax.ShapeDtypeStruct((t, d), F32),
                   jax.ShapeDtypeStruct((t, d), F32),
                   jax.ShapeDtypeStruct((ROUTE_OUT_ROWS, t), F32)],
        compiler_params=_params("parallel"),
        name="outproj",
    )(x2, mix, w_out, norm_w.reshape(1, d), w_router_t, b_router_t)


def _router_params(w_rg, b_rg, w_re, b_re):
    d = w_rg.shape[0]
    gap = ROUTER_EXPERT_ROW - N_GROUPS
    w = jnp.concatenate([w_rg, jnp.zeros((d, gap), F32),
                         jnp.transpose(w_re, (1, 0, 2)).reshape(d, N_EXPERTS),
                         jnp.zeros((d, V7X_LANES - ROUTER_ROWS), F32)], axis=1)
    w_hi = w.astype(BF16)
    w_lo = (w - w_hi.astype(F32)).astype(BF16)
    b_t = jnp.concatenate([b_rg, jnp.zeros((gap,), F32), b_re.reshape(N_EXPERTS)])
    return jnp.concatenate([w_hi, w_lo], axis=1), b_t.reshape(ROUTER_ROWS, 1)


def _dispatch_kernel(pos_ref, lt_ref, h_hbm, x_hbm, zbuf, zsem, sem, *, tm, chunk):
    n_tok = h_hbm.shape[0]

    zbuf[...] = jnp.zeros_like(zbuf)

    def zero_copy(e):
        return pltpu.make_async_copy(zbuf, x_hbm.at[pl.ds(lt_ref[e] * tm, tm), :], zsem)

    for e in range(N_EXPERTS):
        @pl.when(lt_ref[e] >= 0)
        def _():
            zero_copy(e).start()
    for e in range(N_EXPERTS):
        @pl.when(lt_ref[e] >= 0)
        def _():
            zero_copy(e).wait()

    def drain(slot):
        for _ in range(2):
            pltpu.make_async_copy(h_hbm.at[pl.ds(0, chunk), :], x_hbm.at[pl.ds(0, chunk), :],
                                  sem.at[slot]).wait()

    def chunk_body(c, carry):
        slot = lax.rem(c, 2)

        def eight_tokens(g, carry2):
            t0 = pl.multiple_of(c * chunk + g * DISPATCH_UNROLL, DISPATCH_UNROLL)
            for u in range(DISPATCH_UNROLL):
                for k in range(2):
                    p = pos_ref[2 * (t0 + u) + k]
                    pltpu.make_async_copy(h_hbm.at[pl.ds(t0 + u, 1), :], x_hbm.at[pl.ds(p, 1), :],
                                          sem.at[slot]).start()
            return carry2

        lax.fori_loop(0, chunk // DISPATCH_UNROLL, eight_tokens, 0)

        @pl.when(c > 0)
        def _():
            drain(1 - slot)

        return carry

    n_chunks = n_tok // chunk
    lax.fori_loop(0, n_chunks, chunk_body, 0)
    drain((n_chunks - 1) % 2)


def _dispatch(h, pos, last_tile, max_tiles):
    t, d = h.shape
    tm = MOE_TM
    chunk = min(DISPATCH_CHUNK, t)
    grid_spec = pltpu.PrefetchScalarGridSpec(
        num_scalar_prefetch=2,
        grid=(1,),
        in_specs=[pl.BlockSpec(memory_space=pl.ANY)],
        out_specs=pl.BlockSpec(memory_space=pl.ANY),
        scratch_shapes=[pltpu.VMEM((tm, d), F32), pltpu.SemaphoreType.DMA, pltpu.SemaphoreType.DMA((2,))],
    )
    return pl.pallas_call(
        functools.partial(_dispatch_kernel, tm=tm, chunk=chunk),
        grid_spec=grid_spec,
        out_shape=jax.ShapeDtypeStruct((max_tiles * tm, d), F32),
        compiler_params=_params("arbitrary"),
        name="dispatch",
    )(pos, last_tile, h)


def _experts_kernel(te_ref, nx_ref, nt_ref, x_ref, wg_hbm, wu_hbm, wd_hbm, y_ref,
                    wgf, wuf, wdf, wgb, wub, wdb, wsem):
    i = pl.program_id(0)
    n_tiles = nt_ref[0]

    def weight_copies(e):
        return (pltpu.make_async_copy(wg_hbm.at[e], wgf, wsem.at[0]),
                pltpu.make_async_copy(wu_hbm.at[e], wuf, wsem.at[1]),
                pltpu.make_async_copy(wd_hbm.at[e], wdf, wsem.at[2]))

    @pl.when(i == 0)
    def _():
        for c in weight_copies(te_ref[0]):
            c.start()

    @pl.when(i < n_tiles)
    def _():
        @pl.when((i == 0) | (te_ref[i] != te_ref[jnp.maximum(i - 1, 0)]))
        def _():
            for c in weight_copies(te_ref[i]):
                c.wait()
            wgb[...] = wgf[...].astype(BF16)
            wub[...] = wuf[...].astype(BF16)
            wdb[...] = wdf[...].astype(BF16)

            @pl.when(nx_ref[i] >= 0)
            def _():
                for c in weight_copies(nx_ref[i]):
                    c.start()

        x = x_ref[...].astype(BF16)
        gate = jnp.dot(x, wgb[...], preferred_element_type=F32)
        up = jnp.dot(x, wub[...], preferred_element_type=F32)
        act = (gate * jax.nn.sigmoid(gate)) * up
        y_ref[...] = jnp.dot(act.astype(BF16), wdb[...], preferred_element_type=F32)

    @pl.when(i >= n_tiles)
    def _():
        y_ref[...] = jnp.zeros_like(y_ref)


def _experts(x_sorted, tile_expert, next_expert, n_tiles, w_gate, w_up, w_down):
    d = x_sorted.shape[1]
    f = w_gate.shape[2]
    tm = MOE_TM
    max_tiles = tile_expert.shape[0]
    any_spec = pl.BlockSpec(memory_space=pl.ANY)
    grid_spec = pltpu.PrefetchScalarGridSpec(
        num_scalar_prefetch=3,
        grid=(max_tiles,),
        in_specs=[pl.BlockSpec((tm, d), lambda i, te, nx, nt: (jnp.minimum(i, nt[0] - 1), 0)),
                  any_spec, any_spec, any_spec],
        out_specs=pl.BlockSpec((tm, d), lambda i, te, nx, nt: (i, 0)),
        scratch_shapes=[pltpu.VMEM((d, f), F32), pltpu.VMEM((d, f), F32), pltpu.VMEM((f, d), F32),
                        pltpu.VMEM((d, f), BF16), pltpu.VMEM((d, f), BF16), pltpu.VMEM((f, d), BF16),
                        pltpu.SemaphoreType.DMA((3,))],
    )
    return pl.pallas_call(
        _experts_kernel,
        grid_spec=grid_spec,
        out_shape=jax.ShapeDtypeStruct((max_tiles * tm, d), F32),
        compiler_params=_params("arbitrary"),
        name="experts",
    )(tile_expert, next_expert, n_tiles, x_sorted, w_gate, w_up, w_down)


def _combine_kernel(pos_ref, x_ref, cw_ref, y_hbm, nw_ref, o_ref, ybuf, sem, *, final_norm):
    i = pl.program_id(0)
    tm = x_ref.shape[0]
    slot = lax.rem(i, 2)

    def start_gather(tile, dst_slot, unroll):
        base = tile * (2 * tm)

        def start(r, c):
            for k in range(2):
                p = pos_ref[base + 2 * r + k]
                pltpu.make_async_copy(y_hbm.at[pl.ds(p, 1), :], ybuf.at[dst_slot, k, pl.ds(r, 1), :],
                                      sem.at[dst_slot]).start()
            return c

        lax.fori_loop(0, tm, start, 0, unroll=unroll)

    @pl.when(i == 0)
    def _():
        start_gather(0, 0, unroll=8)

    @pl.when(i + 1 < pl.num_programs(0))
    def _():
        start_gather(i + 1, 1 - slot, unroll=True)

    for k in range(2):
        pltpu.make_async_copy(y_hbm.at[pl.ds(0, tm), :], ybuf.at[slot, k], sem.at[slot]).wait()
    cw = cw_ref[...]
    x = x_ref[...] + (cw[:, 0:1] * ybuf[slot, 0] + cw[:, 1:2] * ybuf[slot, 1])
    if final_norm:
        ms = jnp.mean(x * x, axis=-1, keepdims=True)
        x = (x * lax.rsqrt(ms + RMS_EPS)) * nw_ref[...]
    o_ref[...] = x


def _combine(x1, cw, y_sorted, pos, norm_w, final_norm):
    t, d = x1.shape
    tm = min(COMBINE_TM, t)
    grid_spec = pltpu.PrefetchScalarGridSpec(
        num_scalar_prefetch=1,
        grid=(t // tm,),
        in_specs=[pl.BlockSpec((tm, d), lambda i, pos: (i, 0)),
                  pl.BlockSpec((tm, 2), lambda i, pos: (i, 0)),
                  pl.BlockSpec(memory_space=pl.ANY),
                  pl.BlockSpec((1, d), lambda i, pos: (0, 0))],
        out_specs=pl.BlockSpec((tm, d), lambda i, pos: (i, 0)),
        scratch_shapes=[pltpu.VMEM((2, 2, tm, d), F32), pltpu.SemaphoreType.DMA((2,))],
    )
    return pl.pallas_call(
        functools.partial(_combine_kernel, final_norm=final_norm),
        grid_spec=grid_spec,
        out_shape=jax.ShapeDtypeStruct((t, d), F32),
        compiler_params=_params("arbitrary"),
        name="combine",
    )(pos, x1, cw, y_sorted, norm_w.reshape(1, d))


def _rope_tables(s):
    half = HEAD_DIM_RET_QK // 2
    inv = ROPE_BASE ** (-jnp.arange(half, dtype=F32) / half)
    ang = jnp.arange(s).astype(F32)[:, None] * inv[None, :]
    cos, sin = jnp.cos(ang), jnp.sin(ang)
    return jnp.concatenate([cos, cos], axis=-1), jnp.concatenate([-sin, sin], axis=-1)


def _dispatch_plan(expert_ids, tm):
    t = expert_ids.shape[0]
    n_assign = 2 * t
    max_tiles = n_assign // tm + N_EXPERTS
    expert = expert_ids.astype(jnp.int32).reshape(n_assign)
    onehot = (expert[:, None] == jnp.arange(N_EXPERTS, dtype=jnp.int32)[None, :]).astype(jnp.int32)
    running = jnp.cumsum(onehot, axis=0)
    counts = running[-1]
    rank = jnp.sum((running - onehot) * onehot, axis=1)
    tiles_per = (counts + tm - 1) // tm
    tile_end = jnp.cumsum(tiles_per)
    row_start = (tile_end - tiles_per) * tm
    pos = row_start[expert] + rank
    n_tiles = tile_end[-1]
    tile_id = jnp.arange(max_tiles, dtype=jnp.int32)
    last = jnp.minimum(tile_id, n_tiles - 1)
    tile_expert = jnp.sum((tile_end[None, :] <= last[:, None]).astype(jnp.int32), axis=1)
    last_tile = jnp.where(tiles_per > 0, tile_end - 1, -1)
    after = tile_end[tile_expert]
    next_expert = jnp.where(after < n_tiles, tile_expert[jnp.minimum(after, max_tiles - 1)], -1)
    return (tile_expert, next_expert.astype(jnp.int32), n_tiles.reshape(1).astype(jnp.int32),
            last_tile.astype(jnp.int32), pos.astype(jnp.int32))


def kernel(x, norm_mix_w, w_in, ret_gn_w, w_branch_moba, w_branch_ret, w_out, norm_ffn_w, w_router_group, b_router_group, w_router_expert, b_router_expert, w_expert_gate, w_expert_up, w_expert_down, norm_final_w):
    b, s, d = x.shape
    t = b * s
    depth = w_in.shape[0]
    assert s % MOBA_BLOCK == 0 and s % RET_CHUNK == 0 and t % MOE_TM == 0
    cos_t, sin_t = _rope_tables(s)
    log_gamma = jnp.log(1.0 - 2.0 ** (-5.0 - jnp.arange(N_HEADS_RET, dtype=F32)))
    x2 = x.reshape(t, d)
    for l in range(depth):
        proj = _inproj(x2, norm_mix_w[l], w_in[l])
        proj3 = proj.reshape(b, s, proj.shape[1])
        o_a = _moba(proj3).reshape(t, -1)
        y_r = _retention(proj3, cos_t, sin_t, ret_gn_w[l], log_gamma).reshape(t, -1)
        mix = _merge(o_a, y_r, proj, w_branch_moba[l].astype(BF16), w_branch_ret[l].astype(BF16))
        w_router_t, b_router_t = _router_params(w_router_group[l], b_router_group[l],
                                                w_router_expert[l], b_router_expert[l])
        x1, h, route_t = _outproj(x2, mix, w_out[l].astype(BF16), norm_ffn_w[l], w_router_t, b_router_t)
        tile_expert, next_expert, n_tiles, last_tile, pos = _dispatch_plan(route_t[2:4].T, MOE_TM)
        x_sorted = _dispatch(h, pos, last_tile, tile_expert.shape[0])
        y_sorted = _experts(x_sorted, tile_expert, next_expert, n_tiles,
                            w_expert_gate[l], w_expert_up[l], w_expert_down[l])
        x2 = _combine(x1, route_t[0:2].T, y_sorted, pos, norm_final_w, final_norm=(l == depth - 1))
    return x2.reshape(b, s, d)
```

```python
import functools

import jax
import jax.numpy as jnp
from jax import lax
from jax.experimental import pallas as pl
from jax.experimental.pallas import tpu as pltpu

F32 = jnp.float32
BF16 = jnp.bfloat16

V7X_LANES = 128
V7X_VMEM_LIMIT_BYTES = 56 * 1024 * 1024

N_HEADS_MOBA = 8
HEAD_DIM_MOBA = 128
MOBA_BLOCK = 256
MOBA_TOPK = 3
N_HEADS_RET = 8
HEAD_DIM_RET_QK = 128
HEAD_DIM_RET_V = 256
ROPE_BASE = 10000.0
N_GROUPS = 4
EXPERTS_PER_GROUP = 8
N_EXPERTS = N_GROUPS * EXPERTS_PER_GROUP
RMS_EPS = 1e-6
GN_EPS = 1e-6
NEG_BIG = -1e30

ROUTER_EXPERT_ROW = 8
ROUTER_ROWS = ROUTER_EXPERT_ROW + N_EXPERTS
ROUTE_OUT_ROWS = 8
MOBA_HEADS_PER_STEP = 4
MOBA_BIAS_ROWS = 16
MOBA_SUM_ROWS = 16
RET_CHUNK = 256
RET_HEADS_PER_STEP = 2
INPROJ_TM, INPROJ_TN = 1024, 1024
INPROJ_MSUB = 2
MERGE_TM, MERGE_TN = 512, 1024
OUTPROJ_TM = 512
OUTPROJ_SUB = 4
MOE_TM = 256
DISPATCH_TT = 256
COMBINE_TM = 256


def _params(*semantics):
    return pltpu.CompilerParams(dimension_semantics=semantics,
                                vmem_limit_bytes=V7X_VMEM_LIMIT_BYTES)


def _inproj_kernel(x_ref, nw_ref, w_ref, o_ref, h_ref, wb_ref):
    j, m = pl.program_id(1), pl.program_id(2)

    @pl.when(j == 0)
    def _():
        x = x_ref[...]
        ms = jnp.mean(x * x, axis=-1, keepdims=True)
        h_ref[m] = ((x * lax.rsqrt(ms + RMS_EPS)) * nw_ref[...]).astype(BF16)

    @pl.when(m == 0)
    def _():
        wb_ref[...] = w_ref[...].astype(BF16)

    o_ref[...] = jnp.dot(h_ref[m], wb_ref[...], preferred_element_type=F32).astype(o_ref.dtype)


def _inproj(x2, norm_w, w_in):
    t, d = x2.shape
    n = w_in.shape[1]
    tm, tn = min(INPROJ_TM, t), INPROJ_TN
    msub = min(INPROJ_MSUB, t // tm)

    def x_map(p, j, m):
        return (jnp.where(j == 0, p * msub + m, p * msub + msub - 1), 0)

    return pl.pallas_call(
        _inproj_kernel,
        grid=(t // (tm * msub), n // tn, msub),
        in_specs=[pl.BlockSpec((tm, d), x_map),
                  pl.BlockSpec((1, d), lambda p, j, m: (0, 0)),
                  pl.BlockSpec((d, tn), lambda p, j, m: (0, j))],
        out_specs=pl.BlockSpec((tm, tn), lambda p, j, m: (p * msub + m, j)),
        out_shape=jax.ShapeDtypeStruct((t, n), BF16),
        scratch_shapes=[pltpu.VMEM((msub, tm, d), BF16), pltpu.VMEM((d, tn), BF16)],
        compiler_params=_params("parallel", "arbitrary", "arbitrary"),
        name="inproj",
    )(x2, norm_w.reshape(1, d), w_in)


def _moba_kernel(q_ref, k_ref, v_ref, o_ref, qta_ref, ka_ref, vta_ref, km_ref, acc_ref, sc_ref, *, nb, hb):
    blk, hd = MOBA_BLOCK, HEAD_DIM_MOBA
    i = pl.program_id(2)

    @pl.when(i == 0)
    def _():
        lane = lax.broadcasted_iota(jnp.int32, (blk, hd), 1)
        srow = lax.broadcasted_iota(jnp.int32, (MOBA_SUM_ROWS, blk), 0)
        for hh in range(hb):
            cols = slice(hh * hd, (hh + 1) * hd)
            for c in range(nb):
                rows = slice(c * blk, (c + 1) * blk)
                qta_ref[hh, 0:hd, rows] = q_ref[rows, cols].astype(F32).T.astype(BF16)
                qta_ref[hh, hd:2 * hd, rows] = jnp.zeros((hd, blk), BF16)
                ka_ref[hh, rows, 0:hd] = k_ref[rows, cols]
                ka_ref[hh, rows, hd:2 * hd] = (lane == c).astype(BF16)
                vta_ref[hh, 0:hd, rows] = v_ref[rows, cols].astype(F32).T.astype(BF16)
                vta_ref[hh, hd:hd + MOBA_SUM_ROWS, rows] = (srow == 0).astype(BF16)
                km_ref[hh, c:c + 1, :] = (jnp.sum(k_ref[rows, cols].astype(F32), axis=0, keepdims=True)
                                          * (1.0 / blk))

    q0 = pl.multiple_of(i * blk, blk)
    qcols = pl.ds(q0, blk)
    c_exp = (hd ** -0.5) * 1.4426950408889634
    row = lax.broadcasted_iota(jnp.int32, (MOBA_BIAS_ROWS, blk), 0)
    kpos = lax.broadcasted_iota(jnp.int32, (blk, blk), 0)
    qpos = lax.broadcasted_iota(jnp.int32, (blk, blk), 1)

    qts = [qta_ref[hh, 0:hd, qcols] for hh in range(hb)]
    gates = [jnp.dot(km_ref[hh], qts[hh].astype(F32), precision=lax.Precision.HIGHEST,
                     preferred_element_type=F32) for hh in range(hb)]
    own = [jnp.dot(ka_ref[hh, qcols, 0:hd], qts[hh], preferred_element_type=F32)
           for hh in range(hb)]
    init, probs = [], []
    for hh in range(hb):
        gate = jnp.where(row < i, gates[hh], -jnp.inf)
        bias = jnp.full((MOBA_BIAS_ROWS, blk), NEG_BIG, F32)
        for _ in range(MOBA_TOPK):
            top = jnp.max(gate, axis=0, keepdims=True)
            is_top = (gate == top) & (top > -jnp.inf)
            first = jnp.min(jnp.where(is_top, row, MOBA_BIAS_ROWS), axis=0, keepdims=True)
            pick = row == first
            bias = jnp.where(pick, 0.0, bias)
            gate = jnp.where(pick, -jnp.inf, gate)
        qta_ref[hh, hd:hd + MOBA_BIAS_ROWS, qcols] = bias.astype(BF16)
        s = jnp.where(kpos <= qpos, own[hh] * c_exp, NEG_BIG)
        m0 = jnp.max(s, axis=0, keepdims=True)
        init.append(m0)
        probs.append(jnp.exp2(s - m0).astype(BF16))
    for hh in range(hb):
        acc_ref[hh] = jnp.dot(vta_ref[hh, :, qcols], probs[hh], preferred_element_type=F32)

    n_trips = (i + 1) // 2

    def key_rows(t):
        return pl.ds(pl.multiple_of(t * (2 * blk), 2 * blk), 2 * blk)

    def score_dots(t):
        return [jnp.dot(ka_ref[hh, key_rows(t), :], qta_ref[hh, :, qcols], preferred_element_type=F32)
                for hh in range(hb)]

    def fold(t, slot, ms):
        probs, out = [], []
        for hh in range(hb):
            s = sc_ref[slot, hh] * c_exp
            m_new = jnp.maximum(ms[hh], jnp.max(s, axis=0, keepdims=True))
            probs.append((jnp.exp2(ms[hh] - m_new), jnp.exp2(s - m_new).astype(BF16)))
            out.append(m_new)
        for hh in range(hb):
            alpha, p = probs[hh]
            acc_ref[hh] = alpha * acc_ref[hh] + jnp.dot(vta_ref[hh, :, key_rows(t)], p,
                                                        preferred_element_type=F32)
        return tuple(out)

    def scores_to(t, slot):
        for hh, s in enumerate(score_dots(t)):
            sc_ref[slot, hh] = s

    @pl.when(n_trips > 0)
    def _():
        scores_to(0, 0)

    def two_trips(u, ms):
        t = 2 * u
        scores_to(t + 1, 1)
        ms = fold(t, 0, ms)
        scores_to(t + 2, 0)
        return fold(t + 1, 1, ms)

    n_loop = jnp.maximum(n_trips - 1, 0) // 2
    ms = lax.fori_loop(0, n_loop, two_trips, tuple(init))
    t_tail = 2 * n_loop

    @pl.when(n_trips - t_tail == 2)
    def _():
        scores_to(t_tail + 1, 1)
        fold(t_tail + 1, 1, fold(t_tail, 0, ms))

    @pl.when(n_trips - t_tail == 1)
    def _():
        fold(t_tail, 0, ms)

    for hh in range(hb):
        acc = acc_ref[hh]
        o_ref[:, hh * hd:(hh + 1) * hd] = (acc[0:hd, :] / acc[hd:hd + 1, :]).T.astype(o_ref.dtype)


def _moba(proj3):
    b, s, _ = proj3.shape
    nb = s // MOBA_BLOCK
    assert nb % 2 == 0 and nb <= MOBA_BIAS_ROWS
    hd, nh, hb = HEAD_DIM_MOBA, N_HEADS_MOBA, MOBA_HEADS_PER_STEP
    ng = nh // hb
    return pl.pallas_call(
        functools.partial(_moba_kernel, nb=nb, hb=hb),
        grid=(b, ng, nb),
        in_specs=[pl.BlockSpec((None, s, hb * hd), lambda bi, g, i: (bi, 0, g)),
                  pl.BlockSpec((None, s, hb * hd), lambda bi, g, i: (bi, 0, ng + g)),
                  pl.BlockSpec((None, s, hb * hd), lambda bi, g, i: (bi, 0, 2 * ng + g))],
        out_specs=pl.BlockSpec((None, MOBA_BLOCK, hb * hd), lambda bi, g, i: (bi, i, g)),
        out_shape=jax.ShapeDtypeStruct((b, s, nh * hd), BF16),
        scratch_shapes=[pltpu.VMEM((hb, 2 * hd, s), BF16), pltpu.VMEM((hb, s, 2 * hd), BF16),
                        pltpu.VMEM((hb, hd + MOBA_SUM_ROWS, s), BF16),
                        pltpu.VMEM((hb, MOBA_BIAS_ROWS, hd), F32),
                        pltpu.VMEM((hb, hd + MOBA_SUM_ROWS, MOBA_BLOCK), F32),
                        pltpu.VMEM((2, hb, 2 * MOBA_BLOCK, MOBA_BLOCK), F32)],
        compiler_params=_params("parallel", "parallel", "arbitrary"),
        name="moba",
    )(proj3, proj3, proj3)


def _retention_kernel(lg_ref, q_ref, k_ref, v_ref, g_ref, cos_ref, sin_ref, gnw_ref, o_ref,
                      decay_ref, xi_ref, zeta_ref, st_ref, *, n_chunks, hb):
    c_len = RET_CHUNK
    dk, dv = HEAD_DIM_RET_QK, HEAD_DIM_RET_V
    ri = lax.broadcasted_iota(jnp.int32, (c_len, c_len), 0).astype(F32)
    ci = lax.broadcasted_iota(jnp.int32, (c_len, c_len), 1).astype(F32)
    diff = ri - ci
    idx = lax.broadcasted_iota(jnp.int32, (c_len, dk), 0).astype(F32)
    chunk_decay = []
    for hh in range(hb):
        lg = lg_ref[pl.program_id(1) * hb + hh]
        decay_ref[hh] = jnp.where(diff >= 0, jnp.exp(lg * jnp.maximum(diff, 0.0)), 0.0)
        xi_ref[hh] = jnp.exp(lg * (idx + 1.0))
        zeta_ref[hh] = jnp.exp(lg * (c_len - 1.0 - idx))
        chunk_decay.append(jnp.exp(jnp.zeros((1, dv), F32) + lg * c_len))
        st_ref[hh] = jnp.zeros((dk, dv), F32)
    k_scale = dk ** -0.5
    nt = (((1,), (1,)), ((), ()))

    def chunk(c, carry):
        r0 = pl.multiple_of(c * c_len, c_len)
        rows = pl.ds(r0, c_len)
        cs, sn = cos_ref[rows, :], sin_ref[rows, :]
        qbs, kbs, qxs, kzs = [], [], [], []
        for hh in range(hb):
            q = q_ref[rows, hh * dk:(hh + 1) * dk].astype(F32)
            k = k_ref[rows, hh * dk:(hh + 1) * dk].astype(F32)
            qr = q * cs + pltpu.roll(q, dk // 2, 1) * sn
            kr = (k * cs + pltpu.roll(k, dk // 2, 1) * sn) * k_scale
            qbs.append(qr.astype(BF16))
            kbs.append(kr.astype(BF16))
            qxs.append((qr * xi_ref[hh]).astype(BF16))
            kzs.append((kr * zeta_ref[hh]).T.astype(BF16))
        vs = [v_ref[rows, hh * dv:(hh + 1) * dv] for hh in range(hb)]
        scores = [lax.dot_general(qbs[hh], kbs[hh], nt, preferred_element_type=F32) for hh in range(hb)]
        cross = [jnp.dot(qxs[hh], st_ref[hh].astype(BF16), preferred_element_type=F32) for hh in range(hb)]
        upd = [jnp.dot(kzs[hh], vs[hh], preferred_element_type=F32) for hh in range(hb)]
        sd = [(scores[hh] * decay_ref[hh]).astype(BF16) for hh in range(hb)]
        inner = [jnp.dot(sd[hh], vs[hh], preferred_element_type=F32) for hh in range(hb)]
        for hh in range(hb):
            st_ref[hh] = chunk_decay[hh] * st_ref[hh] + upd[hh]
            o = inner[hh] + cross[hh]
            mu = jnp.mean(o, axis=-1, keepdims=True)
            var = jnp.mean(jnp.square(o - mu), axis=-1, keepdims=True)
            yn = (o - mu) * lax.rsqrt(var + GN_EPS)
            cols = slice(hh * dv, (hh + 1) * dv)
            g = g_ref[rows, cols].astype(F32)
            y = (g * jax.nn.sigmoid(g)) * (yn * gnw_ref[:, cols])
            o_ref[rows, cols] = y.astype(o_ref.dtype)
        return carry

    lax.fori_loop(0, n_chunks, chunk, 0)


def _retention(proj3, cos_t, sin_t, gn_w, log_gamma):
    b, s, _ = proj3.shape
    nh, dk, dv, hb = N_HEADS_RET, HEAD_DIM_RET_QK, HEAD_DIM_RET_V, RET_HEADS_PER_STEP
    ng = nh // hb
    q_blk = 3 * N_HEADS_MOBA * HEAD_DIM_MOBA // (hb * dk)
    k_blk = q_blk + ng
    v_blk = (k_blk + ng) * dk // dv
    g_blk = v_blk + ng
    grid_spec = pltpu.PrefetchScalarGridSpec(
        num_scalar_prefetch=1,
        grid=(b, ng),
        in_specs=[pl.BlockSpec((None, s, hb * dk), lambda bi, g, lg: (bi, 0, q_blk + g)),
                  pl.BlockSpec((None, s, hb * dk), lambda bi, g, lg: (bi, 0, k_blk + g)),
                  pl.BlockSpec((None, s, hb * dv), lambda bi, g, lg: (bi, 0, v_blk + g)),
                  pl.BlockSpec((None, s, hb * dv), lambda bi, g, lg: (bi, 0, g_blk + g)),
                  pl.BlockSpec((s, dk), lambda bi, g, lg: (0, 0)),
                  pl.BlockSpec((s, dk), lambda bi, g, lg: (0, 0)),
                  pl.BlockSpec((1, hb * dv), lambda bi, g, lg: (0, g))],
        out_specs=pl.BlockSpec((None, s, hb * dv), lambda bi, g, lg: (bi, 0, g)),
        scratch_shapes=[pltpu.VMEM((hb, RET_CHUNK, RET_CHUNK), F32), pltpu.VMEM((hb, RET_CHUNK, dk), F32),
                        pltpu.VMEM((hb, RET_CHUNK, dk), F32), pltpu.VMEM((hb, dk, dv), F32)],
    )
    return pl.pallas_call(
        functools.partial(_retention_kernel, n_chunks=s // RET_CHUNK, hb=hb),
        grid_spec=grid_spec,
        out_shape=jax.ShapeDtypeStruct((b, s, nh * dv), BF16),
        compiler_params=_params("parallel", "parallel"),
        name="retention",
    )(log_gamma, proj3, proj3, proj3, proj3, cos_t, sin_t, gn_w.reshape(1, nh * dv))


def _merge_kernel(oa_ref, yr_ref, ga_ref, gr_ref, wa_ref, wr_ref, o_ref):
    a = jnp.dot(oa_ref[...], wa_ref[...], preferred_element_type=F32)
    r = jnp.dot(yr_ref[...], wr_ref[...], preferred_element_type=F32)
    mix = jax.nn.sigmoid(ga_ref[...].astype(F32)) * a + jax.nn.sigmoid(gr_ref[...].astype(F32)) * r
    o_ref[...] = mix.astype(o_ref.dtype)


def _merge(o_a, y_r, proj, w_a, w_r):
    t, wa_in = o_a.shape
    wr_in = y_r.shape[1]
    d = w_a.shape[1]
    tm, tn = min(MERGE_TM, t), MERGE_TN
    ga_blk = (proj.shape[1] - 2 * d) // tn
    gr_blk = ga_blk + d // tn
    return pl.pallas_call(
        _merge_kernel,
        grid=(d // tn, t // tm),
        in_specs=[pl.BlockSpec((tm, wa_in), lambda j, i: (i, 0)),
                  pl.BlockSpec((tm, wr_in), lambda j, i: (i, 0)),
                  pl.BlockSpec((tm, tn), lambda j, i: (i, ga_blk + j)),
                  pl.BlockSpec((tm, tn), lambda j, i: (i, gr_blk + j)),
                  pl.BlockSpec((wa_in, tn), lambda j, i: (0, j)),
                  pl.BlockSpec((wr_in, tn), lambda j, i: (0, j))],
        out_specs=pl.BlockSpec((tm, tn), lambda j, i: (i, j)),
        out_shape=jax.ShapeDtypeStruct((t, d), BF16),
        compiler_params=_params("parallel", "parallel"),
        name="merge",
    )(o_a, y_r, proj, proj, w_a, w_r)


def _outproj_kernel(x_ref, mix_ref, wo_ref, nw_ref, wr_ref, br_ref, x1_ref, h_ref, route_ref):
    tm = x_ref.shape[0]
    sub = tm // OUTPROJ_SUB
    blocks = [pl.ds(k * sub, sub) for k in range(OUTPROJ_SUB)]
    proj = [jnp.dot(mix_ref[rows, :], wo_ref[...], preferred_element_type=F32) for rows in blocks]
    parts = []
    for rows, pr in zip(blocks, proj):
        x1 = x_ref[rows, :] + pr
        x1_ref[rows, :] = x1
        ms = jnp.mean(x1 * x1, axis=-1, keepdims=True)
        h = (x1 * lax.rsqrt(ms + RMS_EPS)) * nw_ref[...]
        h_ref[rows, :] = h
        h_hi = h.astype(BF16)
        h_lo = (h - h_hi.astype(F32)).astype(BF16)
        r = (jnp.dot(h_hi, wr_ref[...], preferred_element_type=F32)
             + jnp.dot(h_lo, wr_ref[...], preferred_element_type=F32))
        parts.append((r[:, :V7X_LANES] + r[:, V7X_LANES:]).T[:ROUTER_ROWS, :])
    logits = jnp.concatenate(parts, axis=1) + br_ref[...]
    row = lax.broadcasted_iota(jnp.int32, logits.shape, 0)
    ninf = -jnp.inf
    big = ROUTER_ROWS
    gl = jnp.where(row < N_GROUPS, logits, ninf)
    gmax = jnp.max(gl, axis=0, keepdims=True)
    gsum = jnp.sum(jnp.exp(gl - gmax), axis=0, keepdims=True)
    gidx = jnp.min(jnp.where(gl == gmax, row, big), axis=0, keepdims=True)
    g_weight = 1.0 / gsum
    lo = ROUTER_EXPERT_ROW + EXPERTS_PER_GROUP * gidx
    in_group = (row >= lo) & (row < lo + EXPERTS_PER_GROUP)
    el = jnp.where(in_group, logits, ninf)
    emax = jnp.max(el, axis=0, keepdims=True)
    i1 = jnp.min(jnp.where(el == emax, row, big), axis=0, keepdims=True)
    el2 = jnp.where(row == i1, ninf, el)
    emax2 = jnp.max(el2, axis=0, keepdims=True)
    i2 = jnp.min(jnp.where(el2 == emax2, row, big), axis=0, keepdims=True)
    esum = jnp.sum(jnp.exp(el - emax), axis=0, keepdims=True)
    p1 = 1.0 / esum
    p2 = jnp.exp(emax2 - emax) / esum
    c1 = g_weight * (p1 / (p1 + p2))
    c2 = g_weight * (p2 / (p1 + p2))
    e1 = (i1 - ROUTER_EXPERT_ROW).astype(F32)
    e2 = (i2 - ROUTER_EXPERT_ROW).astype(F32)
    orow = lax.broadcasted_iota(jnp.int32, route_ref.shape, 0)
    route_ref[...] = jnp.where(orow == 0, c1, jnp.where(orow == 1, c2, jnp.where(orow == 2, e1,
                               jnp.where(orow == 3, e2, 0.0))))


def _outproj(x2, mix, w_out, norm_w, w_router_t, b_router_t):
    t, d = x2.shape
    tm = min(OUTPROJ_TM, t)
    row = lambda i: (i, 0)
    const = lambda i: (0, 0)
    return pl.pallas_call(
        _outproj_kernel,
        grid=(t // tm,),
        in_specs=[pl.BlockSpec((tm, d), row),
                  pl.BlockSpec((tm, d), row),
                  pl.BlockSpec((d, d), const),
                  pl.BlockSpec((1, d), const),
                  pl.BlockSpec((d, 2 * V7X_LANES), const),
                  pl.BlockSpec((ROUTER_ROWS, 1), const)],
        out_specs=[pl.BlockSpec((tm, d), row),
                   pl.BlockSpec((tm, d), row),
                   pl.BlockSpec((ROUTE_OUT_ROWS, tm), lambda i: (0, i))],
        out_shape=[jax.ShapeDtypeStruct((t, d), F32),
                   jax.ShapeDtypeStruct((t, d), F32),
                   jax.ShapeDtypeStruct((ROUTE_OUT_ROWS, t), F32)],
        compiler_params=_params("parallel"),
        name="outproj",
    )(x2, mix, w_out, norm_w.reshape(1, d), w_router_t, b_router_t)


def _router_params(w_rg, b_rg, w_re, b_re):
    d = w_rg.shape[0]
    gap = ROUTER_EXPERT_ROW - N_GROUPS
    w = jnp.concatenate([w_rg, jnp.zeros((d, gap), F32),
                         jnp.transpose(w_re, (1, 0, 2)).reshape(d, N_EXPERTS),
                         jnp.zeros((d, V7X_LANES - ROUTER_ROWS), F32)], axis=1)
    w_hi = w.astype(BF16)
    w_lo = (w - w_hi.astype(F32)).astype(BF16)
    b_t = jnp.concatenate([b_rg, jnp.zeros((gap,), F32), b_re.reshape(N_EXPERTS)])
    return jnp.concatenate([w_hi, w_lo], axis=1), b_t.reshape(ROUTER_ROWS, 1)


def _dispatch_kernel(pos_ref, lt_ref, nt_ref, h_ref, x_hbm, hbuf, zbuf, zsem, sem, *, tm, max_tiles):
    i = pl.program_id(0)
    tt = h_ref.shape[0]
    slot = lax.rem(i, 2)

    def drain(s):
        for k in range(2):
            pltpu.make_async_copy(hbuf.at[s], x_hbm.at[pl.ds(0, tt), :], sem.at[s]).wait()

    @pl.when(i == 0)
    def _():
        zbuf[...] = jnp.zeros_like(zbuf)

        def zero_tile(tile):
            return pltpu.make_async_copy(zbuf, x_hbm.at[pl.ds(tile * tm, tm), :], zsem)

        for e in range(N_EXPERTS):
            @pl.when(lt_ref[e] >= 0)
            def _():
                zero_tile(lt_ref[e]).start()

        def start_unused(tile, c):
            zero_tile(tile).start()
            return c

        def wait_unused(tile, c):
            zero_tile(tile).wait()
            return c

        lax.fori_loop(nt_ref[0], max_tiles, start_unused, 0)
        for e in range(N_EXPERTS):
            @pl.when(lt_ref[e] >= 0)
            def _():
                zero_tile(lt_ref[e]).wait()
        lax.fori_loop(nt_ref[0], max_tiles, wait_unused, 0)

    @pl.when(i >= 2)
    def _():
        drain(slot)

    hbuf[slot] = h_ref[...]
    base = i * (2 * tt)

    def start(r, c):
        for k in range(2):
            p = pos_ref[base + 2 * r + k]
            pltpu.make_async_copy(hbuf.at[slot, pl.ds(r, 1), :], x_hbm.at[pl.ds(p, 1), :],
                                  sem.at[slot]).start()
        return c

    lax.fori_loop(0, tt, start, 0, unroll=True)

    @pl.when(i == pl.num_programs(0) - 1)
    def _():
        @pl.when(i >= 1)
        def _():
            drain(1 - slot)
        drain(slot)


def _dispatch(h, pos, last_tile, n_tiles, max_tiles):
    t, d = h.shape
    tm = MOE_TM
    tt = min(DISPATCH_TT, t)
    grid_spec = pltpu.PrefetchScalarGridSpec(
        num_scalar_prefetch=3,
        grid=(t // tt,),
        in_specs=[pl.BlockSpec((tt, d), lambda i, pos, lt, nt: (i, 0))],
        out_specs=pl.BlockSpec(memory_space=pl.ANY),
        scratch_shapes=[pltpu.VMEM((2, tt, d), F32), pltpu.VMEM((tm, d), F32),
                        pltpu.SemaphoreType.DMA, pltpu.SemaphoreType.DMA((2,))],
    )
    return pl.pallas_call(
        functools.partial(_dispatch_kernel, tm=tm, max_tiles=max_tiles),
        grid_spec=grid_spec,
        out_shape=jax.ShapeDtypeStruct((max_tiles * tm, d), F32),
        compiler_params=_params("arbitrary"),
        name="dispatch",
    )(pos, last_tile, n_tiles, h)


def _experts_kernel(te_ref, nx_ref, nt_ref, x_ref, wg_hbm, wu_hbm, wd_hbm, y_ref,
                    wgf, wuf, wdf, wgb, wub, wdb, wsem):
    i = pl.program_id(0)
    n_tiles = nt_ref[0]

    def weight_copies(e):
        return (pltpu.make_async_copy(wg_hbm.at[e], wgf, wsem.at[0]),
                pltpu.make_async_copy(wu_hbm.at[e], wuf, wsem.at[1]),
                pltpu.make_async_copy(wd_hbm.at[e], wdf, wsem.at[2]))

    @pl.when(i == 0)
    def _():
        for c in weight_copies(te_ref[0]):
            c.start()

    @pl.when(i < n_tiles)
    def _():
        @pl.when((i == 0) | (te_ref[i] != te_ref[jnp.maximum(i - 1, 0)]))
        def _():
            for c in weight_copies(te_ref[i]):
                c.wait()
            wgb[...] = wgf[...].astype(BF16)
            wub[...] = wuf[...].astype(BF16)
            wdb[...] = wdf[...].astype(BF16)

            @pl.when(nx_ref[i] >= 0)
            def _():
                for c in weight_copies(nx_ref[i]):
                    c.start()

        x = x_ref[...].astype(BF16)
        gate = jnp.dot(x, wgb[...], preferred_element_type=F32)
        up = jnp.dot(x, wub[...], preferred_element_type=F32)
        act = (gate * jax.nn.sigmoid(gate)) * up
        y_ref[...] = jnp.dot(act.astype(BF16), wdb[...], preferred_element_type=F32)

    @pl.when(i >= n_tiles)
    def _():
        y_ref[...] = jnp.zeros_like(y_ref)


def _experts(x_sorted, tile_expert, next_expert, n_tiles, w_gate, w_up, w_down):
    d = x_sorted.shape[1]
    f = w_gate.shape[2]
    tm = MOE_TM
    max_tiles = tile_expert.shape[0]
    any_spec = pl.BlockSpec(memory_space=pl.ANY)
    grid_spec = pltpu.PrefetchScalarGridSpec(
        num_scalar_prefetch=3,
        grid=(max_tiles,),
        in_specs=[pl.BlockSpec((tm, d), lambda i, te, nx, nt: (jnp.minimum(i, nt[0] - 1), 0)),
                  any_spec, any_spec, any_spec],
        out_specs=pl.BlockSpec((tm, d), lambda i, te, nx, nt: (i, 0)),
        scratch_shapes=[pltpu.VMEM((d, f), F32), pltpu.VMEM((d, f), F32), pltpu.VMEM((f, d), F32),
                        pltpu.VMEM((d, f), BF16), pltpu.VMEM((d, f), BF16), pltpu.VMEM((f, d), BF16),
                        pltpu.SemaphoreType.DMA((3,))],
    )
    return pl.pallas_call(
        _experts_kernel,
        grid_spec=grid_spec,
        out_shape=jax.ShapeDtypeStruct((max_tiles * tm, d), F32),
        compiler_params=_params("arbitrary"),
        name="experts",
    )(tile_expert, next_expert, n_tiles, x_sorted, w_gate, w_up, w_down)


def _combine_kernel(pos_ref, x_ref, cw_ref, y_hbm, nw_ref, o_ref, ybuf, sem, *, final_norm):
    i = pl.program_id(0)
    tm = x_ref.shape[0]
    slot = lax.rem(i, 2)

    def start_gather(tile, dst_slot, unroll):
        base = tile * (2 * tm)

        def start(r, c):
            for k in range(2):
                p = pos_ref[base + 2 * r + k]
                pltpu.make_async_copy(y_hbm.at[pl.ds(p, 1), :], ybuf.at[dst_slot, k, pl.ds(r, 1), :],
                                      sem.at[dst_slot]).start()
            return c

        lax.fori_loop(0, tm, start, 0, unroll=unroll)

    @pl.when(i == 0)
    def _():
        start_gather(0, 0, unroll=8)

    @pl.when(i + 1 < pl.num_programs(0))
    def _():
        start_gather(i + 1, 1 - slot, unroll=True)

    for k in range(2):
        pltpu.make_async_copy(y_hbm.at[pl.ds(0, tm), :], ybuf.at[slot, k], sem.at[slot]).wait()
    cw = cw_ref[...]
    x = x_ref[...] + (cw[:, 0:1] * ybuf[slot, 0] + cw[:, 1:2] * ybuf[slot, 1])
    if final_norm:
        ms = jnp.mean(x * x, axis=-1, keepdims=True)
        x = (x * lax.rsqrt(ms + RMS_EPS)) * nw_ref[...]
    o_ref[...] = x


def _combine(x1, cw, y_sorted, pos, norm_w, final_norm):
    t, d = x1.shape
    tm = min(COMBINE_TM, t)
    grid_spec = pltpu.PrefetchScalarGridSpec(
        num_scalar_prefetch=1,
        grid=(t // tm,),
        in_specs=[pl.BlockSpec((tm, d), lambda i, pos: (i, 0)),
                  pl.BlockSpec((tm, 2), lambda i, pos: (i, 0)),
                  pl.BlockSpec(memory_space=pl.ANY),
                  pl.BlockSpec((1, d), lambda i, pos: (0, 0))],
        out_specs=pl.BlockSpec((tm, d), lambda i, pos: (i, 0)),
        scratch_shapes=[pltpu.VMEM((2, 2, tm, d), F32), pltpu.SemaphoreType.DMA((2,))],
    )
    return pl.pallas_call(
        functools.partial(_combine_kernel, final_norm=final_norm),
        grid_spec=grid_spec,
        out_shape=jax.ShapeDtypeStruct((t, d), F32),
        compiler_params=_params("arbitrary"),
        name="combine",
    )(pos, x1, cw, y_sorted, norm_w.reshape(1, d))


def _rope_tables(s):
    half = HEAD_DIM_RET_QK // 2
    inv = ROPE_BASE ** (-jnp.arange(half, dtype=F32) / half)
    ang = jnp.arange(s).astype(F32)[:, None] * inv[None, :]
    cos, sin = jnp.cos(ang), jnp.sin(ang)
    return jnp.concatenate([cos, cos], axis=-1), jnp.concatenate([-sin, sin], axis=-1)


def _dispatch_plan(expert_ids, tm):
    t = expert_ids.shape[0]
    n_assign = 2 * t
    max_tiles = n_assign // tm + N_EXPERTS
    expert = expert_ids.astype(jnp.int32).reshape(n_assign)
    onehot = (expert[:, None] == jnp.arange(N_EXPERTS, dtype=jnp.int32)[None, :]).astype(jnp.int32)
    running = jnp.cumsum(onehot, axis=0)
    counts = running[-1]
    rank = jnp.sum((running - onehot) * onehot, axis=1)
    tiles_per = (counts + tm - 1) // tm
    tile_end = jnp.cumsum(tiles_per)
    row_start = (tile_end - tiles_per) * tm
    pos = row_start[expert] + rank
    n_tiles = tile_end[-1]
    tile_id = jnp.arange(max_tiles, dtype=jnp.int32)
    last = jnp.minimum(tile_id, n_tiles - 1)
    tile_expert = jnp.sum((tile_end[None, :] <= last[:, None]).astype(jnp.int32), axis=1)
    last_tile = jnp.where(tiles_per > 0, tile_end - 1, -1)
    after = tile_end[tile_expert]
    next_expert = jnp.where(after < n_tiles, tile_expert[jnp.minimum(after, max_tiles - 1)], -1)
    return (tile_expert, next_expert.astype(jnp.int32), n_tiles.reshape(1).astype(jnp.int32),
            last_tile.astype(jnp.int32), pos.astype(jnp.int32))


def kernel(x, norm_mix_w, w_in, ret_gn_w, w_branch_moba, w_branch_ret, w_out, norm_ffn_w, w_router_group, b_router_group, w_router_expert, b_router_expert, w_expert_gate, w_expert_up, w_expert_down, norm_final_w):
    b, s, d = x.shape
    t = b * s
    depth = w_in.shape[0]
    assert s % MOBA_BLOCK == 0 and s % RET_CHUNK == 0 and t % MOE_TM == 0
    cos_t, sin_t = _rope_tables(s)
    log_gamma = jnp.log(1.0 - 2.0 ** (-5.0 - jnp.arange(N_HEADS_RET, dtype=F32)))
    x2 = x.reshape(t, d)
    for l in range(depth):
        proj = _inproj(x2, norm_mix_w[l], w_in[l])
        proj3 = proj.reshape(b, s, proj.shape[1])
        o_a = _moba(proj3).reshape(t, -1)
        y_r = _retention(proj3, cos_t, sin_t, ret_gn_w[l], log_gamma).reshape(t, -1)
        mix = _merge(o_a, y_r, proj, w_branch_moba[l].astype(BF16), w_branch_ret[l].astype(BF16))
        w_router_t, b_router_t = _router_params(w_router_group[l], b_router_group[l],
                                                w_router_expert[l], b_router_expert[l])
        x1, h, route_t = _outproj(x2, mix, w_out[l].astype(BF16), norm_ffn_w[l], w_router_t, b_router_t)
        tile_expert, next_expert, n_tiles, last_tile, pos = _dispatch_plan(route_t[2:4].T, MOE_TM)
        x_sorted = _dispatch(h, pos, last_tile, n_tiles, tile_expert.shape[0])
        y_sorted = _experts(x_sorted, tile_expert, next_expert, n_tiles,
                            w_expert_gate[l], w_expert_up[l], w_expert_down[l])
        x2 = _combine(x1, route_t[0:2].T, y_sorted, pos, norm_final_w, final_norm=(l == depth - 1))
    return x2.reshape(b, s, d)
```

```python
import functools

import jax
import jax.numpy as jnp
from jax import lax
from jax.experimental import pallas as pl
from jax.experimental.pallas import tpu as pltpu

F32 = jnp.float32
BF16 = jnp.bfloat16

V7X_LANES = 128
V7X_VMEM_LIMIT_BYTES = 56 * 1024 * 1024

N_HEADS_MOBA = 8
HEAD_DIM_MOBA = 128
MOBA_BLOCK = 256
MOBA_TOPK = 3
N_HEADS_RET = 8
HEAD_DIM_RET_QK = 128
HEAD_DIM_RET_V = 256
ROPE_BASE = 10000.0
N_GROUPS = 4
EXPERTS_PER_GROUP = 8
N_EXPERTS = N_GROUPS * EXPERTS_PER_GROUP
RMS_EPS = 1e-6
GN_EPS = 1e-6
NEG_BIG = -1e30

ROUTER_EXPERT_ROW = 8
ROUTER_ROWS = ROUTER_EXPERT_ROW + N_EXPERTS
ROUTE_OUT_ROWS = 8
MOBA_HEADS_PER_STEP = 4
MOBA_BIAS_ROWS = 16
MOBA_SUM_ROWS = 16
RET_CHUNK = 256
RET_HEADS_PER_STEP = 2
INPROJ_TM, INPROJ_TN = 1024, 1024
INPROJ_MSUB = 2
MERGE_TM, MERGE_TN = 512, 1024
OUTPROJ_TM = 512
OUTPROJ_SUB = 4
MOE_TM = 256
DISPATCH_TT = 256
COMBINE_TM = 256


def _params(*semantics):
    return pltpu.CompilerParams(dimension_semantics=semantics,
                                vmem_limit_bytes=V7X_VMEM_LIMIT_BYTES)


def _inproj_kernel(x_ref, nw_ref, w_ref, o_ref, h_ref, wb_ref):
    j, m = pl.program_id(1), pl.program_id(2)

    @pl.when(j == 0)
    def _():
        x = x_ref[...]
        ms = jnp.mean(x * x, axis=-1, keepdims=True)
        h_ref[m] = ((x * lax.rsqrt(ms + RMS_EPS)) * nw_ref[...]).astype(BF16)

    @pl.when(m == 0)
    def _():
        wb_ref[...] = w_ref[...].astype(BF16)

    o_ref[...] = jnp.dot(h_ref[m], wb_ref[...], preferred_element_type=F32).astype(o_ref.dtype)


def _inproj(x2, norm_w, w_in):
    t, d = x2.shape
    n = w_in.shape[1]
    tm, tn = min(INPROJ_TM, t), INPROJ_TN
    msub = min(INPROJ_MSUB, t // tm)

    def x_map(p, j, m):
        return (jnp.where(j == 0, p * msub + m, p * msub + msub - 1), 0)

    return pl.pallas_call(
        _inproj_kernel,
        grid=(t // (tm * msub), n // tn, msub),
        in_specs=[pl.BlockSpec((tm, d), x_map),
                  pl.BlockSpec((1, d), lambda p, j, m: (0, 0)),
                  pl.BlockSpec((d, tn), lambda p, j, m: (0, j))],
        out_specs=pl.BlockSpec((tm, tn), lambda p, j, m: (p * msub + m, j)),
        out_shape=jax.ShapeDtypeStruct((t, n), BF16),
        scratch_shapes=[pltpu.VMEM((msub, tm, d), BF16), pltpu.VMEM((d, tn), BF16)],
        compiler_params=_params("parallel", "arbitrary", "arbitrary"),
        name="inproj",
    )(x2, norm_w.reshape(1, d), w_in)


def _moba_kernel(q_ref, k_ref, v_ref, o_ref, qta_ref, ka_ref, vta_ref, km_ref, acc_ref, sc_ref, *, nb, hb):
    blk, hd = MOBA_BLOCK, HEAD_DIM_MOBA
    i = pl.program_id(2)

    @pl.when(i == 0)
    def _():
        lane = lax.broadcasted_iota(jnp.int32, (blk, hd), 1)
        srow = lax.broadcasted_iota(jnp.int32, (MOBA_SUM_ROWS, blk), 0)
        for hh in range(hb):
            cols = slice(hh * hd, (hh + 1) * hd)
            means = []
            for c in range(nb):
                rows = slice(c * blk, (c + 1) * blk)
                qta_ref[hh, 0:hd, rows] = q_ref[rows, cols].astype(F32).T.astype(BF16)
                qta_ref[hh, hd:2 * hd, rows] = jnp.zeros((hd, blk), BF16)
                ka_ref[hh, rows, 0:hd] = k_ref[rows, cols]
                ka_ref[hh, rows, hd:2 * hd] = (lane == c).astype(BF16)
                vta_ref[hh, 0:hd, rows] = v_ref[rows, cols].astype(F32).T.astype(BF16)
                vta_ref[hh, hd:hd + MOBA_SUM_ROWS, rows] = (srow == 0).astype(BF16)
                means.append(jnp.sum(k_ref[rows, cols].astype(F32), axis=0, keepdims=True) * (1.0 / blk))
            means += [jnp.zeros((1, hd), F32)] * (MOBA_BIAS_ROWS - nb)
            rest = jnp.concatenate(means, axis=0)
            for part in range(3):
                term = rest.astype(BF16)
                km_ref[hh, part * MOBA_BIAS_ROWS:(part + 1) * MOBA_BIAS_ROWS, :] = term
                rest = rest - term.astype(F32)

    q0 = pl.multiple_of(i * blk, blk)
    qcols = pl.ds(q0, blk)
    c_exp = (hd ** -0.5) * 1.4426950408889634
    row = lax.broadcasted_iota(jnp.int32, (MOBA_BIAS_ROWS, blk), 0)
    kpos = lax.broadcasted_iota(jnp.int32, (blk, blk), 0)
    qpos = lax.broadcasted_iota(jnp.int32, (blk, blk), 1)

    qts = [qta_ref[hh, 0:hd, qcols] for hh in range(hb)]
    gates = [jnp.dot(km_ref[hh], qts[hh], preferred_element_type=F32) for hh in range(hb)]
    own = [jnp.dot(ka_ref[hh, qcols, 0:hd], qts[hh], preferred_element_type=F32)
           for hh in range(hb)]
    for hh in range(hb):
        g3 = gates[hh]
        gate = (g3[0:MOBA_BIAS_ROWS] + g3[MOBA_BIAS_ROWS:2 * MOBA_BIAS_ROWS]) + g3[2 * MOBA_BIAS_ROWS:]
        gate = jnp.where(row < i, gate, -jnp.inf)
        bias = jnp.full((MOBA_BIAS_ROWS, blk), NEG_BIG, F32)
        for _ in range(MOBA_TOPK):
            top = jnp.max(gate, axis=0, keepdims=True)
            is_top = (gate == top) & (top > -jnp.inf)
            first = jnp.min(jnp.where(is_top, row, MOBA_BIAS_ROWS), axis=0, keepdims=True)
            pick = row == first
            bias = jnp.where(pick, 0.0, bias)
            gate = jnp.where(pick, -jnp.inf, gate)
        qta_ref[hh, hd:hd + MOBA_BIAS_ROWS, qcols] = bias.astype(BF16)

    n_trips = (i + 1) // 2

    def key_rows(t):
        return pl.ds(pl.multiple_of(t * (2 * blk), 2 * blk), 2 * blk)

    def score_dots(t):
        return [jnp.dot(ka_ref[hh, key_rows(t), :], qta_ref[hh, :, qcols], preferred_element_type=F32)
                for hh in range(hb)]

    def scores_to(t, slot):
        for hh, s in enumerate(score_dots(t)):
            sc_ref[slot, hh] = s

    scores_to(0, 0)

    init, probs = [], []
    for hh in range(hb):
        s = jnp.where(kpos <= qpos, own[hh] * c_exp, NEG_BIG)
        m0 = jnp.max(s, axis=0, keepdims=True)
        init.append(m0)
        probs.append(jnp.exp2(s - m0).astype(BF16))
    for hh in range(hb):
        acc_ref[hh] = jnp.dot(vta_ref[hh, :, qcols], probs[hh], preferred_element_type=F32)

    def fold(t, slot, ms):
        probs, out = [], []
        for hh in range(hb):
            s = sc_ref[slot, hh] * c_exp
            m_new = jnp.maximum(ms[hh], jnp.max(s, axis=0, keepdims=True))
            probs.append((jnp.exp2(ms[hh] - m_new), jnp.exp2(s - m_new).astype(BF16)))
            out.append(m_new)
        for hh in range(hb):
            alpha, p = probs[hh]
            acc_ref[hh] = alpha * acc_ref[hh] + jnp.dot(vta_ref[hh, :, key_rows(t)], p,
                                                        preferred_element_type=F32)
        return tuple(out)

    def two_trips(u, ms):
        t = 2 * u
        scores_to(t + 1, 1)
        ms = fold(t, 0, ms)
        scores_to(t + 2, 0)
        return fold(t + 1, 1, ms)

    n_loop = jnp.maximum(n_trips - 1, 0) // 2
    ms = lax.fori_loop(0, n_loop, two_trips, tuple(init))
    t_tail = 2 * n_loop

    @pl.when(n_trips - t_tail == 2)
    def _():
        scores_to(t_tail + 1, 1)
        fold(t_tail + 1, 1, fold(t_tail, 0, ms))

    @pl.when(n_trips - t_tail == 1)
    def _():
        fold(t_tail, 0, ms)

    for hh in range(hb):
        acc = acc_ref[hh]
        o_ref[:, hh * hd:(hh + 1) * hd] = (acc[0:hd, :] / acc[hd:hd + 1, :]).T.astype(o_ref.dtype)


def _moba(proj3):
    b, s, _ = proj3.shape
    nb = s // MOBA_BLOCK
    assert nb % 2 == 0 and nb <= MOBA_BIAS_ROWS
    hd, nh, hb = HEAD_DIM_MOBA, N_HEADS_MOBA, MOBA_HEADS_PER_STEP
    ng = nh // hb
    return pl.pallas_call(
        functools.partial(_moba_kernel, nb=nb, hb=hb),
        grid=(b, ng, nb),
        in_specs=[pl.BlockSpec((None, s, hb * hd), lambda bi, g, i: (bi, 0, g)),
                  pl.BlockSpec((None, s, hb * hd), lambda bi, g, i: (bi, 0, ng + g)),
                  pl.BlockSpec((None, s, hb * hd), lambda bi, g, i: (bi, 0, 2 * ng + g))],
        out_specs=pl.BlockSpec((None, MOBA_BLOCK, hb * hd), lambda bi, g, i: (bi, i, g)),
        out_shape=jax.ShapeDtypeStruct((b, s, nh * hd), BF16),
        scratch_shapes=[pltpu.VMEM((hb, 2 * hd, s), BF16), pltpu.VMEM((hb, s, 2 * hd), BF16),
                        pltpu.VMEM((hb, hd + MOBA_SUM_ROWS, s), BF16),
                        pltpu.VMEM((hb, 3 * MOBA_BIAS_ROWS, hd), BF16),
                        pltpu.VMEM((hb, hd + MOBA_SUM_ROWS, MOBA_BLOCK), F32),
                        pltpu.VMEM((2, hb, 2 * MOBA_BLOCK, MOBA_BLOCK), F32)],
        compiler_params=_params("parallel", "parallel", "arbitrary"),
        name="moba",
    )(proj3, proj3, proj3)


def _retention_kernel(lg_ref, q_ref, k_ref, v_ref, g_ref, cos_ref, sin_ref, gnw_ref, o_ref,
                      decay_ref, xi_ref, zeta_ref, st_ref, *, n_chunks, hb):
    c_len = RET_CHUNK
    dk, dv = HEAD_DIM_RET_QK, HEAD_DIM_RET_V
    ri = lax.broadcasted_iota(jnp.int32, (c_len, c_len), 0).astype(F32)
    ci = lax.broadcasted_iota(jnp.int32, (c_len, c_len), 1).astype(F32)
    diff = ri - ci
    idx = lax.broadcasted_iota(jnp.int32, (c_len, dk), 0).astype(F32)
    chunk_decay = []
    for hh in range(hb):
        lg = lg_ref[pl.program_id(1) * hb + hh]
        decay_ref[hh] = jnp.where(diff >= 0, jnp.exp(lg * jnp.maximum(diff, 0.0)), 0.0)
        xi_ref[hh] = jnp.exp(lg * (idx + 1.0))
        zeta_ref[hh] = jnp.exp(lg * (c_len - 1.0 - idx))
        chunk_decay.append(jnp.exp(jnp.zeros((1, dv), F32) + lg * c_len))
        st_ref[hh] = jnp.zeros((dk, dv), F32)
    k_scale = dk ** -0.5
    nt = (((1,), (1,)), ((), ()))

    def chunk(c, carry):
        r0 = pl.multiple_of(c * c_len, c_len)
        rows = pl.ds(r0, c_len)
        cs, sn = cos_ref[rows, :], sin_ref[rows, :]
        qbs, kbs, qxs, kzs = [], [], [], []
        for hh in range(hb):
            q = q_ref[rows, hh * dk:(hh + 1) * dk].astype(F32)
            k = k_ref[rows, hh * dk:(hh + 1) * dk].astype(F32)
            qr = q * cs + pltpu.roll(q, dk // 2, 1) * sn
            kr = (k * cs + pltpu.roll(k, dk // 2, 1) * sn) * k_scale
            qbs.append(qr.astype(BF16))
            kbs.append(kr.astype(BF16))
            qxs.append((qr * xi_ref[hh]).astype(BF16))
            kzs.append((kr * zeta_ref[hh]).T.astype(BF16))
        vs = [v_ref[rows, hh * dv:(hh + 1) * dv] for hh in range(hb)]
        scores = [lax.dot_general(qbs[hh], kbs[hh], nt, preferred_element_type=F32) for hh in range(hb)]
        cross = [jnp.dot(qxs[hh], st_ref[hh].astype(BF16), preferred_element_type=F32) for hh in range(hb)]
        upd = [jnp.dot(kzs[hh], vs[hh], preferred_element_type=F32) for hh in range(hb)]
        sd = [(scores[hh] * decay_ref[hh]).astype(BF16) for hh in range(hb)]
        inner = [jnp.dot(sd[hh], vs[hh], preferred_element_type=F32) for hh in range(hb)]
        for hh in range(hb):
            st_ref[hh] = chunk_decay[hh] * st_ref[hh] + upd[hh]
            o = inner[hh] + cross[hh]
            mu = jnp.mean(o, axis=-1, keepdims=True)
            var = jnp.mean(jnp.square(o - mu), axis=-1, keepdims=True)
            yn = (o - mu) * lax.rsqrt(var + GN_EPS)
            cols = slice(hh * dv, (hh + 1) * dv)
            g = g_ref[rows, cols].astype(F32)
            y = (g * jax.nn.sigmoid(g)) * (yn * gnw_ref[:, cols])
            o_ref[rows, cols] = y.astype(o_ref.dtype)
        return carry

    lax.fori_loop(0, n_chunks, chunk, 0, unroll=2)


def _retention(proj3, cos_t, sin_t, gn_w, log_gamma):
    b, s, _ = proj3.shape
    nh, dk, dv, hb = N_HEADS_RET, HEAD_DIM_RET_QK, HEAD_DIM_RET_V, RET_HEADS_PER_STEP
    ng = nh // hb
    q_blk = 3 * N_HEADS_MOBA * HEAD_DIM_MOBA // (hb * dk)
    k_blk = q_blk + ng
    v_blk = (k_blk + ng) * dk // dv
    g_blk = v_blk + ng
    grid_spec = pltpu.PrefetchScalarGridSpec(
        num_scalar_prefetch=1,
        grid=(b, ng),
        in_specs=[pl.BlockSpec((None, s, hb * dk), lambda bi, g, lg: (bi, 0, q_blk + g)),
                  pl.BlockSpec((None, s, hb * dk), lambda bi, g, lg: (bi, 0, k_blk + g)),
                  pl.BlockSpec((None, s, hb * dv), lambda bi, g, lg: (bi, 0, v_blk + g)),
                  pl.BlockSpec((None, s, hb * dv), lambda bi, g, lg: (bi, 0, g_blk + g)),
                  pl.BlockSpec((s, dk), lambda bi, g, lg: (0, 0)),
                  pl.BlockSpec((s, dk), lambda bi, g, lg: (0, 0)),
                  pl.BlockSpec((1, hb * dv), lambda bi, g, lg: (0, g))],
        out_specs=pl.BlockSpec((None, s, hb * dv), lambda bi, g, lg: (bi, 0, g)),
        scratch_shapes=[pltpu.VMEM((hb, RET_CHUNK, RET_CHUNK), F32), pltpu.VMEM((hb, RET_CHUNK, dk), F32),
                        pltpu.VMEM((hb, RET_CHUNK, dk), F32), pltpu.VMEM((hb, dk, dv), F32)],
    )
    return pl.pallas_call(
        functools.partial(_retention_kernel, n_chunks=s // RET_CHUNK, hb=hb),
        grid_spec=grid_spec,
        out_shape=jax.ShapeDtypeStruct((b, s, nh * dv), BF16),
        compiler_params=_params("parallel", "parallel"),
        name="retention",
    )(log_gamma, proj3, proj3, proj3, proj3, cos_t, sin_t, gn_w.reshape(1, nh * dv))


def _merge_kernel(oa_ref, yr_ref, ga_ref, gr_ref, wa_ref, wr_ref, o_ref, wab_ref, wrb_ref):
    @pl.when(pl.program_id(1) == 0)
    def _():
        wab_ref[...] = wa_ref[...].astype(BF16)
        wrb_ref[...] = wr_ref[...].astype(BF16)

    a = jnp.dot(oa_ref[...], wab_ref[...], preferred_element_type=F32)
    r = jnp.dot(yr_ref[...], wrb_ref[...], preferred_element_type=F32)
    mix = jax.nn.sigmoid(ga_ref[...].astype(F32)) * a + jax.nn.sigmoid(gr_ref[...].astype(F32)) * r
    o_ref[...] = mix.astype(o_ref.dtype)


def _merge(o_a, y_r, proj, w_a, w_r):
    t, wa_in = o_a.shape
    wr_in = y_r.shape[1]
    d = w_a.shape[1]
    tm, tn = min(MERGE_TM, t), MERGE_TN
    ga_blk = (proj.shape[1] - 2 * d) // tn
    gr_blk = ga_blk + d // tn
    return pl.pallas_call(
        _merge_kernel,
        grid=(d // tn, t // tm),
        in_specs=[pl.BlockSpec((tm, wa_in), lambda j, i: (i, 0)),
                  pl.BlockSpec((tm, wr_in), lambda j, i: (i, 0)),
                  pl.BlockSpec((tm, tn), lambda j, i: (i, ga_blk + j)),
                  pl.BlockSpec((tm, tn), lambda j, i: (i, gr_blk + j)),
                  pl.BlockSpec((wa_in, tn), lambda j, i: (0, j)),
                  pl.BlockSpec((wr_in, tn), lambda j, i: (0, j))],
        out_specs=pl.BlockSpec((tm, tn), lambda j, i: (i, j)),
        out_shape=jax.ShapeDtypeStruct((t, d), BF16),
        scratch_shapes=[pltpu.VMEM((wa_in, tn), BF16), pltpu.VMEM((wr_in, tn), BF16)],
        compiler_params=_params("parallel", "arbitrary"),
        name="merge",
    )(o_a, y_r, proj, proj, w_a, w_r)


def _outproj_kernel(x_ref, mix_ref, wo_ref, nw_ref, wr_ref, br_ref, x1_ref, h_ref, route_ref):
    tm = x_ref.shape[0]
    sub = tm // OUTPROJ_SUB
    blocks = [pl.ds(k * sub, sub) for k in range(OUTPROJ_SUB)]
    proj = [jnp.dot(mix_ref[rows, :], wo_ref[...], preferred_element_type=F32) for rows in blocks]
    parts = []
    for rows, pr in zip(blocks, proj):
        x1 = x_ref[rows, :] + pr
        x1_ref[rows, :] = x1
        ms = jnp.mean(x1 * x1, axis=-1, keepdims=True)
        h = (x1 * lax.rsqrt(ms + RMS_EPS)) * nw_ref[...]
        h_ref[rows, :] = h
        h_hi = h.astype(BF16)
        h_lo = (h - h_hi.astype(F32)).astype(BF16)
        r = (jnp.dot(h_hi, wr_ref[...], preferred_element_type=F32)
             + jnp.dot(h_lo, wr_ref[...], preferred_element_type=F32))
        parts.append((r[:, :V7X_LANES] + r[:, V7X_LANES:]).T[:ROUTER_ROWS, :])
    logits = jnp.concatenate(parts, axis=1) + br_ref[...]
    row = lax.broadcasted_iota(jnp.int32, logits.shape, 0)
    ninf = -jnp.inf
    big = ROUTER_ROWS
    gl = jnp.where(row < N_GROUPS, logits, ninf)
    gmax = jnp.max(gl, axis=0, keepdims=True)
    gsum = jnp.sum(jnp.exp(gl - gmax), axis=0, keepdims=True)
    gidx = jnp.min(jnp.where(gl == gmax, row, big), axis=0, keepdims=True)
    g_weight = 1.0 / gsum
    lo = ROUTER_EXPERT_ROW + EXPERTS_PER_GROUP * gidx
    in_group = (row >= lo) & (row < lo + EXPERTS_PER_GROUP)
    el = jnp.where(in_group, logits, ninf)
    emax = jnp.max(el, axis=0, keepdims=True)
    i1 = jnp.min(jnp.where(el == emax, row, big), axis=0, keepdims=True)
    el2 = jnp.where(row == i1, ninf, el)
    emax2 = jnp.max(el2, axis=0, keepdims=True)
    i2 = jnp.min(jnp.where(el2 == emax2, row, big), axis=0, keepdims=True)
    esum = jnp.sum(jnp.exp(el - emax), axis=0, keepdims=True)
    p1 = 1.0 / esum
    p2 = jnp.exp(emax2 - emax) / esum
    c1 = g_weight * (p1 / (p1 + p2))
    c2 = g_weight * (p2 / (p1 + p2))
    e1 = (i1 - ROUTER_EXPERT_ROW).astype(F32)
    e2 = (i2 - ROUTER_EXPERT_ROW).astype(F32)
    orow = lax.broadcasted_iota(jnp.int32, route_ref.shape, 0)
    route_ref[...] = jnp.where(orow == 0, c1, jnp.where(orow == 1, c2, jnp.where(orow == 2, e1,
                               jnp.where(orow == 3, e2, 0.0))))


def _outproj(x2, mix, w_out, norm_w, w_router_t, b_router_t):
    t, d = x2.shape
    tm = min(OUTPROJ_TM, t)
    row = lambda i: (i, 0)
    const = lambda i: (0, 0)
    return pl.pallas_call(
        _outproj_kernel,
        grid=(t // tm,),
        in_specs=[pl.BlockSpec((tm, d), row),
                  pl.BlockSpec((tm, d), row),
                  pl.BlockSpec((d, d), const),
                  pl.BlockSpec((1, d), const),
                  pl.BlockSpec((d, 2 * V7X_LANES), const),
                  pl.BlockSpec((ROUTER_ROWS, 1), const)],
        out_specs=[pl.BlockSpec((tm, d), row),
                   pl.BlockSpec((tm, d), row),
                   pl.BlockSpec((ROUTE_OUT_ROWS, tm), lambda i: (0, i))],
        out_shape=[jax.ShapeDtypeStruct((t, d), F32),
                   jax.ShapeDtypeStruct((t, d), F32),
                   jax.ShapeDtypeStruct((ROUTE_OUT_ROWS, t), F32)],
        compiler_params=_params("parallel"),
        name="outproj",
    )(x2, mix, w_out, norm_w.reshape(1, d), w_router_t, b_router_t)


def _router_params(w_rg, b_rg, w_re, b_re):
    d = w_rg.shape[0]
    gap = ROUTER_EXPERT_ROW - N_GROUPS
    w = jnp.concatenate([w_rg, jnp.zeros((d, gap), F32),
                         jnp.transpose(w_re, (1, 0, 2)).reshape(d, N_EXPERTS),
                         jnp.zeros((d, V7X_LANES - ROUTER_ROWS), F32)], axis=1)
    w_hi = w.astype(BF16)
    w_lo = (w - w_hi.astype(F32)).astype(BF16)
    b_t = jnp.concatenate([b_rg, jnp.zeros((gap,), F32), b_re.reshape(N_EXPERTS)])
    return jnp.concatenate([w_hi, w_lo], axis=1), b_t.reshape(ROUTER_ROWS, 1)


def _meta(meta_ref, row, col=0):
    return meta_ref[row * PLAN_LANES + col]


def _dispatch_kernel(pos_ref, meta_ref, h_ref, x_hbm, hbuf, zbuf, zsem, sem, *, tm, max_tiles):
    i = pl.program_id(0)
    tt = h_ref.shape[0]
    n_tok = tt * pl.num_programs(0)
    slot = lax.rem(i, 2)
    n_tiles = _meta(meta_ref, META_N_TILES)

    def drain(s):
        for k in range(2):
            pltpu.make_async_copy(hbuf.at[s], x_hbm.at[pl.ds(0, tt), :], sem.at[s]).wait()

    @pl.when(i == 0)
    def _():
        zbuf[...] = jnp.zeros_like(zbuf)

        def zero_tile(tile):
            return pltpu.make_async_copy(zbuf, x_hbm.at[pl.ds(tile * tm, tm), :], zsem)

        for e in range(N_EXPERTS):
            @pl.when(_meta(meta_ref, META_LAST_TILE, e) >= 0)
            def _():
                zero_tile(_meta(meta_ref, META_LAST_TILE, e)).start()

        def start_unused(tile, c):
            zero_tile(tile).start()
            return c

        def wait_unused(tile, c):
            zero_tile(tile).wait()
            return c

        lax.fori_loop(n_tiles, max_tiles, start_unused, 0)
        for e in range(N_EXPERTS):
            @pl.when(_meta(meta_ref, META_LAST_TILE, e) >= 0)
            def _():
                zero_tile(_meta(meta_ref, META_LAST_TILE, e)).wait()
        lax.fori_loop(n_tiles, max_tiles, wait_unused, 0)

    @pl.when(i >= 2)
    def _():
        drain(slot)

    hbuf[slot] = h_ref[...]
    base = i * tt

    def start(r, c):
        for k in range(2):
            p = pos_ref[k * n_tok + base + r]
            pltpu.make_async_copy(hbuf.at[slot, pl.ds(r, 1), :], x_hbm.at[pl.ds(p, 1), :],
                                  sem.at[slot]).start()
        return c

    lax.fori_loop(0, tt, start, 0, unroll=True)

    @pl.when(i == pl.num_programs(0) - 1)
    def _():
        @pl.when(i >= 1)
        def _():
            drain(1 - slot)
        drain(slot)


def _dispatch(h, pos, meta, max_tiles):
    t, d = h.shape
    tm = MOE_TM
    tt = min(DISPATCH_TT, t)
    grid_spec = pltpu.PrefetchScalarGridSpec(
        num_scalar_prefetch=2,
        grid=(t // tt,),
        in_specs=[pl.BlockSpec((tt, d), lambda i, pos, meta: (i, 0))],
        out_specs=pl.BlockSpec(memory_space=pl.ANY),
        scratch_shapes=[pltpu.VMEM((2, tt, d), F32), pltpu.VMEM((tm, d), F32),
                        pltpu.SemaphoreType.DMA, pltpu.SemaphoreType.DMA((2,))],
    )
    return pl.pallas_call(
        functools.partial(_dispatch_kernel, tm=tm, max_tiles=max_tiles),
        grid_spec=grid_spec,
        out_shape=jax.ShapeDtypeStruct((max_tiles * tm, d), F32),
        compiler_params=_params("arbitrary"),
        name="dispatch",
    )(pos, meta, h)


def _experts_kernel(meta_ref, x_ref, wg_hbm, wu_hbm, wd_hbm, y_ref, wgf, wuf, wdf, wgb, wub, wdb, wsem):
    i = pl.program_id(0)
    n_tiles = _meta(meta_ref, META_N_TILES)
    n_act = _meta(meta_ref, META_N_ACTIVE)

    def weight_copies(q):
        e = _meta(meta_ref, META_ACTIVE, q)
        return (pltpu.make_async_copy(wg_hbm.at[e], wgf, wsem.at[0]),
                pltpu.make_async_copy(wu_hbm.at[e], wuf, wsem.at[1]),
                pltpu.make_async_copy(wd_hbm.at[e], wdf, wsem.at[2]))

    @pl.when(i == 0)
    def _():
        for c in weight_copies(0):
            c.start()

    @pl.when(i < n_tiles)
    def _():
        q = _meta(meta_ref, META_TILE_SEQ, i)

        @pl.when((i == 0) | (q != _meta(meta_ref, META_TILE_SEQ, jnp.maximum(i - 1, 0))))
        def _():
            for c in weight_copies(q):
                c.wait()
            wgb[...] = wgf[...].astype(BF16)
            wub[...] = wuf[...].astype(BF16)
            wdb[...] = wdf[...].astype(BF16)

            @pl.when(q + 1 < n_act)
            def _():
                for c in weight_copies(q + 1):
                    c.start()

        x = x_ref[...].astype(BF16)
        gate = jnp.dot(x, wgb[...], preferred_element_type=F32)
        up = jnp.dot(x, wub[...], preferred_element_type=F32)
        act = (gate * jax.nn.sigmoid(gate)) * up
        y_ref[...] = jnp.dot(act.astype(BF16), wdb[...], preferred_element_type=F32)

    @pl.when(i >= n_tiles)
    def _():
        y_ref[...] = jnp.zeros_like(y_ref)


def _experts(x_sorted, meta, w_gate, w_up, w_down):
    d = x_sorted.shape[1]
    f = w_gate.shape[2]
    tm = MOE_TM
    max_tiles = x_sorted.shape[0] // tm
    any_spec = pl.BlockSpec(memory_space=pl.ANY)
    grid_spec = pltpu.PrefetchScalarGridSpec(
        num_scalar_prefetch=1,
        grid=(max_tiles,),
        in_specs=[pl.BlockSpec((tm, d), lambda i, meta: (jnp.minimum(i, _meta(meta, META_N_TILES) - 1), 0)),
                  any_spec, any_spec, any_spec],
        out_specs=pl.BlockSpec((tm, d), lambda i, meta: (i, 0)),
        scratch_shapes=[pltpu.VMEM((d, f), F32), pltpu.VMEM((d, f), F32), pltpu.VMEM((f, d), F32),
                        pltpu.VMEM((d, f), BF16), pltpu.VMEM((d, f), BF16), pltpu.VMEM((f, d), BF16),
                        pltpu.SemaphoreType.DMA((3,))],
    )
    return pl.pallas_call(
        _experts_kernel,
        grid_spec=grid_spec,
        out_shape=jax.ShapeDtypeStruct((max_tiles * tm, d), F32),
        compiler_params=_params("arbitrary"),
        name="experts",
    )(meta, x_sorted, w_gate, w_up, w_down)


def _combine_kernel(pos_ref, x_ref, cw_ref, y_hbm, nw_ref, o_ref, ybuf, sem, *, final_norm):
    i = pl.program_id(0)
    tm = x_ref.shape[0]
    n_tok = tm * pl.num_programs(0)
    slot = lax.rem(i, 2)

    def start_gather(tile, dst_slot, unroll):
        base = tile * tm

        def start(r, c):
            for k in range(2):
                p = pos_ref[k * n_tok + base + r]
                pltpu.make_async_copy(y_hbm.at[pl.ds(p, 1), :], ybuf.at[dst_slot, k, pl.ds(r, 1), :],
                                      sem.at[dst_slot]).start()
            return c

        lax.fori_loop(0, tm, start, 0, unroll=unroll)

    @pl.when(i == 0)
    def _():
        start_gather(0, 0, unroll=8)

    @pl.when(i + 1 < pl.num_programs(0))
    def _():
        start_gather(i + 1, 1 - slot, unroll=True)

    for k in range(2):
        pltpu.make_async_copy(y_hbm.at[pl.ds(0, tm), :], ybuf.at[slot, k], sem.at[slot]).wait()
    cw = cw_ref[...]
    x = x_ref[...] + (cw[:, 0:1] * ybuf[slot, 0] + cw[:, 1:2] * ybuf[slot, 1])
    if final_norm:
        ms = jnp.mean(x * x, axis=-1, keepdims=True)
        x = (x * lax.rsqrt(ms + RMS_EPS)) * nw_ref[...]
    o_ref[...] = x


def _combine(x1, cw, y_sorted, pos, norm_w, final_norm):
    t, d = x1.shape
    tm = min(COMBINE_TM, t)
    grid_spec = pltpu.PrefetchScalarGridSpec(
        num_scalar_prefetch=1,
        grid=(t // tm,),
        in_specs=[pl.BlockSpec((tm, d), lambda i, pos: (i, 0)),
                  pl.BlockSpec((tm, 2), lambda i, pos: (i, 0)),
                  pl.BlockSpec(memory_space=pl.ANY),
                  pl.BlockSpec((1, d), lambda i, pos: (0, 0))],
        out_specs=pl.BlockSpec((tm, d), lambda i, pos: (i, 0)),
        scratch_shapes=[pltpu.VMEM((2, 2, tm, d), F32), pltpu.SemaphoreType.DMA((2,))],
    )
    return pl.pallas_call(
        functools.partial(_combine_kernel, final_norm=final_norm),
        grid_spec=grid_spec,
        out_shape=jax.ShapeDtypeStruct((t, d), F32),
        compiler_params=_params("arbitrary"),
        name="combine",
    )(pos, x1, cw, y_sorted, norm_w.reshape(1, d))


def _rope_tables(s):
    half = HEAD_DIM_RET_QK // 2
    inv = ROPE_BASE ** (-jnp.arange(half, dtype=F32) / half)
    ang = jnp.arange(s).astype(F32)[:, None] * inv[None, :]
    cos, sin = jnp.cos(ang), jnp.sin(ang)
    return jnp.concatenate([cos, cos], axis=-1), jnp.concatenate([-sin, sin], axis=-1)


META_TILE_SEQ, META_ACTIVE, META_LAST_TILE, META_N_ACTIVE, META_N_TILES = 0, 1, 2, 3, 4
META_ROWS = 8
PLAN_LANES = V7X_LANES


def _plan_kernel(route_ref, pfx_ref, tri_ref, pos_ref, meta_ref, *, tm):
    ne, lanes = N_EXPERTS, PLAN_LANES
    t = route_ref.shape[1]
    nblk = t // lanes
    ids = lax.broadcasted_iota(jnp.int32, (ne, lanes), 0).astype(F32)

    def onehots(blk):
        cols = slice(blk * lanes, (blk + 1) * lanes)
        return [(ids == route_ref[2 + k:3 + k, cols]).astype(F32) for k in range(2)]

    stacked = jnp.concatenate([sum(onehots(blk)) for blk in range(nblk)], axis=0)
    prod = jnp.dot(stacked.astype(BF16), pfx_ref[...], preferred_element_type=F32)
    carry = jnp.zeros((ne, lanes), F32)
    before = []
    for blk in range(nblk):
        rows = slice(blk * ne, (blk + 1) * ne)
        before.append(prod[rows, :lanes] + carry)
        carry = carry + prod[rows, lanes:]
    counts = carry
    tiles_per = jnp.floor((counts + (tm - 1.0)) * (1.0 / tm))
    tri = tri_ref[...]
    tile_end = jnp.dot(tri, tiles_per.astype(BF16), preferred_element_type=F32)
    row_start = (tile_end - tiles_per) * tm
    for k in range(2):
        pieces = []
        for blk in range(nblk):
            oh = onehots(blk)[k]
            pieces.append(jnp.sum(oh * (before[blk] + row_start), axis=0, keepdims=True))
        pos_ref[k:k + 1, :] = jnp.concatenate(pieces, axis=1).astype(jnp.int32)

    owns = (tiles_per > 0.0).astype(F32)
    seq = jnp.dot(tri, owns.astype(BF16), preferred_element_type=F32) - 1.0
    n_active = seq[ne - 1:ne, :] + 1.0
    n_tiles = tile_end[ne - 1:ne, :]
    lane = lax.broadcasted_iota(jnp.int32, (ne, lanes), 1).astype(F32)
    tile = jnp.minimum(lane, n_tiles - 1.0)
    tile_seq = jnp.sum(owns * (tile_end <= tile).astype(F32), axis=0, keepdims=True)
    active = jnp.sum(jnp.where((owns > 0.0) & (seq == lane), ids, 0.0), axis=0, keepdims=True)
    last_tile = jnp.sum(jnp.where(ids == lane, jnp.where(owns > 0.0, tile_end - 1.0, -1.0), 0.0),
                        axis=0, keepdims=True)
    zero = jnp.zeros((1, lanes), F32)
    rows = {META_TILE_SEQ: tile_seq, META_ACTIVE: active, META_LAST_TILE: last_tile,
            META_N_ACTIVE: n_active, META_N_TILES: n_tiles}
    meta_ref[...] = jnp.concatenate([rows.get(r, zero) for r in range(META_ROWS)], axis=0).astype(jnp.int32)


def _plan(route_t, tm):
    t = route_t.shape[1]
    max_tiles = 2 * t // tm + N_EXPERTS
    assert max_tiles <= PLAN_LANES and N_EXPERTS <= PLAN_LANES and t % PLAN_LANES == 0
    r = jnp.arange(PLAN_LANES)
    prefix = jnp.concatenate([(r[:, None] < r[None, :]).astype(BF16),
                              jnp.ones((PLAN_LANES, PLAN_LANES), BF16)], axis=1)
    e = jnp.arange(N_EXPERTS)
    tri = (e[None, :] <= e[:, None]).astype(BF16)
    pos, meta = pl.pallas_call(
        functools.partial(_plan_kernel, tm=tm),
        out_shape=[jax.ShapeDtypeStruct((2, t), jnp.int32),
                   jax.ShapeDtypeStruct((META_ROWS, PLAN_LANES), jnp.int32)],
        compiler_params=pltpu.CompilerParams(vmem_limit_bytes=V7X_VMEM_LIMIT_BYTES),
        name="plan",
    )(route_t, prefix, tri)
    return pos.reshape(2 * t), meta.reshape(META_ROWS * PLAN_LANES), max_tiles


def kernel(x, norm_mix_w, w_in, ret_gn_w, w_branch_moba, w_branch_ret, w_out, norm_ffn_w, w_router_group, b_router_group, w_router_expert, b_router_expert, w_expert_gate, w_expert_up, w_expert_down, norm_final_w):
    b, s, d = x.shape
    t = b * s
    depth = w_in.shape[0]
    assert s % MOBA_BLOCK == 0 and s % RET_CHUNK == 0 and t % MOE_TM == 0
    cos_t, sin_t = _rope_tables(s)
    log_gamma = jnp.log(1.0 - 2.0 ** (-5.0 - jnp.arange(N_HEADS_RET, dtype=F32)))
    x2 = x.reshape(t, d)
    for l in range(depth):
        proj = _inproj(x2, norm_mix_w[l], w_in[l])
        proj3 = proj.reshape(b, s, proj.shape[1])
        o_a = _moba(proj3).reshape(t, -1)
        y_r = _retention(proj3, cos_t, sin_t, ret_gn_w[l], log_gamma).reshape(t, -1)
        mix = _merge(o_a, y_r, proj, w_branch_moba[l], w_branch_ret[l])
        w_router_t, b_router_t = _router_params(w_router_group[l], b_router_group[l],
                                                w_router_expert[l], b_router_expert[l])
        x1, h, route_t = _outproj(x2, mix, w_out[l].astype(BF16), norm_ffn_w[l], w_router_t, b_router_t)
        pos, meta, max_tiles = _plan(route_t, MOE_TM)
        x_sorted = _dispatch(h, pos, meta, max_tiles)
        y_sorted = _experts(x_sorted, meta, w_expert_gate[l], w_expert_up[l], w_expert_down[l])
        x2 = _combine(x1, route_t[0:2].T, y_sorted, pos, norm_final_w, final_norm=(l == depth - 1))
    return x2.reshape(b, s, d)
```

```python
import functools

import jax
import jax.numpy as jnp
from jax import lax
from jax.experimental import pallas as pl
from jax.experimental.pallas import tpu as pltpu

F32 = jnp.float32
BF16 = jnp.bfloat16

V7X_LANES = 128
V7X_VMEM_LIMIT_BYTES = 56 * 1024 * 1024

N_HEADS_MOBA = 8
HEAD_DIM_MOBA = 128
MOBA_BLOCK = 256
MOBA_TOPK = 3
N_HEADS_RET = 8
HEAD_DIM_RET_QK = 128
HEAD_DIM_RET_V = 256
ROPE_BASE = 10000.0
N_GROUPS = 4
EXPERTS_PER_GROUP = 8
N_EXPERTS = N_GROUPS * EXPERTS_PER_GROUP
RMS_EPS = 1e-6
GN_EPS = 1e-6
NEG_BIG = -1e30

ROUTER_EXPERT_ROW = 8
ROUTER_ROWS = ROUTER_EXPERT_ROW + N_EXPERTS
ROUTE_OUT_ROWS = 8
MOBA_HEADS_PER_STEP = 4
MOBA_BIAS_ROWS = 16
MOBA_SUM_ROWS = 16
RET_CHUNK = 256
RET_HEADS_PER_STEP = 2
INPROJ_TM, INPROJ_TN = 1024, 1024
INPROJ_MSUB = 2
MERGE_TM, MERGE_TN = 512, 1024
OUTPROJ_TM = 512
OUTPROJ_SUB = 4
MOE_TM = 256
DISPATCH_TT = 256
COMBINE_TM = 256


def _params(*semantics):
    return pltpu.CompilerParams(dimension_semantics=semantics,
                                vmem_limit_bytes=V7X_VMEM_LIMIT_BYTES)


def _inproj_kernel(x_ref, nw_ref, w_ref, o_ref, z_ref, h_ref, wb_ref):
    j, m = pl.program_id(1), pl.program_id(2)
    z_ref[...] = jnp.zeros_like(z_ref)

    @pl.when(j == 0)
    def _():
        x = x_ref[...]
        ms = jnp.mean(x * x, axis=-1, keepdims=True)
        h_ref[m] = ((x * lax.rsqrt(ms + RMS_EPS)) * nw_ref[...]).astype(BF16)

    @pl.when(m == 0)
    def _():
        wb_ref[...] = w_ref[...].astype(BF16)

    o_ref[...] = jnp.dot(h_ref[m], wb_ref[...], preferred_element_type=F32).astype(o_ref.dtype)


def _inproj(x2, norm_w, w_in, zero_rows):
    t, d = x2.shape
    n = w_in.shape[1]
    tm, tn = min(INPROJ_TM, t), INPROJ_TN
    msub = min(INPROJ_MSUB, t // tm)
    nj = n // tn
    n_steps = (t // (tm * msub)) * nj * msub
    zb = pl.cdiv(pl.cdiv(zero_rows, n_steps), 8) * 8
    n_zblocks = pl.cdiv(zero_rows, zb)

    def x_map(p, j, m):
        return (jnp.where(j == 0, p * msub + m, p * msub + msub - 1), 0)

    def z_map(p, j, m):
        return (jnp.minimum((p * nj + j) * msub + m, n_zblocks - 1), 0)

    return pl.pallas_call(
        _inproj_kernel,
        grid=(t // (tm * msub), nj, msub),
        in_specs=[pl.BlockSpec((tm, d), x_map),
                  pl.BlockSpec((1, d), lambda p, j, m: (0, 0)),
                  pl.BlockSpec((d, tn), lambda p, j, m: (0, j))],
        out_specs=[pl.BlockSpec((tm, tn), lambda p, j, m: (p * msub + m, j)),
                   pl.BlockSpec((zb, d), z_map)],
        out_shape=[jax.ShapeDtypeStruct((t, n), BF16), jax.ShapeDtypeStruct((zero_rows, d), F32)],
        scratch_shapes=[pltpu.VMEM((msub, tm, d), BF16), pltpu.VMEM((d, tn), BF16)],
        compiler_params=_params("arbitrary", "arbitrary", "arbitrary"),
        name="inproj",
    )(x2, norm_w.reshape(1, d), w_in)


def _moba_kernel(q_ref, k_ref, v_ref, o_ref, qta_ref, ka_ref, vta_ref, km_ref, acc_ref, sc_ref, *, nb, hb):
    blk, hd = MOBA_BLOCK, HEAD_DIM_MOBA
    i = pl.program_id(2)

    @pl.when(i == 0)
    def _():
        lane = lax.broadcasted_iota(jnp.int32, (blk, hd), 1)
        srow = lax.broadcasted_iota(jnp.int32, (MOBA_SUM_ROWS, blk), 0)
        for hh in range(hb):
            cols = slice(hh * hd, (hh + 1) * hd)
            means = []
            for c in range(nb):
                rows = slice(c * blk, (c + 1) * blk)
                qta_ref[hh, 0:hd, rows] = q_ref[rows, cols].astype(F32).T.astype(BF16)
                qta_ref[hh, hd:2 * hd, rows] = jnp.zeros((hd, blk), BF16)
                ka_ref[hh, rows, 0:hd] = k_ref[rows, cols]
                ka_ref[hh, rows, hd:2 * hd] = (lane == c).astype(BF16)
                vta_ref[hh, 0:hd, rows] = v_ref[rows, cols].astype(F32).T.astype(BF16)
                vta_ref[hh, hd:hd + MOBA_SUM_ROWS, rows] = (srow == 0).astype(BF16)
                means.append(jnp.sum(k_ref[rows, cols].astype(F32), axis=0, keepdims=True) * (1.0 / blk))
            means += [jnp.zeros((1, hd), F32)] * (MOBA_BIAS_ROWS - nb)
            rest = jnp.concatenate(means, axis=0)
            for part in range(3):
                term = rest.astype(BF16)
                km_ref[hh, part * MOBA_BIAS_ROWS:(part + 1) * MOBA_BIAS_ROWS, :] = term
                rest = rest - term.astype(F32)

    q0 = pl.multiple_of(i * blk, blk)
    qcols = pl.ds(q0, blk)
    c_exp = (hd ** -0.5) * 1.4426950408889634
    row = lax.broadcasted_iota(jnp.int32, (MOBA_BIAS_ROWS, blk), 0)
    kpos = lax.broadcasted_iota(jnp.int32, (blk, blk), 0)
    qpos = lax.broadcasted_iota(jnp.int32, (blk, blk), 1)

    qts = [qta_ref[hh, 0:hd, qcols] for hh in range(hb)]
    gates = [jnp.dot(km_ref[hh], qts[hh], preferred_element_type=F32) for hh in range(hb)]
    own = [jnp.dot(ka_ref[hh, qcols, 0:hd], qts[hh], preferred_element_type=F32)
           for hh in range(hb)]
    for hh in range(hb):
        g3 = gates[hh]
        gate = (g3[0:MOBA_BIAS_ROWS] + g3[MOBA_BIAS_ROWS:2 * MOBA_BIAS_ROWS]) + g3[2 * MOBA_BIAS_ROWS:]
        gate = jnp.where(row < i, gate, -jnp.inf)
        bias = jnp.full((MOBA_BIAS_ROWS, blk), NEG_BIG, F32)
        for _ in range(MOBA_TOPK):
            top = jnp.max(gate, axis=0, keepdims=True)
            is_top = (gate == top) & (top > -jnp.inf)
            first = jnp.min(jnp.where(is_top, row, MOBA_BIAS_ROWS), axis=0, keepdims=True)
            pick = row == first
            bias = jnp.where(pick, 0.0, bias)
            gate = jnp.where(pick, -jnp.inf, gate)
        qta_ref[hh, hd:hd + MOBA_BIAS_ROWS, qcols] = bias.astype(BF16)

    n_trips = (i + 1) // 2

    def key_rows(t):
        return pl.ds(pl.multiple_of(t * (2 * blk), 2 * blk), 2 * blk)

    def score_dots(t):
        return [jnp.dot(ka_ref[hh, key_rows(t), :], qta_ref[hh, :, qcols], preferred_element_type=F32)
                for hh in range(hb)]

    def scores_to(t, slot):
        for hh, s in enumerate(score_dots(t)):
            sc_ref[slot, hh] = s

    scores_to(0, 0)

    init, probs = [], []
    for hh in range(hb):
        s = jnp.where(kpos <= qpos, own[hh] * c_exp, NEG_BIG)
        m0 = jnp.max(s, axis=0, keepdims=True)
        init.append(m0)
        probs.append(jnp.exp2(s - m0).astype(BF16))
    for hh in range(hb):
        acc_ref[hh] = jnp.dot(vta_ref[hh, :, qcols], probs[hh], preferred_element_type=F32)

    def fold(t, slot, ms):
        probs, out = [], []
        for hh in range(hb):
            s = sc_ref[slot, hh] * c_exp
            m_new = jnp.maximum(ms[hh], jnp.max(s, axis=0, keepdims=True))
            probs.append((jnp.exp2(ms[hh] - m_new), jnp.exp2(s - m_new).astype(BF16)))
            out.append(m_new)
        for hh in range(hb):
            alpha, p = probs[hh]
            acc_ref[hh] = alpha * acc_ref[hh] + jnp.dot(vta_ref[hh, :, key_rows(t)], p,
                                                        preferred_element_type=F32)
        return tuple(out)

    def two_trips(u, ms):
        t = 2 * u
        scores_to(t + 1, 1)
        ms = fold(t, 0, ms)
        scores_to(t + 2, 0)
        return fold(t + 1, 1, ms)

    n_loop = jnp.maximum(n_trips - 1, 0) // 2
    ms = lax.fori_loop(0, n_loop, two_trips, tuple(init))
    t_tail = 2 * n_loop

    @pl.when(n_trips - t_tail == 2)
    def _():
        scores_to(t_tail + 1, 1)
        fold(t_tail + 1, 1, fold(t_tail, 0, ms))

    @pl.when(n_trips - t_tail == 1)
    def _():
        fold(t_tail, 0, ms)

    for hh in range(hb):
        acc = acc_ref[hh]
        o_ref[:, hh * hd:(hh + 1) * hd] = (acc[0:hd, :] / acc[hd:hd + 1, :]).T.astype(o_ref.dtype)


def _moba(proj3):
    b, s, _ = proj3.shape
    nb = s // MOBA_BLOCK
    assert nb % 2 == 0 and nb <= MOBA_BIAS_ROWS
    hd, nh, hb = HEAD_DIM_MOBA, N_HEADS_MOBA, MOBA_HEADS_PER_STEP
    ng = nh // hb
    return pl.pallas_call(
        functools.partial(_moba_kernel, nb=nb, hb=hb),
        grid=(b, ng, nb),
        in_specs=[pl.BlockSpec((None, s, hb * hd), lambda bi, g, i: (bi, 0, g)),
                  pl.BlockSpec((None, s, hb * hd), lambda bi, g, i: (bi, 0, ng + g)),
                  pl.BlockSpec((None, s, hb * hd), lambda bi, g, i: (bi, 0, 2 * ng + g))],
        out_specs=pl.BlockSpec((None, MOBA_BLOCK, hb * hd), lambda bi, g, i: (bi, i, g)),
        out_shape=jax.ShapeDtypeStruct((b, s, nh * hd), BF16),
        scratch_shapes=[pltpu.VMEM((hb, 2 * hd, s), BF16), pltpu.VMEM((hb, s, 2 * hd), BF16),
                        pltpu.VMEM((hb, hd + MOBA_SUM_ROWS, s), BF16),
                        pltpu.VMEM((hb, 3 * MOBA_BIAS_ROWS, hd), BF16),
                        pltpu.VMEM((hb, hd + MOBA_SUM_ROWS, MOBA_BLOCK), F32),
                        pltpu.VMEM((2, hb, 2 * MOBA_BLOCK, MOBA_BLOCK), F32)],
        compiler_params=_params("parallel", "parallel", "arbitrary"),
        name="moba",
    )(proj3, proj3, proj3)


def _retention_kernel(lg_ref, q_ref, k_ref, v_ref, g_ref, cos_ref, sin_ref, gnw_ref, o_ref,
                      decay_ref, xi_ref, zeta_ref, st_ref, *, n_chunks, hb):
    c_len = RET_CHUNK
    dk, dv = HEAD_DIM_RET_QK, HEAD_DIM_RET_V
    ri = lax.broadcasted_iota(jnp.int32, (c_len, c_len), 0).astype(F32)
    ci = lax.broadcasted_iota(jnp.int32, (c_len, c_len), 1).astype(F32)
    diff = ri - ci
    idx = lax.broadcasted_iota(jnp.int32, (c_len, dk), 0).astype(F32)
    chunk_decay = []
    for hh in range(hb):
        lg = lg_ref[pl.program_id(1) * hb + hh]
        decay_ref[hh] = jnp.where(diff >= 0, jnp.exp(lg * jnp.maximum(diff, 0.0)), 0.0)
        xi_ref[hh] = jnp.exp(lg * (idx + 1.0))
        zeta_ref[hh] = jnp.exp(lg * (c_len - 1.0 - idx))
        chunk_decay.append(jnp.exp(jnp.zeros((1, dv), F32) + lg * c_len))
        st_ref[hh] = jnp.zeros((dk, dv), F32)
    k_scale = dk ** -0.5
    nt = (((1,), (1,)), ((), ()))

    def chunk(c, carry):
        r0 = pl.multiple_of(c * c_len, c_len)
        rows = pl.ds(r0, c_len)
        cs, sn = cos_ref[rows, :], sin_ref[rows, :]
        qbs, kbs, qxs, kzs = [], [], [], []
        for hh in range(hb):
            q = q_ref[rows, hh * dk:(hh + 1) * dk].astype(F32)
            k = k_ref[rows, hh * dk:(hh + 1) * dk].astype(F32)
            qr = q * cs + pltpu.roll(q, dk // 2, 1) * sn
            kr = (k * cs + pltpu.roll(k, dk // 2, 1) * sn) * k_scale
            qbs.append(qr.astype(BF16))
            kbs.append(kr.astype(BF16))
            qxs.append((qr * xi_ref[hh]).astype(BF16))
            kzs.append((kr * zeta_ref[hh]).T.astype(BF16))
        vs = [v_ref[rows, hh * dv:(hh + 1) * dv] for hh in range(hb)]
        scores = [lax.dot_general(qbs[hh], kbs[hh], nt, preferred_element_type=F32) for hh in range(hb)]
        cross = [jnp.dot(qxs[hh], st_ref[hh].astype(BF16), preferred_element_type=F32) for hh in range(hb)]
        upd = [jnp.dot(kzs[hh], vs[hh], preferred_element_type=F32) for hh in range(hb)]
        sd = [(scores[hh] * decay_ref[hh]).astype(BF16) for hh in range(hb)]
        inner = [jnp.dot(sd[hh], vs[hh], preferred_element_type=F32) for hh in range(hb)]
        for hh in range(hb):
            st_ref[hh] = chunk_decay[hh] * st_ref[hh] + upd[hh]
            o = inner[hh] + cross[hh]
            mu = jnp.mean(o, axis=-1, keepdims=True)
            var = jnp.mean(jnp.square(o - mu), axis=-1, keepdims=True)
            yn = (o - mu) * lax.rsqrt(var + GN_EPS)
            cols = slice(hh * dv, (hh + 1) * dv)
            g = g_ref[rows, cols].astype(F32)
            y = (g * jax.nn.sigmoid(g)) * (yn * gnw_ref[:, cols])
            o_ref[rows, cols] = y.astype(o_ref.dtype)
        return carry

    lax.fori_loop(0, n_chunks, chunk, 0, unroll=2)


def _retention(proj3, cos_t, sin_t, gn_w, log_gamma):
    b, s, _ = proj3.shape
    nh, dk, dv, hb = N_HEADS_RET, HEAD_DIM_RET_QK, HEAD_DIM_RET_V, RET_HEADS_PER_STEP
    ng = nh // hb
    q_blk = 3 * N_HEADS_MOBA * HEAD_DIM_MOBA // (hb * dk)
    k_blk = q_blk + ng
    v_blk = (k_blk + ng) * dk // dv
    g_blk = v_blk + ng
    grid_spec = pltpu.PrefetchScalarGridSpec(
        num_scalar_prefetch=1,
        grid=(b, ng),
        in_specs=[pl.BlockSpec((None, s, hb * dk), lambda bi, g, lg: (bi, 0, q_blk + g)),
                  pl.BlockSpec((None, s, hb * dk), lambda bi, g, lg: (bi, 0, k_blk + g)),
                  pl.BlockSpec((None, s, hb * dv), lambda bi, g, lg: (bi, 0, v_blk + g)),
                  pl.BlockSpec((None, s, hb * dv), lambda bi, g, lg: (bi, 0, g_blk + g)),
                  pl.BlockSpec((s, dk), lambda bi, g, lg: (0, 0)),
                  pl.BlockSpec((s, dk), lambda bi, g, lg: (0, 0)),
                  pl.BlockSpec((1, hb * dv), lambda bi, g, lg: (0, g))],
        out_specs=pl.BlockSpec((None, s, hb * dv), lambda bi, g, lg: (bi, 0, g)),
        scratch_shapes=[pltpu.VMEM((hb, RET_CHUNK, RET_CHUNK), F32), pltpu.VMEM((hb, RET_CHUNK, dk), F32),
                        pltpu.VMEM((hb, RET_CHUNK, dk), F32), pltpu.VMEM((hb, dk, dv), F32)],
    )
    return pl.pallas_call(
        functools.partial(_retention_kernel, n_chunks=s // RET_CHUNK, hb=hb),
        grid_spec=grid_spec,
        out_shape=jax.ShapeDtypeStruct((b, s, nh * dv), BF16),
        compiler_params=_params("parallel", "parallel"),
        name="retention",
    )(log_gamma, proj3, proj3, proj3, proj3, cos_t, sin_t, gn_w.reshape(1, nh * dv))


def _merge_kernel(oa_ref, yr_ref, ga_ref, gr_ref, wa_ref, wr_ref, o_ref, wab_ref, wrb_ref):
    @pl.when(pl.program_id(1) == 0)
    def _():
        wab_ref[...] = wa_ref[...].astype(BF16)
        wrb_ref[...] = wr_ref[...].astype(BF16)

    a = jnp.dot(oa_ref[...], wab_ref[...], preferred_element_type=F32)
    r = jnp.dot(yr_ref[...], wrb_ref[...], preferred_element_type=F32)
    mix = jax.nn.sigmoid(ga_ref[...].astype(F32)) * a + jax.nn.sigmoid(gr_ref[...].astype(F32)) * r
    o_ref[...] = mix.astype(o_ref.dtype)


def _merge(o_a, y_r, proj, w_a, w_r):
    t, wa_in = o_a.shape
    wr_in = y_r.shape[1]
    d = w_a.shape[1]
    tm, tn = min(MERGE_TM, t), MERGE_TN
    ga_blk = (proj.shape[1] - 2 * d) // tn
    gr_blk = ga_blk + d // tn
    return pl.pallas_call(
        _merge_kernel,
        grid=(d // tn, t // tm),
        in_specs=[pl.BlockSpec((tm, wa_in), lambda j, i: (i, 0)),
                  pl.BlockSpec((tm, wr_in), lambda j, i: (i, 0)),
                  pl.BlockSpec((tm, tn), lambda j, i: (i, ga_blk + j)),
                  pl.BlockSpec((tm, tn), lambda j, i: (i, gr_blk + j)),
                  pl.BlockSpec((wa_in, tn), lambda j, i: (0, j)),
                  pl.BlockSpec((wr_in, tn), lambda j, i: (0, j))],
        out_specs=pl.BlockSpec((tm, tn), lambda j, i: (i, j)),
        out_shape=jax.ShapeDtypeStruct((t, d), BF16),
        scratch_shapes=[pltpu.VMEM((wa_in, tn), BF16), pltpu.VMEM((wr_in, tn), BF16)],
        compiler_params=_params("parallel", "arbitrary"),
        name="merge",
    )(o_a, y_r, proj, proj, w_a, w_r)


def _outproj_kernel(x_ref, mix_ref, wo_ref, nw_ref, wr_ref, br_ref, x1_ref, h_ref, route_ref):
    tm = x_ref.shape[0]
    sub = tm // OUTPROJ_SUB
    blocks = [pl.ds(k * sub, sub) for k in range(OUTPROJ_SUB)]
    proj = [jnp.dot(mix_ref[rows, :], wo_ref[...], preferred_element_type=F32) for rows in blocks]
    parts = []
    for rows, pr in zip(blocks, proj):
        x1 = x_ref[rows, :] + pr
        x1_ref[rows, :] = x1
        ms = jnp.mean(x1 * x1, axis=-1, keepdims=True)
        h = (x1 * lax.rsqrt(ms + RMS_EPS)) * nw_ref[...]
        h_ref[rows, :] = h
        h_hi = h.astype(BF16)
        h_lo = (h - h_hi.astype(F32)).astype(BF16)
        r = (jnp.dot(h_hi, wr_ref[...], preferred_element_type=F32)
             + jnp.dot(h_lo, wr_ref[...], preferred_element_type=F32))
        parts.append((r[:, :V7X_LANES] + r[:, V7X_LANES:]).T[:ROUTER_ROWS, :])
    logits = jnp.concatenate(parts, axis=1) + br_ref[...]
    row = lax.broadcasted_iota(jnp.int32, logits.shape, 0)
    ninf = -jnp.inf
    big = ROUTER_ROWS
    gl = jnp.where(row < N_GROUPS, logits, ninf)
    gmax = jnp.max(gl, axis=0, keepdims=True)
    gsum = jnp.sum(jnp.exp(gl - gmax), axis=0, keepdims=True)
    gidx = jnp.min(jnp.where(gl == gmax, row, big), axis=0, keepdims=True)
    g_weight = 1.0 / gsum
    lo = ROUTER_EXPERT_ROW + EXPERTS_PER_GROUP * gidx
    in_group = (row >= lo) & (row < lo + EXPERTS_PER_GROUP)
    el = jnp.where(in_group, logits, ninf)
    emax = jnp.max(el, axis=0, keepdims=True)
    i1 = jnp.min(jnp.where(el == emax, row, big), axis=0, keepdims=True)
    el2 = jnp.where(row == i1, ninf, el)
    emax2 = jnp.max(el2, axis=0, keepdims=True)
    i2 = jnp.min(jnp.where(el2 == emax2, row, big), axis=0, keepdims=True)
    esum = jnp.sum(jnp.exp(el - emax), axis=0, keepdims=True)
    p1 = 1.0 / esum
    p2 = jnp.exp(emax2 - emax) / esum
    c1 = g_weight * (p1 / (p1 + p2))
    c2 = g_weight * (p2 / (p1 + p2))
    e1 = (i1 - ROUTER_EXPERT_ROW).astype(F32)
    e2 = (i2 - ROUTER_EXPERT_ROW).astype(F32)
    orow = lax.broadcasted_iota(jnp.int32, route_ref.shape, 0)
    route_ref[...] = jnp.where(orow == 0, c1, jnp.where(orow == 1, c2, jnp.where(orow == 2, e1,
                               jnp.where(orow == 3, e2, 0.0))))


def _outproj(x2, mix, w_out, norm_w, w_router_t, b_router_t):
    t, d = x2.shape
    tm = min(OUTPROJ_TM, t)
    row = lambda i: (i, 0)
    const = lambda i: (0, 0)
    return pl.pallas_call(
        _outproj_kernel,
        grid=(t // tm,),
        in_specs=[pl.BlockSpec((tm, d), row),
                  pl.BlockSpec((tm, d), row),
                  pl.BlockSpec((d, d), const),
                  pl.BlockSpec((1, d), const),
                  pl.BlockSpec((d, 2 * V7X_LANES), const),
                  pl.BlockSpec((ROUTER_ROWS, 1), const)],
        out_specs=[pl.BlockSpec((tm, d), row),
                   pl.BlockSpec((tm, d), row),
                   pl.BlockSpec((ROUTE_OUT_ROWS, tm), lambda i: (0, i))],
        out_shape=[jax.ShapeDtypeStruct((t, d), F32),
                   jax.ShapeDtypeStruct((t, d), F32),
                   jax.ShapeDtypeStruct((ROUTE_OUT_ROWS, t), F32)],
        compiler_params=_params("parallel"),
        name="outproj",
    )(x2, mix, w_out, norm_w.reshape(1, d), w_router_t, b_router_t)


def _router_params(w_rg, b_rg, w_re, b_re):
    d = w_rg.shape[0]
    gap = ROUTER_EXPERT_ROW - N_GROUPS
    w = jnp.concatenate([w_rg, jnp.zeros((d, gap), F32),
                         jnp.transpose(w_re, (1, 0, 2)).reshape(d, N_EXPERTS),
                         jnp.zeros((d, V7X_LANES - ROUTER_ROWS), F32)], axis=1)
    w_hi = w.astype(BF16)
    w_lo = (w - w_hi.astype(F32)).astype(BF16)
    b_t = jnp.concatenate([b_rg, jnp.zeros((gap,), F32), b_re.reshape(N_EXPERTS)])
    return jnp.concatenate([w_hi, w_lo], axis=1), b_t.reshape(ROUTER_ROWS, 1)


def _meta(meta_ref, row, col=0):
    return meta_ref[row * PLAN_LANES + col]


def _dispatch_kernel(pos_ref, h_ref, zeros_hbm, x_hbm, hbuf, sem):
    del zeros_hbm
    i = pl.program_id(0)
    tt = h_ref.shape[0]
    n_tok = tt * pl.num_programs(0)
    slot = lax.rem(i, 2)

    def drain(s):
        for k in range(2):
            pltpu.make_async_copy(hbuf.at[s], x_hbm.at[pl.ds(0, tt), :], sem.at[s]).wait()

    @pl.when(i >= 2)
    def _():
        drain(slot)

    hbuf[slot] = h_ref[...]
    base = i * tt

    def start(r, c):
        for k in range(2):
            p = pos_ref[k * n_tok + base + r]
            pltpu.make_async_copy(hbuf.at[slot, pl.ds(r, 1), :], x_hbm.at[pl.ds(p, 1), :],
                                  sem.at[slot]).start()
        return c

    lax.fori_loop(0, tt, start, 0, unroll=True)

    @pl.when(i == pl.num_programs(0) - 1)
    def _():
        @pl.when(i >= 1)
        def _():
            drain(1 - slot)
        drain(slot)


def _dispatch(h, pos, zeros):
    t, d = h.shape
    tt = min(DISPATCH_TT, t)
    grid_spec = pltpu.PrefetchScalarGridSpec(
        num_scalar_prefetch=1,
        grid=(t // tt,),
        in_specs=[pl.BlockSpec((tt, d), lambda i, pos: (i, 0)), pl.BlockSpec(memory_space=pl.ANY)],
        out_specs=pl.BlockSpec(memory_space=pl.ANY),
        scratch_shapes=[pltpu.VMEM((2, tt, d), F32), pltpu.SemaphoreType.DMA((2,))],
    )
    return pl.pallas_call(
        _dispatch_kernel,
        grid_spec=grid_spec,
        out_shape=jax.ShapeDtypeStruct(zeros.shape, zeros.dtype),
        input_output_aliases={2: 0},
        compiler_params=_params("arbitrary"),
        name="dispatch",
    )(pos, h, zeros)


def _experts_kernel(meta_ref, x_ref, wg_hbm, wu_hbm, wd_hbm, y_ref, wgf, wuf, wdf, wgb, wub, wdb, wsem):
    i = pl.program_id(0)
    n_tiles = _meta(meta_ref, META_N_TILES)
    n_act = _meta(meta_ref, META_N_ACTIVE)

    def weight_copies(q):
        e = _meta(meta_ref, META_ACTIVE, q)
        return (pltpu.make_async_copy(wg_hbm.at[e], wgf, wsem.at[0]),
                pltpu.make_async_copy(wu_hbm.at[e], wuf, wsem.at[1]),
                pltpu.make_async_copy(wd_hbm.at[e], wdf, wsem.at[2]))

    @pl.when(i == 0)
    def _():
        for c in weight_copies(0):
            c.start()

    @pl.when(i < n_tiles)
    def _():
        q = _meta(meta_ref, META_TILE_SEQ, i)

        @pl.when((i == 0) | (q != _meta(meta_ref, META_TILE_SEQ, jnp.maximum(i - 1, 0))))
        def _():
            for c in weight_copies(q):
                c.wait()
            wgb[...] = wgf[...].astype(BF16)
            wub[...] = wuf[...].astype(BF16)
            wdb[...] = wdf[...].astype(BF16)

            @pl.when(q + 1 < n_act)
            def _():
                for c in weight_copies(q + 1):
                    c.start()

        x = x_ref[...].astype(BF16)
        gate = jnp.dot(x, wgb[...], preferred_element_type=F32)
        up = jnp.dot(x, wub[...], preferred_element_type=F32)
        act = (gate * jax.nn.sigmoid(gate)) * up
        y_ref[...] = jnp.dot(act.astype(BF16), wdb[...], preferred_element_type=F32)

    @pl.when(i >= n_tiles)
    def _():
        y_ref[...] = jnp.zeros_like(y_ref)


def _experts(x_sorted, meta, w_gate, w_up, w_down):
    d = x_sorted.shape[1]
    f = w_gate.shape[2]
    tm = MOE_TM
    max_tiles = x_sorted.shape[0] // tm
    any_spec = pl.BlockSpec(memory_space=pl.ANY)
    grid_spec = pltpu.PrefetchScalarGridSpec(
        num_scalar_prefetch=1,
        grid=(max_tiles,),
        in_specs=[pl.BlockSpec((tm, d), lambda i, meta: (jnp.minimum(i, _meta(meta, META_N_TILES) - 1), 0)),
                  any_spec, any_spec, any_spec],
        out_specs=pl.BlockSpec((tm, d), lambda i, meta: (i, 0)),
        scratch_shapes=[pltpu.VMEM((d, f), F32), pltpu.VMEM((d, f), F32), pltpu.VMEM((f, d), F32),
                        pltpu.VMEM((d, f), BF16), pltpu.VMEM((d, f), BF16), pltpu.VMEM((f, d), BF16),
                        pltpu.SemaphoreType.DMA((3,))],
    )
    return pl.pallas_call(
        _experts_kernel,
        grid_spec=grid_spec,
        out_shape=jax.ShapeDtypeStruct((max_tiles * tm, d), F32),
        compiler_params=_params("arbitrary"),
        name="experts",
    )(meta, x_sorted, w_gate, w_up, w_down)


def _combine_kernel(pos_ref, x_ref, cw_ref, y_hbm, nw_ref, o_ref, ybuf, sem, *, final_norm):
    i = pl.program_id(0)
    tm = x_ref.shape[0]
    n_tok = tm * pl.num_programs(0)
    slot = lax.rem(i, 2)

    def start_gather(tile, dst_slot, unroll):
        base = tile * tm

        def start(r, c):
            for k in range(2):
                p = pos_ref[k * n_tok + base + r]
                pltpu.make_async_copy(y_hbm.at[pl.ds(p, 1), :], ybuf.at[dst_slot, k, pl.ds(r, 1), :],
                                      sem.at[dst_slot]).start()
            return c

        lax.fori_loop(0, tm, start, 0, unroll=unroll)

    @pl.when(i == 0)
    def _():
        start_gather(0, 0, unroll=8)

    @pl.when(i + 1 < pl.num_programs(0))
    def _():
        start_gather(i + 1, 1 - slot, unroll=True)

    for k in range(2):
        pltpu.make_async_copy(y_hbm.at[pl.ds(0, tm), :], ybuf.at[slot, k], sem.at[slot]).wait()
    cw = cw_ref[...]
    x = x_ref[...] + (cw[:, 0:1] * ybuf[slot, 0] + cw[:, 1:2] * ybuf[slot, 1])
    if final_norm:
        ms = jnp.mean(x * x, axis=-1, keepdims=True)
        x = (x * lax.rsqrt(ms + RMS_EPS)) * nw_ref[...]
    o_ref[...] = x


def _combine(x1, cw, y_sorted, pos, norm_w, final_norm):
    t, d = x1.shape
    tm = min(COMBINE_TM, t)
    grid_spec = pltpu.PrefetchScalarGridSpec(
        num_scalar_prefetch=1,
        grid=(t // tm,),
        in_specs=[pl.BlockSpec((tm, d), lambda i, pos: (i, 0)),
                  pl.BlockSpec((tm, 2), lambda i, pos: (i, 0)),
                  pl.BlockSpec(memory_space=pl.ANY),
                  pl.BlockSpec((1, d), lambda i, pos: (0, 0))],
        out_specs=pl.BlockSpec((tm, d), lambda i, pos: (i, 0)),
        scratch_shapes=[pltpu.VMEM((2, 2, tm, d), F32), pltpu.SemaphoreType.DMA((2,))],
    )
    return pl.pallas_call(
        functools.partial(_combine_kernel, final_norm=final_norm),
        grid_spec=grid_spec,
        out_shape=jax.ShapeDtypeStruct((t, d), F32),
        compiler_params=_params("arbitrary"),
        name="combine",
    )(pos, x1, cw, y_sorted, norm_w.reshape(1, d))


def _rope_tables(s):
    half = HEAD_DIM_RET_QK // 2
    inv = ROPE_BASE ** (-jnp.arange(half, dtype=F32) / half)
    ang = jnp.arange(s).astype(F32)[:, None] * inv[None, :]
    cos, sin = jnp.cos(ang), jnp.sin(ang)
    return jnp.concatenate([cos, cos], axis=-1), jnp.concatenate([-sin, sin], axis=-1)


META_TILE_SEQ, META_ACTIVE, META_N_ACTIVE, META_N_TILES = 0, 1, 2, 3
META_ROWS = 8
PLAN_LANES = V7X_LANES


def _plan_kernel(route_ref, pfx_ref, tri_ref, pos_ref, meta_ref, *, tm):
    ne, lanes = N_EXPERTS, PLAN_LANES
    t = route_ref.shape[1]
    nblk = t // lanes
    ids = lax.broadcasted_iota(jnp.int32, (ne, lanes), 0).astype(F32)

    def onehots(blk):
        cols = slice(blk * lanes, (blk + 1) * lanes)
        return [(ids == route_ref[2 + k:3 + k, cols]).astype(F32) for k in range(2)]

    stacked = jnp.concatenate([sum(onehots(blk)) for blk in range(nblk)], axis=0)
    prod = jnp.dot(stacked.astype(BF16), pfx_ref[...], preferred_element_type=F32)
    carry = jnp.zeros((ne, lanes), F32)
    before = []
    for blk in range(nblk):
        rows = slice(blk * ne, (blk + 1) * ne)
        before.append(prod[rows, :lanes] + carry)
        carry = carry + prod[rows, lanes:]
    counts = carry
    tiles_per = jnp.floor((counts + (tm - 1.0)) * (1.0 / tm))
    tri = tri_ref[...]
    tile_end = jnp.dot(tri, tiles_per.astype(BF16), preferred_element_type=F32)
    row_start = (tile_end - tiles_per) * tm
    for k in range(2):
        pieces = []
        for blk in range(nblk):
            oh = onehots(blk)[k]
            pieces.append(jnp.sum(oh * (before[blk] + row_start), axis=0, keepdims=True))
        pos_ref[k:k + 1, :] = jnp.concatenate(pieces, axis=1).astype(jnp.int32)

    owns = (tiles_per > 0.0).astype(F32)
    seq = jnp.dot(tri, owns.astype(BF16), preferred_element_type=F32) - 1.0
    n_active = seq[ne - 1:ne, :] + 1.0
    n_tiles = tile_end[ne - 1:ne, :]
    lane = lax.broadcasted_iota(jnp.int32, (ne, lanes), 1).astype(F32)
    tile = jnp.minimum(lane, n_tiles - 1.0)
    tile_seq = jnp.sum(owns * (tile_end <= tile).astype(F32), axis=0, keepdims=True)
    active = jnp.sum(jnp.where((owns > 0.0) & (seq == lane), ids, 0.0), axis=0, keepdims=True)
    zero = jnp.zeros((1, lanes), F32)
    rows = {META_TILE_SEQ: tile_seq, META_ACTIVE: active, META_N_ACTIVE: n_active, META_N_TILES: n_tiles}
    meta_ref[...] = jnp.concatenate([rows.get(r, zero) for r in range(META_ROWS)], axis=0).astype(jnp.int32)


def _plan(route_t, tm, max_tiles):
    t = route_t.shape[1]
    assert max_tiles <= PLAN_LANES and N_EXPERTS <= PLAN_LANES and t % PLAN_LANES == 0
    r = jnp.arange(PLAN_LANES)
    prefix = jnp.concatenate([(r[:, None] < r[None, :]).astype(BF16),
                              jnp.ones((PLAN_LANES, PLAN_LANES), BF16)], axis=1)
    e = jnp.arange(N_EXPERTS)
    tri = (e[None, :] <= e[:, None]).astype(BF16)
    pos, meta = pl.pallas_call(
        functools.partial(_plan_kernel, tm=tm),
        out_shape=[jax.ShapeDtypeStruct((2, t), jnp.int32),
                   jax.ShapeDtypeStruct((META_ROWS, PLAN_LANES), jnp.int32)],
        compiler_params=pltpu.CompilerParams(vmem_limit_bytes=V7X_VMEM_LIMIT_BYTES),
        name="plan",
    )(route_t, prefix, tri)
    return pos.reshape(2 * t), meta.reshape(META_ROWS * PLAN_LANES)


def kernel(x, norm_mix_w, w_in, ret_gn_w, w_branch_moba, w_branch_ret, w_out, norm_ffn_w, w_router_group, b_router_group, w_router_expert, b_router_expert, w_expert_gate, w_expert_up, w_expert_down, norm_final_w):
    b, s, d = x.shape
    t = b * s
    depth = w_in.shape[0]
    assert s % MOBA_BLOCK == 0 and s % RET_CHUNK == 0 and t % MOE_TM == 0
    cos_t, sin_t = _rope_tables(s)
    log_gamma = jnp.log(1.0 - 2.0 ** (-5.0 - jnp.arange(N_HEADS_RET, dtype=F32)))
    x2 = x.reshape(t, d)
    for l in range(depth):
        max_tiles = 2 * t // MOE_TM + N_EXPERTS
        proj, zeros = _inproj(x2, norm_mix_w[l], w_in[l], max_tiles * MOE_TM)
        proj3 = proj.reshape(b, s, proj.shape[1])
        o_a = _moba(proj3).reshape(t, -1)
        y_r = _retention(proj3, cos_t, sin_t, ret_gn_w[l], log_gamma).reshape(t, -1)
        mix = _merge(o_a, y_r, proj, w_branch_moba[l], w_branch_ret[l])
        w_router_t, b_router_t = _router_params(w_router_group[l], b_router_group[l],
                                                w_router_expert[l], b_router_expert[l])
        x1, h, route_t = _outproj(x2, mix, w_out[l].astype(BF16), norm_ffn_w[l], w_router_t, b_router_t)
        pos, meta = _plan(route_t, MOE_TM, max_tiles)
        x_sorted = _dispatch(h, pos, zeros)
        y_sorted = _experts(x_sorted, meta, w_expert_gate[l], w_expert_up[l], w_expert_down[l])
        x2 = _combine(x1, route_t[0:2].T, y_sorted, pos, norm_final_w, final_norm=(l == depth - 1))
    return x2.reshape(b, s, d)
```

```python
import functools

import jax
import jax.numpy as jnp
import numpy as np
from jax import lax
from jax.experimental import pallas as pl
from jax.experimental.pallas import tpu as pltpu

F32 = jnp.float32
BF16 = jnp.bfloat16

V7X_LANES = 128
V7X_VMEM_LIMIT_BYTES = 56 * 1024 * 1024

N_HEADS_MOBA = 8
HEAD_DIM_MOBA = 128
MOBA_BLOCK = 256
MOBA_TOPK = 3
N_HEADS_RET = 8
HEAD_DIM_RET_QK = 128
HEAD_DIM_RET_V = 256
ROPE_BASE = 10000.0
N_GROUPS = 4
EXPERTS_PER_GROUP = 8
N_EXPERTS = N_GROUPS * EXPERTS_PER_GROUP
RMS_EPS = 1e-6
GN_EPS = 1e-6
NEG_BIG = -1e30

ROUTER_EXPERT_ROW = 8
ROUTER_ROWS = ROUTER_EXPERT_ROW + N_EXPERTS
ROUTE_OUT_ROWS = 8
MOBA_HEADS_PER_STEP = 4
MOBA_BIAS_ROWS = 16
MOBA_SUM_ROWS = 16
RET_CHUNK = 256
RET_HEADS_PER_STEP = 2
INPROJ_TM, INPROJ_TN = 1024, 1024
INPROJ_MSUB = 2
MERGE_TM, MERGE_TN = 512, 1024
OUTPROJ_TM = 512
OUTPROJ_SUB = 4
MOE_TM = 256
DISPATCH_TT = 256
COMBINE_TM = 256


def _params(*semantics):
    return pltpu.CompilerParams(dimension_semantics=semantics,
                                vmem_limit_bytes=V7X_VMEM_LIMIT_BYTES)


def _inproj_kernel(x_ref, nw_ref, w_ref, o_ref, z_ref, h_ref):
    j, m = pl.program_id(1), pl.program_id(2)

    @pl.when(j == 0)
    def _():
        x = x_ref[...]
        ms = jnp.mean(x * x, axis=-1, keepdims=True)
        h_ref[m] = ((x * lax.rsqrt(ms + RMS_EPS)) * nw_ref[...]).astype(BF16)

    z_ref[...] = jnp.zeros_like(z_ref)
    o_ref[...] = jnp.dot(h_ref[m], w_ref[...].astype(BF16), preferred_element_type=F32).astype(o_ref.dtype)


def _inproj(x2, norm_w, w_in, zero_rows):
    t, d = x2.shape
    n = w_in.shape[1]
    tm, tn = min(INPROJ_TM, t), INPROJ_TN
    msub = min(INPROJ_MSUB, t // tm)
    nj = n // tn
    n_steps = (t // (tm * msub)) * nj * msub
    zb = pl.cdiv(pl.cdiv(zero_rows, n_steps), 8) * 8
    n_zblocks = pl.cdiv(zero_rows, zb)

    def x_map(p, j, m):
        return (jnp.where(j == 0, p * msub + m, p * msub + msub - 1), 0)

    def z_map(p, j, m):
        return (jnp.minimum((p * nj + j) * msub + m, n_zblocks - 1), 0)

    return pl.pallas_call(
        _inproj_kernel,
        grid=(t // (tm * msub), nj, msub),
        in_specs=[pl.BlockSpec((tm, d), x_map),
                  pl.BlockSpec((1, d), lambda p, j, m: (0, 0)),
                  pl.BlockSpec((d, tn), lambda p, j, m: (0, j))],
        out_specs=[pl.BlockSpec((tm, tn), lambda p, j, m: (p * msub + m, j)),
                   pl.BlockSpec((zb, d), z_map)],
        out_shape=[jax.ShapeDtypeStruct((t, n), BF16), jax.ShapeDtypeStruct((zero_rows, d), F32)],
        scratch_shapes=[pltpu.VMEM((msub, tm, d), BF16)],
        compiler_params=_params("arbitrary", "arbitrary", "arbitrary"),
        name="inproj",
    )(x2, norm_w.reshape(1, d), w_in)


def _moba_kernel(q_ref, k_ref, v_ref, o_ref, qta_ref, ka_ref, vta_ref, km_ref, acc_ref, sc_ref, *, nb, hb):
    blk, hd = MOBA_BLOCK, HEAD_DIM_MOBA
    i = pl.program_id(2)

    @pl.when(i == 0)
    def _():
        lane = lax.broadcasted_iota(jnp.int32, (blk, hd), 1)
        srow = lax.broadcasted_iota(jnp.int32, (MOBA_SUM_ROWS, blk), 0)
        for hh in range(hb):
            cols = slice(hh * hd, (hh + 1) * hd)
            means = []
            for c in range(nb):
                rows = slice(c * blk, (c + 1) * blk)
                qta_ref[hh, 0:hd, rows] = q_ref[rows, cols].astype(F32).T.astype(BF16)
                qta_ref[hh, hd:2 * hd, rows] = jnp.zeros((hd, blk), BF16)
                ka_ref[hh, rows, 0:hd] = k_ref[rows, cols]
                ka_ref[hh, rows, hd:2 * hd] = (lane == c).astype(BF16)
                vta_ref[hh, 0:hd, rows] = v_ref[rows, cols].astype(F32).T.astype(BF16)
                vta_ref[hh, hd:hd + MOBA_SUM_ROWS, rows] = (srow == 0).astype(BF16)
                means.append(jnp.sum(k_ref[rows, cols].astype(F32), axis=0, keepdims=True) * (1.0 / blk))
            means += [jnp.zeros((1, hd), F32)] * (MOBA_BIAS_ROWS - nb)
            rest = jnp.concatenate(means, axis=0)
            for part in range(3):
                term = rest.astype(BF16)
                km_ref[hh, part * MOBA_BIAS_ROWS:(part + 1) * MOBA_BIAS_ROWS, :] = term
                rest = rest - term.astype(F32)

    q0 = pl.multiple_of(i * blk, blk)
    qcols = pl.ds(q0, blk)
    c_exp = (hd ** -0.5) * 1.4426950408889634
    row = lax.broadcasted_iota(jnp.int32, (MOBA_BIAS_ROWS, blk), 0)
    kpos = lax.broadcasted_iota(jnp.int32, (blk, blk), 0)
    qpos = lax.broadcasted_iota(jnp.int32, (blk, blk), 1)

    qts = [qta_ref[hh, 0:hd, qcols] for hh in range(hb)]
    gates = [jnp.dot(km_ref[hh], qts[hh], preferred_element_type=F32) for hh in range(hb)]
    own = [jnp.dot(ka_ref[hh, qcols, 0:hd], qts[hh], preferred_element_type=F32)
           for hh in range(hb)]
    for hh in range(hb):
        g3 = gates[hh]
        gate = (g3[0:MOBA_BIAS_ROWS] + g3[MOBA_BIAS_ROWS:2 * MOBA_BIAS_ROWS]) + g3[2 * MOBA_BIAS_ROWS:]
        gate = jnp.where(row < i, gate, -jnp.inf)
        bias = jnp.full((MOBA_BIAS_ROWS, blk), NEG_BIG, F32)
        for _ in range(MOBA_TOPK):
            top = jnp.max(gate, axis=0, keepdims=True)
            is_top = (gate == top) & (top > -jnp.inf)
            first = jnp.min(jnp.where(is_top, row, MOBA_BIAS_ROWS), axis=0, keepdims=True)
            pick = row == first
            bias = jnp.where(pick, 0.0, bias)
            gate = jnp.where(pick, -jnp.inf, gate)
        qta_ref[hh, hd:hd + MOBA_BIAS_ROWS, qcols] = bias.astype(BF16)

    n_trips = (i + 1) // 2

    def key_rows(t):
        return pl.ds(pl.multiple_of(t * (2 * blk), 2 * blk), 2 * blk)

    def score_dots(t):
        return [jnp.dot(ka_ref[hh, key_rows(t), :], qta_ref[hh, :, qcols], preferred_element_type=F32)
                for hh in range(hb)]

    def scores_to(t, slot):
        for hh, s in enumerate(score_dots(t)):
            sc_ref[slot, hh] = s

    scores_to(0, 0)

    init, probs = [], []
    for hh in range(hb):
        s = jnp.where(kpos <= qpos, own[hh] * c_exp, NEG_BIG)
        m0 = jnp.max(s, axis=0, keepdims=True)
        init.append(m0)
        probs.append(jnp.exp2(s - m0).astype(BF16))
    for hh in range(hb):
        acc_ref[hh] = jnp.dot(vta_ref[hh, :, qcols], probs[hh], preferred_element_type=F32)

    def fold(t, slot, ms):
        probs, out = [], []
        for hh in range(hb):
            s = sc_ref[slot, hh] * c_exp
            m_new = jnp.maximum(ms[hh], jnp.max(s, axis=0, keepdims=True))
            probs.append((jnp.exp2(ms[hh] - m_new), jnp.exp2(s - m_new).astype(BF16)))
            out.append(m_new)
        for hh in range(hb):
            alpha, p = probs[hh]
            acc_ref[hh] = alpha * acc_ref[hh] + jnp.dot(vta_ref[hh, :, key_rows(t)], p,
                                                        preferred_element_type=F32)
        return tuple(out)

    def two_trips(u, ms):
        t = 2 * u
        scores_to(t + 1, 1)
        ms = fold(t, 0, ms)
        scores_to(t + 2, 0)
        return fold(t + 1, 1, ms)

    n_loop = jnp.maximum(n_trips - 1, 0) // 2
    ms = lax.fori_loop(0, n_loop, two_trips, tuple(init))
    t_tail = 2 * n_loop

    @pl.when(n_trips - t_tail == 2)
    def _():
        scores_to(t_tail + 1, 1)
        fold(t_tail + 1, 1, fold(t_tail, 0, ms))

    @pl.when(n_trips - t_tail == 1)
    def _():
        fold(t_tail, 0, ms)

    for hh in range(hb):
        acc = acc_ref[hh]
        o_ref[:, hh * hd:(hh + 1) * hd] = (acc[0:hd, :] / acc[hd:hd + 1, :]).T.astype(o_ref.dtype)


def _moba(proj3):
    b, s, _ = proj3.shape
    nb = s // MOBA_BLOCK
    assert nb % 2 == 0 and nb <= MOBA_BIAS_ROWS
    hd, nh, hb = HEAD_DIM_MOBA, N_HEADS_MOBA, MOBA_HEADS_PER_STEP
    ng = nh // hb
    return pl.pallas_call(
        functools.partial(_moba_kernel, nb=nb, hb=hb),
        grid=(b, ng, nb),
        in_specs=[pl.BlockSpec((None, s, hb * hd), lambda bi, g, i: (bi, 0, g)),
                  pl.BlockSpec((None, s, hb * hd), lambda bi, g, i: (bi, 0, ng + g)),
                  pl.BlockSpec((None, s, hb * hd), lambda bi, g, i: (bi, 0, 2 * ng + g))],
        out_specs=pl.BlockSpec((None, MOBA_BLOCK, hb * hd), lambda bi, g, i: (bi, i, g)),
        out_shape=jax.ShapeDtypeStruct((b, s, nh * hd), BF16),
        scratch_shapes=[pltpu.VMEM((hb, 2 * hd, s), BF16), pltpu.VMEM((hb, s, 2 * hd), BF16),
                        pltpu.VMEM((hb, hd + MOBA_SUM_ROWS, s), BF16),
                        pltpu.VMEM((hb, 3 * MOBA_BIAS_ROWS, hd), BF16),
                        pltpu.VMEM((hb, hd + MOBA_SUM_ROWS, MOBA_BLOCK), F32),
                        pltpu.VMEM((2, hb, 2 * MOBA_BLOCK, MOBA_BLOCK), F32)],
        compiler_params=_params("parallel", "parallel", "arbitrary"),
        name="moba",
    )(proj3, proj3, proj3)


def _retention_kernel(lg_ref, q_ref, k_ref, v_ref, g_ref, cos_ref, sin_ref, gnw_ref, o_ref,
                      decay_ref, xi_ref, zeta_ref, st_ref, *, n_chunks, hb):
    c_len = RET_CHUNK
    dk, dv = HEAD_DIM_RET_QK, HEAD_DIM_RET_V
    ri = lax.broadcasted_iota(jnp.int32, (c_len, c_len), 0).astype(F32)
    ci = lax.broadcasted_iota(jnp.int32, (c_len, c_len), 1).astype(F32)
    diff = ri - ci
    idx = lax.broadcasted_iota(jnp.int32, (c_len, dk), 0).astype(F32)
    chunk_decay = []
    for hh in range(hb):
        lg = lg_ref[pl.program_id(1) * hb + hh]
        decay_ref[hh] = jnp.where(diff >= 0, jnp.exp(lg * jnp.maximum(diff, 0.0)), 0.0)
        xi_ref[hh] = jnp.exp(lg * (idx + 1.0))
        zeta_ref[hh] = jnp.exp(lg * (c_len - 1.0 - idx))
        chunk_decay.append(jnp.exp(jnp.zeros((1, dv), F32) + lg * c_len))
        st_ref[hh] = jnp.zeros((dk, dv), F32)
    k_scale = dk ** -0.5
    nt = (((1,), (1,)), ((), ()))

    def chunk(c, carry):
        r0 = pl.multiple_of(c * c_len, c_len)
        rows = pl.ds(r0, c_len)
        cs, sn = cos_ref[rows, :], sin_ref[rows, :]
        qbs, kbs, qxs, kzs = [], [], [], []
        for hh in range(hb):
            q = q_ref[rows, hh * dk:(hh + 1) * dk].astype(F32)
            k = k_ref[rows, hh * dk:(hh + 1) * dk].astype(F32)
            qr = q * cs + pltpu.roll(q, dk // 2, 1) * sn
            kr = (k * cs + pltpu.roll(k, dk // 2, 1) * sn) * k_scale
            qbs.append(qr.astype(BF16))
            kbs.append(kr.astype(BF16))
            qxs.append((qr * xi_ref[hh]).astype(BF16))
            kzs.append((kr * zeta_ref[hh]).T.astype(BF16))
        vs = [v_ref[rows, hh * dv:(hh + 1) * dv] for hh in range(hb)]
        scores = [lax.dot_general(qbs[hh], kbs[hh], nt, preferred_element_type=F32) for hh in range(hb)]
        cross = [jnp.dot(qxs[hh], st_ref[hh].astype(BF16), preferred_element_type=F32) for hh in range(hb)]
        upd = [jnp.dot(kzs[hh], vs[hh], preferred_element_type=F32) for hh in range(hb)]
        sd = [(scores[hh] * decay_ref[hh]).astype(BF16) for hh in range(hb)]
        inner = [jnp.dot(sd[hh], vs[hh], preferred_element_type=F32) for hh in range(hb)]
        for hh in range(hb):
            st_ref[hh] = chunk_decay[hh] * st_ref[hh] + upd[hh]
            o = inner[hh] + cross[hh]
            mu = jnp.mean(o, axis=-1, keepdims=True)
            var = jnp.mean(jnp.square(o - mu), axis=-1, keepdims=True)
            yn = (o - mu) * lax.rsqrt(var + GN_EPS)
            cols = slice(hh * dv, (hh + 1) * dv)
            g = g_ref[rows, cols].astype(F32)
            y = (g * jax.nn.sigmoid(g)) * (yn * gnw_ref[:, cols])
            o_ref[rows, cols] = y.astype(o_ref.dtype)
        return carry

    lax.fori_loop(0, n_chunks, chunk, 0, unroll=2)


def _retention(proj3, cos_t, sin_t, gn_w, log_gamma):
    b, s, _ = proj3.shape
    nh, dk, dv, hb = N_HEADS_RET, HEAD_DIM_RET_QK, HEAD_DIM_RET_V, RET_HEADS_PER_STEP
    ng = nh // hb
    q_blk = 3 * N_HEADS_MOBA * HEAD_DIM_MOBA // (hb * dk)
    k_blk = q_blk + ng
    v_blk = (k_blk + ng) * dk // dv
    g_blk = v_blk + ng
    grid_spec = pltpu.PrefetchScalarGridSpec(
        num_scalar_prefetch=1,
        grid=(b, ng),
        in_specs=[pl.BlockSpec((None, s, hb * dk), lambda bi, g, lg: (bi, 0, q_blk + g)),
                  pl.BlockSpec((None, s, hb * dk), lambda bi, g, lg: (bi, 0, k_blk + g)),
                  pl.BlockSpec((None, s, hb * dv), lambda bi, g, lg: (bi, 0, v_blk + g)),
                  pl.BlockSpec((None, s, hb * dv), lambda bi, g, lg: (bi, 0, g_blk + g)),
                  pl.BlockSpec((s, dk), lambda bi, g, lg: (0, 0)),
                  pl.BlockSpec((s, dk), lambda bi, g, lg: (0, 0)),
                  pl.BlockSpec((1, hb * dv), lambda bi, g, lg: (0, g))],
        out_specs=pl.BlockSpec((None, s, hb * dv), lambda bi, g, lg: (bi, 0, g)),
        scratch_shapes=[pltpu.VMEM((hb, RET_CHUNK, RET_CHUNK), F32), pltpu.VMEM((hb, RET_CHUNK, dk), F32),
                        pltpu.VMEM((hb, RET_CHUNK, dk), F32), pltpu.VMEM((hb, dk, dv), F32)],
    )
    return pl.pallas_call(
        functools.partial(_retention_kernel, n_chunks=s // RET_CHUNK, hb=hb),
        grid_spec=grid_spec,
        out_shape=jax.ShapeDtypeStruct((b, s, nh * dv), BF16),
        compiler_params=_params("parallel", "parallel"),
        name="retention",
    )(log_gamma, proj3, proj3, proj3, proj3, cos_t, sin_t, gn_w.reshape(1, nh * dv))


def _merge_kernel(oa_ref, yr_ref, ga_ref, gr_ref, wa_ref, wr_ref, o_ref, wab_ref, wrb_ref):
    @pl.when(pl.program_id(1) == 0)
    def _():
        wab_ref[...] = wa_ref[...].astype(BF16)
        wrb_ref[...] = wr_ref[...].astype(BF16)

    a = jnp.dot(oa_ref[...], wab_ref[...], preferred_element_type=F32)
    r = jnp.dot(yr_ref[...], wrb_ref[...], preferred_element_type=F32)
    mix = jax.nn.sigmoid(ga_ref[...].astype(F32)) * a + jax.nn.sigmoid(gr_ref[...].astype(F32)) * r
    o_ref[...] = mix.astype(o_ref.dtype)


def _merge(o_a, y_r, proj, w_a, w_r):
    t, wa_in = o_a.shape
    wr_in = y_r.shape[1]
    d = w_a.shape[1]
    tm, tn = min(MERGE_TM, t), MERGE_TN
    ga_blk = (proj.shape[1] - 2 * d) // tn
    gr_blk = ga_blk + d // tn
    return pl.pallas_call(
        _merge_kernel,
        grid=(d // tn, t // tm),
        in_specs=[pl.BlockSpec((tm, wa_in), lambda j, i: (i, 0)),
                  pl.BlockSpec((tm, wr_in), lambda j, i: (i, 0)),
                  pl.BlockSpec((tm, tn), lambda j, i: (i, ga_blk + j)),
                  pl.BlockSpec((tm, tn), lambda j, i: (i, gr_blk + j)),
                  pl.BlockSpec((wa_in, tn), lambda j, i: (0, j)),
                  pl.BlockSpec((wr_in, tn), lambda j, i: (0, j))],
        out_specs=pl.BlockSpec((tm, tn), lambda j, i: (i, j)),
        out_shape=jax.ShapeDtypeStruct((t, d), BF16),
        scratch_shapes=[pltpu.VMEM((wa_in, tn), BF16), pltpu.VMEM((wr_in, tn), BF16)],
        compiler_params=_params("parallel", "arbitrary"),
        name="merge",
    )(o_a, y_r, proj, proj, w_a, w_r)


def _outproj_kernel(x_ref, mix_ref, wo_ref, nw_ref, wr_ref, br_ref, x1_ref, h_ref, route_ref):
    tm = x_ref.shape[0]
    sub = tm // OUTPROJ_SUB
    blocks = [pl.ds(k * sub, sub) for k in range(OUTPROJ_SUB)]
    proj = [jnp.dot(mix_ref[rows, :], wo_ref[...], preferred_element_type=F32) for rows in blocks]
    parts = []
    for rows, pr in zip(blocks, proj):
        x1 = x_ref[rows, :] + pr
        x1_ref[rows, :] = x1
        ms = jnp.mean(x1 * x1, axis=-1, keepdims=True)
        h = (x1 * lax.rsqrt(ms + RMS_EPS)) * nw_ref[...]
        h_ref[rows, :] = h
        h_hi = h.astype(BF16)
        h_lo = (h - h_hi.astype(F32)).astype(BF16)
        r = (jnp.dot(h_hi, wr_ref[...], preferred_element_type=F32)
             + jnp.dot(h_lo, wr_ref[...], preferred_element_type=F32))
        parts.append((r[:, :V7X_LANES] + r[:, V7X_LANES:]).T[:ROUTER_ROWS, :])
    logits = jnp.concatenate(parts, axis=1) + br_ref[...]
    row = lax.broadcasted_iota(jnp.int32, logits.shape, 0)
    ninf = -jnp.inf
    big = ROUTER_ROWS
    gl = jnp.where(row < N_GROUPS, logits, ninf)
    gmax = jnp.max(gl, axis=0, keepdims=True)
    gsum = jnp.sum(jnp.exp(gl - gmax), axis=0, keepdims=True)
    gidx = jnp.min(jnp.where(gl == gmax, row, big), axis=0, keepdims=True)
    g_weight = 1.0 / gsum
    lo = ROUTER_EXPERT_ROW + EXPERTS_PER_GROUP * gidx
    in_group = (row >= lo) & (row < lo + EXPERTS_PER_GROUP)
    el = jnp.where(in_group, logits, ninf)
    emax = jnp.max(el, axis=0, keepdims=True)
    i1 = jnp.min(jnp.where(el == emax, row, big), axis=0, keepdims=True)
    el2 = jnp.where(row == i1, ninf, el)
    emax2 = jnp.max(el2, axis=0, keepdims=True)
    i2 = jnp.min(jnp.where(el2 == emax2, row, big), axis=0, keepdims=True)
    esum = jnp.sum(jnp.exp(el - emax), axis=0, keepdims=True)
    p1 = 1.0 / esum
    p2 = jnp.exp(emax2 - emax) / esum
    c1 = g_weight * (p1 / (p1 + p2))
    c2 = g_weight * (p2 / (p1 + p2))
    e1 = (i1 - ROUTER_EXPERT_ROW).astype(F32)
    e2 = (i2 - ROUTER_EXPERT_ROW).astype(F32)
    orow = lax.broadcasted_iota(jnp.int32, route_ref.shape, 0)
    route_ref[...] = jnp.where(orow == 0, c1, jnp.where(orow == 1, c2, jnp.where(orow == 2, e1,
                               jnp.where(orow == 3, e2, 0.0))))


def _outproj(x2, mix, w_out, norm_w, w_router_t, b_router_t):
    t, d = x2.shape
    tm = min(OUTPROJ_TM, t)
    row = lambda i: (i, 0)
    const = lambda i: (0, 0)
    return pl.pallas_call(
        _outproj_kernel,
        grid=(t // tm,),
        in_specs=[pl.BlockSpec((tm, d), row),
                  pl.BlockSpec((tm, d), row),
                  pl.BlockSpec((d, d), const),
                  pl.BlockSpec((1, d), const),
                  pl.BlockSpec((d, 2 * V7X_LANES), const),
                  pl.BlockSpec((ROUTER_ROWS, 1), const)],
        out_specs=[pl.BlockSpec((tm, d), row),
                   pl.BlockSpec((tm, d), row),
                   pl.BlockSpec((ROUTE_OUT_ROWS, tm), lambda i: (0, i))],
        out_shape=[jax.ShapeDtypeStruct((t, d), F32),
                   jax.ShapeDtypeStruct((t, d), F32),
                   jax.ShapeDtypeStruct((ROUTE_OUT_ROWS, t), F32)],
        compiler_params=_params("parallel"),
        name="outproj",
    )(x2, mix, w_out, norm_w.reshape(1, d), w_router_t, b_router_t)


def _router_params(w_rg, b_rg, w_re, b_re):
    d = w_rg.shape[0]
    gap = ROUTER_EXPERT_ROW - N_GROUPS
    w = jnp.concatenate([w_rg, jnp.zeros((d, gap), F32),
                         jnp.transpose(w_re, (1, 0, 2)).reshape(d, N_EXPERTS),
                         jnp.zeros((d, V7X_LANES - ROUTER_ROWS), F32)], axis=1)
    w_hi = w.astype(BF16)
    w_lo = (w - w_hi.astype(F32)).astype(BF16)
    b_t = jnp.concatenate([b_rg, jnp.zeros((gap,), F32), b_re.reshape(N_EXPERTS)])
    return jnp.concatenate([w_hi, w_lo], axis=1), b_t.reshape(ROUTER_ROWS, 1)


def _meta(meta_ref, row, col=0):
    return meta_ref[row * PLAN_LANES + col]


def _dispatch_kernel(pos_ref, h_ref, zeros_hbm, x_hbm, hbuf, sem):
    del zeros_hbm
    i = pl.program_id(0)
    tt = h_ref.shape[0]
    n_tok = tt * pl.num_programs(0)
    slot = lax.rem(i, 2)

    def drain(s):
        for k in range(2):
            pltpu.make_async_copy(hbuf.at[s], x_hbm.at[pl.ds(0, tt), :], sem.at[s]).wait()

    @pl.when(i >= 2)
    def _():
        drain(slot)

    hbuf[slot] = h_ref[...]
    base = i * tt

    def start(r, c):
        for k in range(2):
            p = pos_ref[k * n_tok + base + r]
            pltpu.make_async_copy(hbuf.at[slot, pl.ds(r, 1), :], x_hbm.at[pl.ds(p, 1), :],
                                  sem.at[slot]).start()
        return c

    lax.fori_loop(0, tt, start, 0, unroll=True)

    @pl.when(i == pl.num_programs(0) - 1)
    def _():
        @pl.when(i >= 1)
        def _():
            drain(1 - slot)
        drain(slot)


def _dispatch(h, pos, zeros):
    t, d = h.shape
    tt = min(DISPATCH_TT, t)
    grid_spec = pltpu.PrefetchScalarGridSpec(
        num_scalar_prefetch=1,
        grid=(t // tt,),
        in_specs=[pl.BlockSpec((tt, d), lambda i, pos: (i, 0)), pl.BlockSpec(memory_space=pl.ANY)],
        out_specs=pl.BlockSpec(memory_space=pl.ANY),
        scratch_shapes=[pltpu.VMEM((2, tt, d), F32), pltpu.SemaphoreType.DMA((2,))],
    )
    return pl.pallas_call(
        _dispatch_kernel,
        grid_spec=grid_spec,
        out_shape=jax.ShapeDtypeStruct(zeros.shape, zeros.dtype),
        input_output_aliases={2: 0},
        compiler_params=_params("arbitrary"),
        name="dispatch",
    )(pos, h, zeros)


def _experts_kernel(meta_ref, x_ref, wg_hbm, wu_hbm, wd_hbm, y_ref, wgf, wuf, wdf, wgb, wub, wdb, wsem):
    i = pl.program_id(0)
    n_tiles = _meta(meta_ref, META_N_TILES)
    n_act = _meta(meta_ref, META_N_ACTIVE)

    def weight_copies(q):
        e = _meta(meta_ref, META_ACTIVE, q)
        return (pltpu.make_async_copy(wg_hbm.at[e], wgf, wsem.at[0]),
                pltpu.make_async_copy(wu_hbm.at[e], wuf, wsem.at[1]),
                pltpu.make_async_copy(wd_hbm.at[e], wdf, wsem.at[2]))

    @pl.when(i == 0)
    def _():
        for c in weight_copies(0):
            c.start()

    @pl.when(i < n_tiles)
    def _():
        q = _meta(meta_ref, META_TILE_SEQ, i)

        @pl.when((i == 0) | (q != _meta(meta_ref, META_TILE_SEQ, jnp.maximum(i - 1, 0))))
        def _():
            for c in weight_copies(q):
                c.wait()
            wgb[...] = wgf[...].astype(BF16)
            wub[...] = wuf[...].astype(BF16)
            wdb[...] = wdf[...].astype(BF16)

            @pl.when(q + 1 < n_act)
            def _():
                for c in weight_copies(q + 1):
                    c.start()

        x = x_ref[...].astype(BF16)
        gate = jnp.dot(x, wgb[...], preferred_element_type=F32)
        up = jnp.dot(x, wub[...], preferred_element_type=F32)
        act = (gate * jax.nn.sigmoid(gate)) * up
        y_ref[...] = jnp.dot(act.astype(BF16), wdb[...], preferred_element_type=F32)

    @pl.when(i >= n_tiles)
    def _():
        y_ref[...] = jnp.zeros_like(y_ref)


def _experts(x_sorted, meta, w_gate, w_up, w_down):
    d = x_sorted.shape[1]
    f = w_gate.shape[2]
    tm = MOE_TM
    max_tiles = x_sorted.shape[0] // tm
    any_spec = pl.BlockSpec(memory_space=pl.ANY)
    grid_spec = pltpu.PrefetchScalarGridSpec(
        num_scalar_prefetch=1,
        grid=(max_tiles,),
        in_specs=[pl.BlockSpec((tm, d), lambda i, meta: (jnp.minimum(i, _meta(meta, META_N_TILES) - 1), 0)),
                  any_spec, any_spec, any_spec],
        out_specs=pl.BlockSpec((tm, d), lambda i, meta: (i, 0)),
        scratch_shapes=[pltpu.VMEM((d, f), F32), pltpu.VMEM((d, f), F32), pltpu.VMEM((f, d), F32),
                        pltpu.VMEM((d, f), BF16), pltpu.VMEM((d, f), BF16), pltpu.VMEM((f, d), BF16),
                        pltpu.SemaphoreType.DMA((3,))],
    )
    return pl.pallas_call(
        _experts_kernel,
        grid_spec=grid_spec,
        out_shape=jax.ShapeDtypeStruct((max_tiles * tm, d), F32),
        compiler_params=_params("arbitrary"),
        name="experts",
    )(meta, x_sorted, w_gate, w_up, w_down)


def _combine_kernel(pos_ref, x_ref, cw_ref, y_hbm, nw_ref, o_ref, ybuf, sem, *, final_norm):
    i = pl.program_id(0)
    tm = x_ref.shape[0]
    n_tok = tm * pl.num_programs(0)
    slot = lax.rem(i, 2)

    def start_gather(tile, dst_slot, unroll):
        base = tile * tm

        def start(r, c):
            for k in range(2):
                p = pos_ref[k * n_tok + base + r]
                pltpu.make_async_copy(y_hbm.at[pl.ds(p, 1), :], ybuf.at[dst_slot, k, pl.ds(r, 1), :],
                                      sem.at[dst_slot]).start()
            return c

        lax.fori_loop(0, tm, start, 0, unroll=unroll)

    @pl.when(i == 0)
    def _():
        start_gather(0, 0, unroll=8)

    @pl.when(i + 1 < pl.num_programs(0))
    def _():
        start_gather(i + 1, 1 - slot, unroll=True)

    for k in range(2):
        pltpu.make_async_copy(y_hbm.at[pl.ds(0, tm), :], ybuf.at[slot, k], sem.at[slot]).wait()
    cw = cw_ref[...]
    x = x_ref[...] + (cw[:, 0:1] * ybuf[slot, 0] + cw[:, 1:2] * ybuf[slot, 1])
    if final_norm:
        ms = jnp.mean(x * x, axis=-1, keepdims=True)
        x = (x * lax.rsqrt(ms + RMS_EPS)) * nw_ref[...]
    o_ref[...] = x


def _combine(x1, cw, y_sorted, pos, norm_w, final_norm):
    t, d = x1.shape
    tm = min(COMBINE_TM, t)
    grid_spec = pltpu.PrefetchScalarGridSpec(
        num_scalar_prefetch=1,
        grid=(t // tm,),
        in_specs=[pl.BlockSpec((tm, d), lambda i, pos: (i, 0)),
                  pl.BlockSpec((tm, 2), lambda i, pos: (i, 0)),
                  pl.BlockSpec(memory_space=pl.ANY),
                  pl.BlockSpec((1, d), lambda i, pos: (0, 0))],
        out_specs=pl.BlockSpec((tm, d), lambda i, pos: (i, 0)),
        scratch_shapes=[pltpu.VMEM((2, 2, tm, d), F32), pltpu.SemaphoreType.DMA((2,))],
    )
    return pl.pallas_call(
        functools.partial(_combine_kernel, final_norm=final_norm),
        grid_spec=grid_spec,
        out_shape=jax.ShapeDtypeStruct((t, d), F32),
        compiler_params=_params("arbitrary"),
        name="combine",
    )(pos, x1, cw, y_sorted, norm_w.reshape(1, d))


def _rope_tables(s):
    half = HEAD_DIM_RET_QK // 2
    inv = ROPE_BASE ** (-np.arange(half, dtype=np.float64) / half)
    ang = np.arange(s, dtype=np.float64)[:, None] * inv[None, :]
    cos, sin = np.cos(ang), np.sin(ang)
    return (jnp.asarray(np.concatenate([cos, cos], axis=-1), F32),
            jnp.asarray(np.concatenate([-sin, sin], axis=-1), F32))


META_TILE_SEQ, META_ACTIVE, META_N_ACTIVE, META_N_TILES = 0, 1, 2, 3
META_ROWS = 8
PLAN_LANES = V7X_LANES


def _plan_kernel(route_ref, pfx_ref, tri_ref, pos_ref, meta_ref, *, tm):
    ne, lanes = N_EXPERTS, PLAN_LANES
    t = route_ref.shape[1]
    nblk = t // lanes
    ids = lax.broadcasted_iota(jnp.int32, (ne, lanes), 0).astype(F32)

    def onehots(blk):
        cols = slice(blk * lanes, (blk + 1) * lanes)
        return [(ids == route_ref[2 + k:3 + k, cols]).astype(F32) for k in range(2)]

    stacked = jnp.concatenate([sum(onehots(blk)) for blk in range(nblk)], axis=0)
    prod = jnp.dot(stacked.astype(BF16), pfx_ref[...], preferred_element_type=F32)
    carry = jnp.zeros((ne, lanes), F32)
    before = []
    for blk in range(nblk):
        rows = slice(blk * ne, (blk + 1) * ne)
        before.append(prod[rows, :lanes] + carry)
        carry = carry + prod[rows, lanes:]
    counts = carry
    tiles_per = jnp.floor((counts + (tm - 1.0)) * (1.0 / tm))
    tri = tri_ref[...]
    tile_end = jnp.dot(tri, tiles_per.astype(BF16), preferred_element_type=F32)
    row_start = (tile_end - tiles_per) * tm
    for k in range(2):
        pieces = []
        for blk in range(nblk):
            oh = onehots(blk)[k]
            pieces.append(jnp.sum(oh * (before[blk] + row_start), axis=0, keepdims=True))
        pos_ref[k:k + 1, :] = jnp.concatenate(pieces, axis=1).astype(jnp.int32)

    owns = (tiles_per > 0.0).astype(F32)
    seq = jnp.dot(tri, owns.astype(BF16), preferred_element_type=F32) - 1.0
    n_active = seq[ne - 1:ne, :] + 1.0
    n_tiles = tile_end[ne - 1:ne, :]
    lane = lax.broadcasted_iota(jnp.int32, (ne, lanes), 1).astype(F32)
    tile = jnp.minimum(lane, n_tiles - 1.0)
    tile_seq = jnp.sum(owns * (tile_end <= tile).astype(F32), axis=0, keepdims=True)
    active = jnp.sum(jnp.where((owns > 0.0) & (seq == lane), ids, 0.0), axis=0, keepdims=True)
    zero = jnp.zeros((1, lanes), F32)
    rows = {META_TILE_SEQ: tile_seq, META_ACTIVE: active, META_N_ACTIVE: n_active, META_N_TILES: n_tiles}
    meta_ref[...] = jnp.concatenate([rows.get(r, zero) for r in range(META_ROWS)], axis=0).astype(jnp.int32)


def _plan(route_t, tm, max_tiles):
    t = route_t.shape[1]
    assert max_tiles <= PLAN_LANES and N_EXPERTS <= PLAN_LANES and t % PLAN_LANES == 0
    r = jnp.arange(PLAN_LANES)
    prefix = jnp.concatenate([(r[:, None] < r[None, :]).astype(BF16),
                              jnp.ones((PLAN_LANES, PLAN_LANES), BF16)], axis=1)
    e = jnp.arange(N_EXPERTS)
    tri = (e[None, :] <= e[:, None]).astype(BF16)
    pos, meta = pl.pallas_call(
        functools.partial(_plan_kernel, tm=tm),
        out_shape=[jax.ShapeDtypeStruct((2, t), jnp.int32),
                   jax.ShapeDtypeStruct((META_ROWS, PLAN_LANES), jnp.int32)],
        compiler_params=pltpu.CompilerParams(vmem_limit_bytes=V7X_VMEM_LIMIT_BYTES),
        name="plan",
    )(route_t, prefix, tri)
    return pos.reshape(2 * t), meta.reshape(META_ROWS * PLAN_LANES)


def kernel(x, norm_mix_w, w_in, ret_gn_w, w_branch_moba, w_branch_ret, w_out, norm_ffn_w, w_router_group, b_router_group, w_router_expert, b_router_expert, w_expert_gate, w_expert_up, w_expert_down, norm_final_w):
    b, s, d = x.shape
    t = b * s
    depth = w_in.shape[0]
    assert s % MOBA_BLOCK == 0 and s % RET_CHUNK == 0 and t % MOE_TM == 0
    cos_t, sin_t = _rope_tables(s)
    log_gamma = jnp.log(1.0 - 2.0 ** (-5.0 - jnp.arange(N_HEADS_RET, dtype=F32)))
    x2 = x.reshape(t, d)
    for l in range(depth):
        max_tiles = 2 * t // MOE_TM + N_EXPERTS
        proj, zeros = _inproj(x2, norm_mix_w[l], w_in[l], max_tiles * MOE_TM)
        proj3 = proj.reshape(b, s, proj.shape[1])
        o_a = _moba(proj3).reshape(t, -1)
        y_r = _retention(proj3, cos_t, sin_t, ret_gn_w[l], log_gamma).reshape(t, -1)
        mix = _merge(o_a, y_r, proj, w_branch_moba[l], w_branch_ret[l])
        w_router_t, b_router_t = _router_params(w_router_group[l], b_router_group[l],
                                                w_router_expert[l], b_router_expert[l])
        x1, h, route_t = _outproj(x2, mix, w_out[l].astype(BF16), norm_ffn_w[l], w_router_t, b_router_t)
        pos, meta = _plan(route_t, MOE_TM, max_tiles)
        x_sorted = _dispatch(h, pos, zeros)
        y_sorted = _experts(x_sorted, meta, w_expert_gate[l], w_expert_up[l], w_expert_down[l])
        x2 = _combine(x1, route_t[0:2].T, y_sorted, pos, norm_final_w, final_norm=(l == depth - 1))
    return x2.reshape(b, s, d)
```

```python
import functools

import jax
import jax.numpy as jnp
import numpy as np
from jax import lax
from jax.experimental import pallas as pl
from jax.experimental.pallas import tpu as pltpu

F32 = jnp.float32
BF16 = jnp.bfloat16

V7X_LANES = 128
V7X_VMEM_LIMIT_BYTES = 56 * 1024 * 1024

N_HEADS_MOBA = 8
HEAD_DIM_MOBA = 128
MOBA_BLOCK = 256
MOBA_TOPK = 3
N_HEADS_RET = 8
HEAD_DIM_RET_QK = 128
HEAD_DIM_RET_V = 256
ROPE_BASE = 10000.0
N_GROUPS = 4
EXPERTS_PER_GROUP = 8
N_EXPERTS = N_GROUPS * EXPERTS_PER_GROUP
RMS_EPS = 1e-6
GN_EPS = 1e-6
NEG_BIG = -1e30

ROUTER_EXPERT_ROW = 8
ROUTER_ROWS = ROUTER_EXPERT_ROW + N_EXPERTS
ROUTE_OUT_ROWS = 8
MOBA_HEADS_PER_STEP = 4
MOBA_BIAS_ROWS = 16
MOBA_SUM_ROWS = 16
RET_CHUNK = 256
RET_HEADS_PER_STEP = 2
INPROJ_TM, INPROJ_TN = 1024, 1024
INPROJ_MSUB = 2
MERGE_TM, MERGE_TN = 512, 1024
OUTPROJ_TM = 512
OUTPROJ_SUB = 4
MOE_TM = 256
DISPATCH_TT = 256
COMBINE_TM = 256


def _params(*semantics):
    return pltpu.CompilerParams(dimension_semantics=semantics,
                                vmem_limit_bytes=V7X_VMEM_LIMIT_BYTES)


def _inproj_kernel(x_ref, nw_ref, w_ref, o_ref, z_ref, h_ref):
    j, m = pl.program_id(1), pl.program_id(2)

    def project(h):
        z_ref[...] = jnp.zeros_like(z_ref)
        o_ref[...] = jnp.dot(h, w_ref[...].astype(BF16), preferred_element_type=F32).astype(o_ref.dtype)

    @pl.when(j == 0)
    def _():
        x = x_ref[...]
        ms = jnp.mean(x * x, axis=-1, keepdims=True)
        h = ((x * lax.rsqrt(ms + RMS_EPS)) * nw_ref[...]).astype(BF16)
        h_ref[m] = h
        project(h)

    @pl.when(j != 0)
    def _():
        project(h_ref[m])


def _inproj(x2, norm_w, w_in, zero_rows):
    t, d = x2.shape
    n = w_in.shape[1]
    tm, tn = min(INPROJ_TM, t), INPROJ_TN
    msub = min(INPROJ_MSUB, t // tm)
    nj = n // tn
    n_steps = (t // (tm * msub)) * nj * msub
    zb = pl.cdiv(pl.cdiv(zero_rows, n_steps), 8) * 8
    n_zblocks = pl.cdiv(zero_rows, zb)

    def x_map(p, j, m):
        return (jnp.where(j == 0, p * msub + m, p * msub + msub - 1), 0)

    def z_map(p, j, m):
        return (jnp.minimum((p * nj + j) * msub + m, n_zblocks - 1), 0)

    return pl.pallas_call(
        _inproj_kernel,
        grid=(t // (tm * msub), nj, msub),
        in_specs=[pl.BlockSpec((tm, d), x_map),
                  pl.BlockSpec((1, d), lambda p, j, m: (0, 0)),
                  pl.BlockSpec((d, tn), lambda p, j, m: (0, j))],
        out_specs=[pl.BlockSpec((tm, tn), lambda p, j, m: (p * msub + m, j)),
                   pl.BlockSpec((zb, d), z_map)],
        out_shape=[jax.ShapeDtypeStruct((t, n), BF16), jax.ShapeDtypeStruct((zero_rows, d), F32)],
        scratch_shapes=[pltpu.VMEM((msub, tm, d), BF16)],
        compiler_params=_params("arbitrary", "arbitrary", "arbitrary"),
        name="inproj",
    )(x2, norm_w.reshape(1, d), w_in)


def _moba_kernel(q_ref, k_ref, v_ref, o_ref, qta_ref, ka_ref, vta_ref, km_ref, acc_ref, sc_ref, *, nb, hb):
    blk, hd = MOBA_BLOCK, HEAD_DIM_MOBA
    i = pl.program_id(2)

    @pl.when(i == 0)
    def _():
        lane = lax.broadcasted_iota(jnp.int32, (blk, hd), 1)
        srow = lax.broadcasted_iota(jnp.int32, (MOBA_SUM_ROWS, blk), 0)
        for hh in range(hb):
            cols = slice(hh * hd, (hh + 1) * hd)
            means = []
            for c in range(nb):
                rows = slice(c * blk, (c + 1) * blk)
                qta_ref[hh, 0:hd, rows] = q_ref[rows, cols].astype(F32).T.astype(BF16)
                qta_ref[hh, hd:2 * hd, rows] = jnp.zeros((hd, blk), BF16)
                ka_ref[hh, rows, 0:hd] = k_ref[rows, cols]
                ka_ref[hh, rows, hd:2 * hd] = (lane == c).astype(BF16)
                vta_ref[hh, 0:hd, rows] = v_ref[rows, cols].astype(F32).T.astype(BF16)
                vta_ref[hh, hd:hd + MOBA_SUM_ROWS, rows] = (srow == 0).astype(BF16)
                means.append(jnp.sum(k_ref[rows, cols].astype(F32), axis=0, keepdims=True) * (1.0 / blk))
            means += [jnp.zeros((1, hd), F32)] * (MOBA_BIAS_ROWS - nb)
            rest = jnp.concatenate(means, axis=0)
            for part in range(3):
                term = rest.astype(BF16)
                km_ref[hh, part * MOBA_BIAS_ROWS:(part + 1) * MOBA_BIAS_ROWS, :] = term
                rest = rest - term.astype(F32)

    q0 = pl.multiple_of(i * blk, blk)
    qcols = pl.ds(q0, blk)
    c_exp = (hd ** -0.5) * 1.4426950408889634
    row = lax.broadcasted_iota(jnp.int32, (MOBA_BIAS_ROWS, blk), 0)
    kpos = lax.broadcasted_iota(jnp.int32, (blk, blk), 0)
    qpos = lax.broadcasted_iota(jnp.int32, (blk, blk), 1)

    qts = [qta_ref[hh, 0:hd, qcols] for hh in range(hb)]
    gates = [jnp.dot(km_ref[hh], qts[hh], preferred_element_type=F32) for hh in range(hb)]
    own = [jnp.dot(ka_ref[hh, qcols, 0:hd], qts[hh], preferred_element_type=F32)
           for hh in range(hb)]
    for hh in range(hb):
        g3 = gates[hh]
        gate = (g3[0:MOBA_BIAS_ROWS] + g3[MOBA_BIAS_ROWS:2 * MOBA_BIAS_ROWS]) + g3[2 * MOBA_BIAS_ROWS:]
        gate = jnp.where(row < i, gate, -jnp.inf)
        bias = jnp.full((MOBA_BIAS_ROWS, blk), NEG_BIG, F32)
        for _ in range(MOBA_TOPK):
            top = jnp.max(gate, axis=0, keepdims=True)
            is_top = (gate == top) & (top > -jnp.inf)
            first = jnp.min(jnp.where(is_top, row, MOBA_BIAS_ROWS), axis=0, keepdims=True)
            pick = row == first
            bias = jnp.where(pick, 0.0, bias)
            gate = jnp.where(pick, -jnp.inf, gate)
        qta_ref[hh, hd:hd + MOBA_BIAS_ROWS, qcols] = bias.astype(BF16)

    n_trips = (i + 1) // 2

    def key_rows(t):
        return pl.ds(pl.multiple_of(t * (2 * blk), 2 * blk), 2 * blk)

    def score_dots(t):
        return [jnp.dot(ka_ref[hh, key_rows(t), :], qta_ref[hh, :, qcols], preferred_element_type=F32)
                for hh in range(hb)]

    def scores_to(t, slot):
        for hh, s in enumerate(score_dots(t)):
            sc_ref[slot, hh] = s

    scores_to(0, 0)

    init, probs = [], []
    for hh in range(hb):
        s = jnp.where(kpos <= qpos, own[hh] * c_exp, NEG_BIG)
        m0 = jnp.max(s, axis=0, keepdims=True)
        init.append(m0)
        probs.append(jnp.exp2(s - m0).astype(BF16))
    for hh in range(hb):
        acc_ref[hh] = jnp.dot(vta_ref[hh, :, qcols], probs[hh], preferred_element_type=F32)

    def fold(t, slot, ms):
        probs, out = [], []
        for hh in range(hb):
            s = sc_ref[slot, hh] * c_exp
            m_new = jnp.maximum(ms[hh], jnp.max(s, axis=0, keepdims=True))
            probs.append((jnp.exp2(ms[hh] - m_new), jnp.exp2(s - m_new).astype(BF16)))
            out.append(m_new)
        for hh in range(hb):
            alpha, p = probs[hh]
            acc_ref[hh] = alpha * acc_ref[hh] + jnp.dot(vta_ref[hh, :, key_rows(t)], p,
                                                        preferred_element_type=F32)
        return tuple(out)

    def two_trips(u, ms):
        t = 2 * u
        scores_to(t + 1, 1)
        ms = fold(t, 0, ms)
        scores_to(t + 2, 0)
        return fold(t + 1, 1, ms)

    n_loop = jnp.maximum(n_trips - 1, 0) // 2
    ms = lax.fori_loop(0, n_loop, two_trips, tuple(init))
    t_tail = 2 * n_loop

    @pl.when(n_trips - t_tail == 2)
    def _():
        scores_to(t_tail + 1, 1)
        fold(t_tail + 1, 1, fold(t_tail, 0, ms))

    @pl.when(n_trips - t_tail == 1)
    def _():
        fold(t_tail, 0, ms)

    for hh in range(hb):
        acc = acc_ref[hh]
        o_ref[:, hh * hd:(hh + 1) * hd] = (acc[0:hd, :] / acc[hd:hd + 1, :]).T.astype(o_ref.dtype)


def _moba(proj3):
    b, s, _ = proj3.shape
    nb = s // MOBA_BLOCK
    assert nb % 2 == 0 and nb <= MOBA_BIAS_ROWS
    hd, nh, hb = HEAD_DIM_MOBA, N_HEADS_MOBA, MOBA_HEADS_PER_STEP
    ng = nh // hb
    return pl.pallas_call(
        functools.partial(_moba_kernel, nb=nb, hb=hb),
        grid=(b, ng, nb),
        in_specs=[pl.BlockSpec((None, s, hb * hd), lambda bi, g, i: (bi, 0, g)),
                  pl.BlockSpec((None, s, hb * hd), lambda bi, g, i: (bi, 0, ng + g)),
                  pl.BlockSpec((None, s, hb * hd), lambda bi, g, i: (bi, 0, 2 * ng + g))],
        out_specs=pl.BlockSpec((None, MOBA_BLOCK, hb * hd), lambda bi, g, i: (bi, i, g)),
        out_shape=jax.ShapeDtypeStruct((b, s, nh * hd), BF16),
        scratch_shapes=[pltpu.VMEM((hb, 2 * hd, s), BF16), pltpu.VMEM((hb, s, 2 * hd), BF16),
                        pltpu.VMEM((hb, hd + MOBA_SUM_ROWS, s), BF16),
                        pltpu.VMEM((hb, 3 * MOBA_BIAS_ROWS, hd), BF16),
                        pltpu.VMEM((hb, hd + MOBA_SUM_ROWS, MOBA_BLOCK), F32),
                        pltpu.VMEM((2, hb, 2 * MOBA_BLOCK, MOBA_BLOCK), F32)],
        compiler_params=_params("parallel", "parallel", "arbitrary"),
        name="moba",
    )(proj3, proj3, proj3)


def _retention_kernel(lg_ref, q_ref, k_ref, v_ref, g_ref, cos_ref, sin_ref, gnw_ref, o_ref,
                      decay_ref, xi_ref, zeta_ref, st_ref, *, n_chunks, hb):
    c_len = RET_CHUNK
    dk, dv = HEAD_DIM_RET_QK, HEAD_DIM_RET_V
    ri = lax.broadcasted_iota(jnp.int32, (c_len, c_len), 0).astype(F32)
    ci = lax.broadcasted_iota(jnp.int32, (c_len, c_len), 1).astype(F32)
    diff = ri - ci
    idx = lax.broadcasted_iota(jnp.int32, (c_len, dk), 0).astype(F32)
    chunk_decay = []
    for hh in range(hb):
        lg = lg_ref[pl.program_id(1) * hb + hh]
        decay_ref[hh] = jnp.where(diff >= 0, jnp.exp(lg * jnp.maximum(diff, 0.0)), 0.0)
        xi_ref[hh] = jnp.exp(lg * (idx + 1.0))
        zeta_ref[hh] = jnp.exp(lg * (c_len - 1.0 - idx))
        chunk_decay.append(jnp.exp(jnp.zeros((1, dv), F32) + lg * c_len))
        st_ref[hh] = jnp.zeros((dk, dv), F32)
    k_scale = dk ** -0.5
    nt = (((1,), (1,)), ((), ()))

    def chunk(c, carry):
        r0 = pl.multiple_of(c * c_len, c_len)
        rows = pl.ds(r0, c_len)
        cs, sn = cos_ref[rows, :], sin_ref[rows, :]
        qbs, kbs, qxs, kzs = [], [], [], []
        for hh in range(hb):
            q = q_ref[rows, hh * dk:(hh + 1) * dk].astype(F32)
            k = k_ref[rows, hh * dk:(hh + 1) * dk].astype(F32)
            qr = q * cs + pltpu.roll(q, dk // 2, 1) * sn
            kr = (k * cs + pltpu.roll(k, dk // 2, 1) * sn) * k_scale
            qbs.append(qr.astype(BF16))
            kbs.append(kr.astype(BF16))
            qxs.append((qr * xi_ref[hh]).astype(BF16))
            kzs.append((kr * zeta_ref[hh]).T.astype(BF16))
        vs = [v_ref[rows, hh * dv:(hh + 1) * dv] for hh in range(hb)]
        scores = [lax.dot_general(qbs[hh], kbs[hh], nt, preferred_element_type=F32) for hh in range(hb)]
        cross = [jnp.dot(qxs[hh], st_ref[hh].astype(BF16), preferred_element_type=F32) for hh in range(hb)]
        upd = [jnp.dot(kzs[hh], vs[hh], preferred_element_type=F32) for hh in range(hb)]
        sd = [(scores[hh] * decay_ref[hh]).astype(BF16) for hh in range(hb)]
        inner = [jnp.dot(sd[hh], vs[hh], preferred_element_type=F32) for hh in range(hb)]
        for hh in range(hb):
            st_ref[hh] = chunk_decay[hh] * st_ref[hh] + upd[hh]
            o = inner[hh] + cross[hh]
            mu = jnp.mean(o, axis=-1, keepdims=True)
            var = jnp.mean(jnp.square(o - mu), axis=-1, keepdims=True)
            yn = (o - mu) * lax.rsqrt(var + GN_EPS)
            cols = slice(hh * dv, (hh + 1) * dv)
            g = g_ref[rows, cols].astype(F32)
            y = (g * jax.nn.sigmoid(g)) * (yn * gnw_ref[:, cols])
            o_ref[rows, cols] = y.astype(o_ref.dtype)
        return carry

    lax.fori_loop(0, n_chunks, chunk, 0, unroll=2)


def _retention(proj3, cos_t, sin_t, gn_w, log_gamma):
    b, s, _ = proj3.shape
    nh, dk, dv, hb = N_HEADS_RET, HEAD_DIM_RET_QK, HEAD_DIM_RET_V, RET_HEADS_PER_STEP
    ng = nh // hb
    q_blk = 3 * N_HEADS_MOBA * HEAD_DIM_MOBA // (hb * dk)
    k_blk = q_blk + ng
    v_blk = (k_blk + ng) * dk // dv
    g_blk = v_blk + ng
    grid_spec = pltpu.PrefetchScalarGridSpec(
        num_scalar_prefetch=1,
        grid=(b, ng),
        in_specs=[pl.BlockSpec((None, s, hb * dk), lambda bi, g, lg: (bi, 0, q_blk + g)),
                  pl.BlockSpec((None, s, hb * dk), lambda bi, g, lg: (bi, 0, k_blk + g)),
                  pl.BlockSpec((None, s, hb * dv), lambda bi, g, lg: (bi, 0, v_blk + g)),
                  pl.BlockSpec((None, s, hb * dv), lambda bi, g, lg: (bi, 0, g_blk + g)),
                  pl.BlockSpec((s, dk), lambda bi, g, lg: (0, 0)),
                  pl.BlockSpec((s, dk), lambda bi, g, lg: (0, 0)),
                  pl.BlockSpec((1, hb * dv), lambda bi, g, lg: (0, g))],
        out_specs=pl.BlockSpec((None, s, hb * dv), lambda bi, g, lg: (bi, 0, g)),
        scratch_shapes=[pltpu.VMEM((hb, RET_CHUNK, RET_CHUNK), F32), pltpu.VMEM((hb, RET_CHUNK, dk), F32),
                        pltpu.VMEM((hb, RET_CHUNK, dk), F32), pltpu.VMEM((hb, dk, dv), F32)],
    )
    return pl.pallas_call(
        functools.partial(_retention_kernel, n_chunks=s // RET_CHUNK, hb=hb),
        grid_spec=grid_spec,
        out_shape=jax.ShapeDtypeStruct((b, s, nh * dv), BF16),
        compiler_params=_params("parallel", "parallel"),
        name="retention",
    )(log_gamma, proj3, proj3, proj3, proj3, cos_t, sin_t, gn_w.reshape(1, nh * dv))


def _merge_kernel(oa_ref, yr_ref, ga_ref, gr_ref, wa_ref, wr_ref, o_ref, wab_ref, wrb_ref):
    @pl.when(pl.program_id(1) == 0)
    def _():
        wab_ref[...] = wa_ref[...].astype(BF16)
        wrb_ref[...] = wr_ref[...].astype(BF16)

    a = jnp.dot(oa_ref[...], wab_ref[...], preferred_element_type=F32)
    r = jnp.dot(yr_ref[...], wrb_ref[...], preferred_element_type=F32)
    mix = jax.nn.sigmoid(ga_ref[...].astype(F32)) * a + jax.nn.sigmoid(gr_ref[...].astype(F32)) * r
    o_ref[...] = mix.astype(o_ref.dtype)


def _merge(o_a, y_r, proj, w_a, w_r):
    t, wa_in = o_a.shape
    wr_in = y_r.shape[1]
    d = w_a.shape[1]
    tm, tn = min(MERGE_TM, t), MERGE_TN
    ga_blk = (proj.shape[1] - 2 * d) // tn
    gr_blk = ga_blk + d // tn
    return pl.pallas_call(
        _merge_kernel,
        grid=(d // tn, t // tm),
        in_specs=[pl.BlockSpec((tm, wa_in), lambda j, i: (i, 0)),
                  pl.BlockSpec((tm, wr_in), lambda j, i: (i, 0)),
                  pl.BlockSpec((tm, tn), lambda j, i: (i, ga_blk + j)),
                  pl.BlockSpec((tm, tn), lambda j, i: (i, gr_blk + j)),
                  pl.BlockSpec((wa_in, tn), lambda j, i: (0, j)),
                  pl.BlockSpec((wr_in, tn), lambda j, i: (0, j))],
        out_specs=pl.BlockSpec((tm, tn), lambda j, i: (i, j)),
        out_shape=jax.ShapeDtypeStruct((t, d), BF16),
        scratch_shapes=[pltpu.VMEM((wa_in, tn), BF16), pltpu.VMEM((wr_in, tn), BF16)],
        compiler_params=_params("parallel", "arbitrary"),
        name="merge",
    )(o_a, y_r, proj, proj, w_a, w_r)


def _outproj_kernel(x_ref, mix_ref, wo_ref, nw_ref, wr_ref, br_ref, x1_ref, h_ref, route_ref):
    tm = x_ref.shape[0]
    sub = tm // OUTPROJ_SUB
    blocks = [pl.ds(k * sub, sub) for k in range(OUTPROJ_SUB)]
    proj = [jnp.dot(mix_ref[rows, :], wo_ref[...], preferred_element_type=F32) for rows in blocks]
    parts = []
    for rows, pr in zip(blocks, proj):
        x1 = x_ref[rows, :] + pr
        x1_ref[rows, :] = x1
        ms = jnp.mean(x1 * x1, axis=-1, keepdims=True)
        h = (x1 * lax.rsqrt(ms + RMS_EPS)) * nw_ref[...]
        h_ref[rows, :] = h
        h_hi = h.astype(BF16)
        h_lo = (h - h_hi.astype(F32)).astype(BF16)
        r = (jnp.dot(h_hi, wr_ref[...], preferred_element_type=F32)
             + jnp.dot(h_lo, wr_ref[...], preferred_element_type=F32))
        parts.append((r[:, :V7X_LANES] + r[:, V7X_LANES:]).T[:ROUTER_ROWS, :])
    logits = jnp.concatenate(parts, axis=1) + br_ref[...]
    row = lax.broadcasted_iota(jnp.int32, logits.shape, 0)
    ninf = -jnp.inf
    big = ROUTER_ROWS
    gl = jnp.where(row < N_GROUPS, logits, ninf)
    gmax = jnp.max(gl, axis=0, keepdims=True)
    gsum = jnp.sum(jnp.exp(gl - gmax), axis=0, keepdims=True)
    gidx = jnp.min(jnp.where(gl == gmax, row, big), axis=0, keepdims=True)
    g_weight = 1.0 / gsum
    lo = ROUTER_EXPERT_ROW + EXPERTS_PER_GROUP * gidx
    in_group = (row >= lo) & (row < lo + EXPERTS_PER_GROUP)
    el = jnp.where(in_group, logits, ninf)
    emax = jnp.max(el, axis=0, keepdims=True)
    i1 = jnp.min(jnp.where(el == emax, row, big), axis=0, keepdims=True)
    el2 = jnp.where(row == i1, ninf, el)
    emax2 = jnp.max(el2, axis=0, keepdims=True)
    i2 = jnp.min(jnp.where(el2 == emax2, row, big), axis=0, keepdims=True)
    esum = jnp.sum(jnp.exp(el - emax), axis=0, keepdims=True)
    p1 = 1.0 / esum
    p2 = jnp.exp(emax2 - emax) / esum
    c1 = g_weight * (p1 / (p1 + p2))
    c2 = g_weight * (p2 / (p1 + p2))
    e1 = (i1 - ROUTER_EXPERT_ROW).astype(F32)
    e2 = (i2 - ROUTER_EXPERT_ROW).astype(F32)
    orow = lax.broadcasted_iota(jnp.int32, route_ref.shape, 0)
    route_ref[...] = jnp.where(orow == 0, c1, jnp.where(orow == 1, c2, jnp.where(orow == 2, e1,
                               jnp.where(orow == 3, e2, 0.0))))


def _outproj(x2, mix, w_out, norm_w, w_router_t, b_router_t):
    t, d = x2.shape
    tm = min(OUTPROJ_TM, t)
    row = lambda i: (i, 0)
    const = lambda i: (0, 0)
    return pl.pallas_call(
        _outproj_kernel,
        grid=(t // tm,),
        in_specs=[pl.BlockSpec((tm, d), row),
                  pl.BlockSpec((tm, d), row),
                  pl.BlockSpec((d, d), const),
                  pl.BlockSpec((1, d), const),
                  pl.BlockSpec((d, 2 * V7X_LANES), const),
                  pl.BlockSpec((ROUTER_ROWS, 1), const)],
        out_specs=[pl.BlockSpec((tm, d), row),
                   pl.BlockSpec((tm, d), row),
                   pl.BlockSpec((ROUTE_OUT_ROWS, tm), lambda i: (0, i))],
        out_shape=[jax.ShapeDtypeStruct((t, d), F32),
                   jax.ShapeDtypeStruct((t, d), F32),
                   jax.ShapeDtypeStruct((ROUTE_OUT_ROWS, t), F32)],
        compiler_params=_params("parallel"),
        name="outproj",
    )(x2, mix, w_out, norm_w.reshape(1, d), w_router_t, b_router_t)


def _router_params(w_rg, b_rg, w_re, b_re):
    d = w_rg.shape[0]
    gap = ROUTER_EXPERT_ROW - N_GROUPS
    w = jnp.concatenate([w_rg, jnp.zeros((d, gap), F32),
                         jnp.transpose(w_re, (1, 0, 2)).reshape(d, N_EXPERTS),
                         jnp.zeros((d, V7X_LANES - ROUTER_ROWS), F32)], axis=1)
    w_hi = w.astype(BF16)
    w_lo = (w - w_hi.astype(F32)).astype(BF16)
    b_t = jnp.concatenate([b_rg, jnp.zeros((gap,), F32), b_re.reshape(N_EXPERTS)])
    return jnp.concatenate([w_hi, w_lo], axis=1), b_t.reshape(ROUTER_ROWS, 1)


def _meta(meta_ref, row, col=0):
    return meta_ref[row * PLAN_LANES + col]


def _dispatch_kernel(pos_ref, h_ref, zeros_hbm, x_hbm, hbuf, sem):
    del zeros_hbm
    i = pl.program_id(0)
    tt = h_ref.shape[0]
    n_tok = tt * pl.num_programs(0)
    slot = lax.rem(i, 2)

    def drain(s):
        for k in range(2):
            pltpu.make_async_copy(hbuf.at[s], x_hbm.at[pl.ds(0, tt), :], sem.at[s]).wait()

    @pl.when(i >= 2)
    def _():
        drain(slot)

    hbuf[slot] = h_ref[...]
    base = i * tt

    def start(r, c):
        for k in range(2):
            p = pos_ref[k * n_tok + base + r]
            pltpu.make_async_copy(hbuf.at[slot, pl.ds(r, 1), :], x_hbm.at[pl.ds(p, 1), :],
                                  sem.at[slot]).start()
        return c

    lax.fori_loop(0, tt, start, 0, unroll=True)

    @pl.when(i == pl.num_programs(0) - 1)
    def _():
        @pl.when(i >= 1)
        def _():
            drain(1 - slot)
        drain(slot)


def _dispatch(h, pos, zeros):
    t, d = h.shape
    tt = min(DISPATCH_TT, t)
    grid_spec = pltpu.PrefetchScalarGridSpec(
        num_scalar_prefetch=1,
        grid=(t // tt,),
        in_specs=[pl.BlockSpec((tt, d), lambda i, pos: (i, 0)), pl.BlockSpec(memory_space=pl.ANY)],
        out_specs=pl.BlockSpec(memory_space=pl.ANY),
        scratch_shapes=[pltpu.VMEM((2, tt, d), F32), pltpu.SemaphoreType.DMA((2,))],
    )
    return pl.pallas_call(
        _dispatch_kernel,
        grid_spec=grid_spec,
        out_shape=jax.ShapeDtypeStruct(zeros.shape, zeros.dtype),
        input_output_aliases={2: 0},
        compiler_params=_params("arbitrary"),
        name="dispatch",
    )(pos, h, zeros)


def _experts_kernel(meta_ref, x_ref, wg_hbm, wu_hbm, wd_hbm, y_ref, wgf, wuf, wdf, wgb, wub, wdb, wsem):
    i = pl.program_id(0)
    n_tiles = _meta(meta_ref, META_N_TILES)
    n_act = _meta(meta_ref, META_N_ACTIVE)

    def weight_copies(q):
        e = _meta(meta_ref, META_ACTIVE, q)
        return (pltpu.make_async_copy(wg_hbm.at[e], wgf, wsem.at[0]),
                pltpu.make_async_copy(wu_hbm.at[e], wuf, wsem.at[1]),
                pltpu.make_async_copy(wd_hbm.at[e], wdf, wsem.at[2]))

    @pl.when(i == 0)
    def _():
        for c in weight_copies(0):
            c.start()

    @pl.when(i < n_tiles)
    def _():
        q = _meta(meta_ref, META_TILE_SEQ, i)

        @pl.when((i == 0) | (q != _meta(meta_ref, META_TILE_SEQ, jnp.maximum(i - 1, 0))))
        def _():
            for c in weight_copies(q):
                c.wait()
            wgb[...] = wgf[...].astype(BF16)
            wub[...] = wuf[...].astype(BF16)
            wdb[...] = wdf[...].astype(BF16)

            @pl.when(q + 1 < n_act)
            def _():
                for c in weight_copies(q + 1):
                    c.start()

        x = x_ref[...].astype(BF16)
        gate = jnp.dot(x, wgb[...], preferred_element_type=F32)
        up = jnp.dot(x, wub[...], preferred_element_type=F32)
        act = (gate * jax.nn.sigmoid(gate)) * up
        y_ref[...] = jnp.dot(act.astype(BF16), wdb[...], preferred_element_type=F32)

    @pl.when(i >= n_tiles)
    def _():
        y_ref[...] = jnp.zeros_like(y_ref)


def _experts(x_sorted, meta, w_gate, w_up, w_down):
    d = x_sorted.shape[1]
    f = w_gate.shape[2]
    tm = MOE_TM
    max_tiles = x_sorted.shape[0] // tm
    any_spec = pl.BlockSpec(memory_space=pl.ANY)
    grid_spec = pltpu.PrefetchScalarGridSpec(
        num_scalar_prefetch=1,
        grid=(max_tiles,),
        in_specs=[pl.BlockSpec((tm, d), lambda i, meta: (jnp.minimum(i, _meta(meta, META_N_TILES) - 1), 0)),
                  any_spec, any_spec, any_spec],
        out_specs=pl.BlockSpec((tm, d), lambda i, meta: (i, 0)),
        scratch_shapes=[pltpu.VMEM((d, f), F32), pltpu.VMEM((d, f), F32), pltpu.VMEM((f, d), F32),
                        pltpu.VMEM((d, f), BF16), pltpu.VMEM((d, f), BF16), pltpu.VMEM((f, d), BF16),
                        pltpu.SemaphoreType.DMA((3,))],
    )
    return pl.pallas_call(
        _experts_kernel,
        grid_spec=grid_spec,
        out_shape=jax.ShapeDtypeStruct((max_tiles * tm, d), F32),
        compiler_params=_params("arbitrary"),
        name="experts",
    )(meta, x_sorted, w_gate, w_up, w_down)


def _combine_kernel(pos_ref, x_ref, cw_ref, y_hbm, nw_ref, o_ref, ybuf, sem, *, final_norm):
    i = pl.program_id(0)
    tm = x_ref.shape[0]
    n_tok = tm * pl.num_programs(0)
    slot = lax.rem(i, 2)

    def start_gather(tile, dst_slot, unroll):
        base = tile * tm

        def start(r, c):
            for k in range(2):
                p = pos_ref[k * n_tok + base + r]
                pltpu.make_async_copy(y_hbm.at[pl.ds(p, 1), :], ybuf.at[dst_slot, k, pl.ds(r, 1), :],
                                      sem.at[dst_slot]).start()
            return c

        lax.fori_loop(0, tm, start, 0, unroll=unroll)

    @pl.when(i == 0)
    def _():
        start_gather(0, 0, unroll=8)

    @pl.when(i + 1 < pl.num_programs(0))
    def _():
        start_gather(i + 1, 1 - slot, unroll=True)

    for k in range(2):
        pltpu.make_async_copy(y_hbm.at[pl.ds(0, tm), :], ybuf.at[slot, k], sem.at[slot]).wait()
    cw = cw_ref[...]
    x = x_ref[...] + (cw[:, 0:1] * ybuf[slot, 0] + cw[:, 1:2] * ybuf[slot, 1])
    if final_norm:
        ms = jnp.mean(x * x, axis=-1, keepdims=True)
        x = (x * lax.rsqrt(ms + RMS_EPS)) * nw_ref[...]
    o_ref[...] = x


def _combine(x1, cw, y_sorted, pos, norm_w, final_norm):
    t, d = x1.shape
    tm = min(COMBINE_TM, t)
    grid_spec = pltpu.PrefetchScalarGridSpec(
        num_scalar_prefetch=1,
        grid=(t // tm,),
        in_specs=[pl.BlockSpec((tm, d), lambda i, pos: (i, 0)),
                  pl.BlockSpec((tm, 2), lambda i, pos: (i, 0)),
                  pl.BlockSpec(memory_space=pl.ANY),
                  pl.BlockSpec((1, d), lambda i, pos: (0, 0))],
        out_specs=pl.BlockSpec((tm, d), lambda i, pos: (i, 0)),
        scratch_shapes=[pltpu.VMEM((2, 2, tm, d), F32), pltpu.SemaphoreType.DMA((2,))],
    )
    return pl.pallas_call(
        functools.partial(_combine_kernel, final_norm=final_norm),
        grid_spec=grid_spec,
        out_shape=jax.ShapeDtypeStruct((t, d), F32),
        compiler_params=_params("arbitrary"),
        name="combine",
    )(pos, x1, cw, y_sorted, norm_w.reshape(1, d))


def _rope_tables(s):
    half = HEAD_DIM_RET_QK // 2
    inv = ROPE_BASE ** (-np.arange(half, dtype=np.float64) / half)
    ang = np.arange(s, dtype=np.float64)[:, None] * inv[None, :]
    cos, sin = np.cos(ang), np.sin(ang)
    return (jnp.asarray(np.concatenate([cos, cos], axis=-1), F32),
            jnp.asarray(np.concatenate([-sin, sin], axis=-1), F32))


META_TILE_SEQ, META_ACTIVE, META_N_ACTIVE, META_N_TILES = 0, 1, 2, 3
META_ROWS = 8
PLAN_LANES = V7X_LANES


def _plan_kernel(route_ref, pfx_ref, tri_ref, pos_ref, meta_ref, *, tm):
    ne, lanes = N_EXPERTS, PLAN_LANES
    t = route_ref.shape[1]
    nblk = t // lanes
    ids = lax.broadcasted_iota(jnp.int32, (ne, lanes), 0).astype(F32)

    def onehots(blk):
        cols = slice(blk * lanes, (blk + 1) * lanes)
        return [(ids == route_ref[2 + k:3 + k, cols]).astype(F32) for k in range(2)]

    stacked = jnp.concatenate([sum(onehots(blk)) for blk in range(nblk)], axis=0)
    prod = jnp.dot(stacked.astype(BF16), pfx_ref[...], preferred_element_type=F32)
    carry = jnp.zeros((ne, lanes), F32)
    before = []
    for blk in range(nblk):
        rows = slice(blk * ne, (blk + 1) * ne)
        before.append(prod[rows, :lanes] + carry)
        carry = carry + prod[rows, lanes:]
    counts = carry
    tiles_per = jnp.floor((counts + (tm - 1.0)) * (1.0 / tm))
    tri = tri_ref[...]
    tile_end = jnp.dot(tri, tiles_per.astype(BF16), preferred_element_type=F32)
    row_start = (tile_end - tiles_per) * tm
    for k in range(2):
        pieces = []
        for blk in range(nblk):
            oh = onehots(blk)[k]
            pieces.append(jnp.sum(oh * (before[blk] + row_start), axis=0, keepdims=True))
        pos_ref[k:k + 1, :] = jnp.concatenate(pieces, axis=1).astype(jnp.int32)

    owns = (tiles_per > 0.0).astype(F32)
    seq = jnp.dot(tri, owns.astype(BF16), preferred_element_type=F32) - 1.0
    n_active = seq[ne - 1:ne, :] + 1.0
    n_tiles = tile_end[ne - 1:ne, :]
    lane = lax.broadcasted_iota(jnp.int32, (ne, lanes), 1).astype(F32)
    tile = jnp.minimum(lane, n_tiles - 1.0)
    tile_seq = jnp.sum(owns * (tile_end <= tile).astype(F32), axis=0, keepdims=True)
    active = jnp.sum(jnp.where((owns > 0.0) & (seq == lane), ids, 0.0), axis=0, keepdims=True)
    zero = jnp.zeros((1, lanes), F32)
    rows = {META_TILE_SEQ: tile_seq, META_ACTIVE: active, META_N_ACTIVE: n_active, META_N_TILES: n_tiles}
    meta_ref[...] = jnp.concatenate([rows.get(r, zero) for r in range(META_ROWS)], axis=0).astype(jnp.int32)


def _plan(route_t, tm, max_tiles):
    t = route_t.shape[1]
    assert max_tiles <= PLAN_LANES and N_EXPERTS <= PLAN_LANES and t % PLAN_LANES == 0
    r = jnp.arange(PLAN_LANES)
    prefix = jnp.concatenate([(r[:, None] < r[None, :]).astype(BF16),
                              jnp.ones((PLAN_LANES, PLAN_LANES), BF16)], axis=1)
    e = jnp.arange(N_EXPERTS)
    tri = (e[None, :] <= e[:, None]).astype(BF16)
    pos, meta = pl.pallas_call(
        functools.partial(_plan_kernel, tm=tm),
        out_shape=[jax.ShapeDtypeStruct((2, t), jnp.int32),
                   jax.ShapeDtypeStruct((META_ROWS, PLAN_LANES), jnp.int32)],
        compiler_params=pltpu.CompilerParams(vmem_limit_bytes=V7X_VMEM_LIMIT_BYTES),
        name="plan",
    )(route_t, prefix, tri)
    return pos.reshape(2 * t), meta.reshape(META_ROWS * PLAN_LANES)


def kernel(x, norm_mix_w, w_in, ret_gn_w, w_branch_moba, w_branch_ret, w_out, norm_ffn_w, w_router_group, b_router_group, w_router_expert, b_router_expert, w_expert_gate, w_expert_up, w_expert_down, norm_final_w):
    b, s, d = x.shape
    t = b * s
    depth = w_in.shape[0]
    assert s % MOBA_BLOCK == 0 and s % RET_CHUNK == 0 and t % MOE_TM == 0
    cos_t, sin_t = _rope_tables(s)
    log_gamma = jnp.log(1.0 - 2.0 ** (-5.0 - jnp.arange(N_HEADS_RET, dtype=F32)))
    x2 = x.reshape(t, d)
    for l in range(depth):
        max_tiles = 2 * t // MOE_TM + N_EXPERTS
        proj, zeros = _inproj(x2, norm_mix_w[l], w_in[l], max_tiles * MOE_TM)
        proj3 = proj.reshape(b, s, proj.shape[1])
        o_a = _moba(proj3).reshape(t, -1)
        y_r = _retention(proj3, cos_t, sin_t, ret_gn_w[l], log_gamma).reshape(t, -1)
        mix = _merge(o_a, y_r, proj, w_branch_moba[l], w_branch_ret[l])
        w_router_t, b_router_t = _router_params(w_router_group[l], b_router_group[l],
                                                w_router_expert[l], b_router_expert[l])
        x1, h, route_t = _outproj(x2, mix, w_out[l].astype(BF16), norm_ffn_w[l], w_router_t, b_router_t)
        pos, meta = _plan(route_t, MOE_TM, max_tiles)
        x_sorted = _dispatch(h, pos, zeros)
        y_sorted = _experts(x_sorted, meta, w_expert_gate[l], w_expert_up[l], w_expert_down[l])
        x2 = _combine(x1, route_t[0:2].T, y_sorted, pos, norm_final_w, final_norm=(l == depth - 1))
    return x2.reshape(b, s, d)
```

```python
import functools

import jax
import jax.numpy as jnp
import numpy as np
from jax import lax
from jax.experimental import pallas as pl
from jax.experimental.pallas import tpu as pltpu

F32 = jnp.float32
BF16 = jnp.bfloat16

V7X_LANES = 128
V7X_VMEM_LIMIT_BYTES = 56 * 1024 * 1024

N_HEADS_MOBA = 8
HEAD_DIM_MOBA = 128
MOBA_BLOCK = 256
MOBA_TOPK = 3
N_HEADS_RET = 8
HEAD_DIM_RET_QK = 128
HEAD_DIM_RET_V = 256
ROPE_BASE = 10000.0
N_GROUPS = 4
EXPERTS_PER_GROUP = 8
N_EXPERTS = N_GROUPS * EXPERTS_PER_GROUP
RMS_EPS = 1e-6
GN_EPS = 1e-6
NEG_BIG = -1e30

ROUTER_EXPERT_ROW = 8
ROUTER_ROWS = ROUTER_EXPERT_ROW + N_EXPERTS
ROUTE_OUT_ROWS = 8
MOBA_HEADS_PER_STEP = 4
MOBA_BIAS_ROWS = 16
MOBA_SUM_ROWS = 16
RET_CHUNK = 256
RET_HEADS_PER_STEP = 2
INPROJ_TM, INPROJ_TN = 1024, 512
INPROJ_MSUB = 4
MERGE_TM, MERGE_TN = 512, 1024
OUTPROJ_TM = 512
OUTPROJ_SUB = 4
MOE_TM = 256
DISPATCH_TT = 256
COMBINE_TM = 256


def _params(*semantics):
    return pltpu.CompilerParams(dimension_semantics=semantics,
                                vmem_limit_bytes=V7X_VMEM_LIMIT_BYTES)


def _inproj_kernel(x_ref, nw_ref, w_ref, eg_ref, eu_ref, ed_ref, o_ref, z_ref, egb_ref, eub_ref, edb_ref, h_ref):
    j, m = pl.program_id(1), pl.program_id(2)

    def project(h):
        z_ref[...] = jnp.zeros_like(z_ref)
        egb_ref[...] = eg_ref[...].astype(BF16)
        eub_ref[...] = eu_ref[...].astype(BF16)
        edb_ref[...] = ed_ref[...].astype(BF16)
        o_ref[...] = jnp.dot(h, w_ref[...].astype(BF16), preferred_element_type=F32).astype(o_ref.dtype)

    @pl.when(j == 0)
    def _():
        x = x_ref[...]
        ms = jnp.mean(x * x, axis=-1, keepdims=True)
        h = ((x * lax.rsqrt(ms + RMS_EPS)) * nw_ref[...]).astype(BF16)
        h_ref[m] = h
        project(h)

    @pl.when(j != 0)
    def _():
        project(h_ref[m])


def _inproj(x2, norm_w, w_in, zero_rows, expert_weights):
    t, d = x2.shape
    n = w_in.shape[1]
    tm, tn = min(INPROJ_TM, t), INPROJ_TN
    msub = min(INPROJ_MSUB, t // tm)
    nj = n // tn
    n_steps = (t // (tm * msub)) * nj * msub

    def x_map(p, j, m):
        return (jnp.where(j == 0, p * msub + m, p * msub + msub - 1), 0)

    def side_stream(rows, cols, align):
        rb = pl.cdiv(pl.cdiv(rows, n_steps), align) * align
        n_blocks = pl.cdiv(rows, rb)
        return pl.BlockSpec((rb, cols), lambda p, j, m: (jnp.minimum((p * nj + j) * msub + m, n_blocks - 1), 0))

    flat = [w.reshape(w.shape[0] * w.shape[1], w.shape[2]) for w in expert_weights]
    expert_specs = [side_stream(w.shape[0], w.shape[1], 16) for w in flat]
    outs = pl.pallas_call(
        _inproj_kernel,
        grid=(t // (tm * msub), nj, msub),
        in_specs=[pl.BlockSpec((tm, d), x_map),
                  pl.BlockSpec((1, d), lambda p, j, m: (0, 0)),
                  pl.BlockSpec((d, tn), lambda p, j, m: (0, j))] + expert_specs,
        out_specs=[pl.BlockSpec((tm, tn), lambda p, j, m: (p * msub + m, j)),
                   side_stream(zero_rows, d, 8)] + expert_specs,
        out_shape=[jax.ShapeDtypeStruct((t, n), BF16), jax.ShapeDtypeStruct((zero_rows, d), F32)]
                  + [jax.ShapeDtypeStruct(w.shape, BF16) for w in flat],
        scratch_shapes=[pltpu.VMEM((msub, tm, d), BF16)],
        compiler_params=_params("arbitrary", "arbitrary", "arbitrary"),
        name="inproj",
    )(x2, norm_w.reshape(1, d), w_in, *flat)
    return outs[0], outs[1], [o.reshape(w.shape) for o, w in zip(outs[2:], expert_weights)]


def _moba_kernel(q_ref, k_ref, v_ref, o_ref, qta_ref, ka_ref, vta_ref, km_ref, acc_ref, sc_ref, *, nb, hb):
    blk, hd = MOBA_BLOCK, HEAD_DIM_MOBA
    i = pl.program_id(2)

    @pl.when(i == 0)
    def _():
        lane = lax.broadcasted_iota(jnp.int32, (blk, hd), 1)
        srow = lax.broadcasted_iota(jnp.int32, (MOBA_SUM_ROWS, blk), 0)
        for hh in range(hb):
            cols = slice(hh * hd, (hh + 1) * hd)
            means = []
            for c in range(nb):
                rows = slice(c * blk, (c + 1) * blk)
                qta_ref[hh, 0:hd, rows] = q_ref[rows, cols].astype(F32).T.astype(BF16)
                qta_ref[hh, hd:2 * hd, rows] = jnp.zeros((hd, blk), BF16)
                ka_ref[hh, rows, 0:hd] = k_ref[rows, cols]
                ka_ref[hh, rows, hd:2 * hd] = (lane == c).astype(BF16)
                vta_ref[hh, 0:hd, rows] = v_ref[rows, cols].astype(F32).T.astype(BF16)
                vta_ref[hh, hd:hd + MOBA_SUM_ROWS, rows] = (srow == 0).astype(BF16)
                means.append(jnp.sum(k_ref[rows, cols].astype(F32), axis=0, keepdims=True) * (1.0 / blk))
            means += [jnp.zeros((1, hd), F32)] * (MOBA_BIAS_ROWS - nb)
            rest = jnp.concatenate(means, axis=0)
            for part in range(3):
                term = rest.astype(BF16)
                km_ref[hh, part * MOBA_BIAS_ROWS:(part + 1) * MOBA_BIAS_ROWS, :] = term
                rest = rest - term.astype(F32)

    q0 = pl.multiple_of(i * blk, blk)
    qcols = pl.ds(q0, blk)
    c_exp = (hd ** -0.5) * 1.4426950408889634
    row = lax.broadcasted_iota(jnp.int32, (MOBA_BIAS_ROWS, blk), 0)
    kpos = lax.broadcasted_iota(jnp.int32, (blk, blk), 0)
    qpos = lax.broadcasted_iota(jnp.int32, (blk, blk), 1)

    qts = [qta_ref[hh, 0:hd, qcols] for hh in range(hb)]
    gates = [jnp.dot(km_ref[hh], qts[hh], preferred_element_type=F32) for hh in range(hb)]
    own = [jnp.dot(ka_ref[hh, qcols, 0:hd], qts[hh], preferred_element_type=F32)
           for hh in range(hb)]
    for hh in range(hb):
        g3 = gates[hh]
        gate = (g3[0:MOBA_BIAS_ROWS] + g3[MOBA_BIAS_ROWS:2 * MOBA_BIAS_ROWS]) + g3[2 * MOBA_BIAS_ROWS:]
        gate = jnp.where(row < i, gate, -jnp.inf)
        bias = jnp.full((MOBA_BIAS_ROWS, blk), NEG_BIG, F32)
        for _ in range(MOBA_TOPK):
            top = jnp.max(gate, axis=0, keepdims=True)
            is_top = (gate == top) & (top > -jnp.inf)
            first = jnp.min(jnp.where(is_top, row, MOBA_BIAS_ROWS), axis=0, keepdims=True)
            pick = row == first
            bias = jnp.where(pick, 0.0, bias)
            gate = jnp.where(pick, -jnp.inf, gate)
        qta_ref[hh, hd:hd + MOBA_BIAS_ROWS, qcols] = bias.astype(BF16)

    n_trips = (i + 1) // 2

    def key_rows(t):
        return pl.ds(pl.multiple_of(t * (2 * blk), 2 * blk), 2 * blk)

    def score_dots(t):
        return [jnp.dot(ka_ref[hh, key_rows(t), :], qta_ref[hh, :, qcols], preferred_element_type=F32)
                for hh in range(hb)]

    def scores_to(t, slot):
        for hh, s in enumerate(score_dots(t)):
            sc_ref[slot, hh] = s

    scores_to(0, 0)

    init, probs = [], []
    for hh in range(hb):
        s = jnp.where(kpos <= qpos, own[hh] * c_exp, NEG_BIG)
        m0 = jnp.max(s, axis=0, keepdims=True)
        init.append(m0)
        probs.append(jnp.exp2(s - m0).astype(BF16))
    for hh in range(hb):
        acc_ref[hh] = jnp.dot(vta_ref[hh, :, qcols], probs[hh], preferred_element_type=F32)

    def fold(t, slot, ms):
        probs, out = [], []
        for hh in range(hb):
            s = sc_ref[slot, hh] * c_exp
            m_new = jnp.maximum(ms[hh], jnp.max(s, axis=0, keepdims=True))
            probs.append((jnp.exp2(ms[hh] - m_new), jnp.exp2(s - m_new).astype(BF16)))
            out.append(m_new)
        for hh in range(hb):
            alpha, p = probs[hh]
            acc_ref[hh] = alpha * acc_ref[hh] + jnp.dot(vta_ref[hh, :, key_rows(t)], p,
                                                        preferred_element_type=F32)
        return tuple(out)

    def two_trips(u, ms):
        t = 2 * u
        scores_to(t + 1, 1)
        ms = fold(t, 0, ms)
        scores_to(t + 2, 0)
        return fold(t + 1, 1, ms)

    n_loop = jnp.maximum(n_trips - 1, 0) // 2
    ms = lax.fori_loop(0, n_loop, two_trips, tuple(init))
    t_tail = 2 * n_loop

    @pl.when(n_trips - t_tail == 2)
    def _():
        scores_to(t_tail + 1, 1)
        fold(t_tail + 1, 1, fold(t_tail, 0, ms))

    @pl.when(n_trips - t_tail == 1)
    def _():
        fold(t_tail, 0, ms)

    for hh in range(hb):
        acc = acc_ref[hh]
        o_ref[:, hh * hd:(hh + 1) * hd] = (acc[0:hd, :] / acc[hd:hd + 1, :]).T.astype(o_ref.dtype)


def _moba(proj3):
    b, s, _ = proj3.shape
    nb = s // MOBA_BLOCK
    assert nb % 2 == 0 and nb <= MOBA_BIAS_ROWS
    hd, nh, hb = HEAD_DIM_MOBA, N_HEADS_MOBA, MOBA_HEADS_PER_STEP
    ng = nh // hb
    return pl.pallas_call(
        functools.partial(_moba_kernel, nb=nb, hb=hb),
        grid=(b, ng, nb),
        in_specs=[pl.BlockSpec((None, s, hb * hd), lambda bi, g, i: (bi, 0, g)),
                  pl.BlockSpec((None, s, hb * hd), lambda bi, g, i: (bi, 0, ng + g)),
                  pl.BlockSpec((None, s, hb * hd), lambda bi, g, i: (bi, 0, 2 * ng + g))],
        out_specs=pl.BlockSpec((None, MOBA_BLOCK, hb * hd), lambda bi, g, i: (bi, i, g)),
        out_shape=jax.ShapeDtypeStruct((b, s, nh * hd), BF16),
        scratch_shapes=[pltpu.VMEM((hb, 2 * hd, s), BF16), pltpu.VMEM((hb, s, 2 * hd), BF16),
                        pltpu.VMEM((hb, hd + MOBA_SUM_ROWS, s), BF16),
                        pltpu.VMEM((hb, 3 * MOBA_BIAS_ROWS, hd), BF16),
                        pltpu.VMEM((hb, hd + MOBA_SUM_ROWS, MOBA_BLOCK), F32),
                        pltpu.VMEM((2, hb, 2 * MOBA_BLOCK, MOBA_BLOCK), F32)],
        compiler_params=_params("parallel", "parallel", "arbitrary"),
        name="moba",
    )(proj3, proj3, proj3)


def _retention_kernel(lg_ref, q_ref, k_ref, v_ref, g_ref, cos_ref, sin_ref, gnw_ref, o_ref,
                      decay_ref, xi_ref, zeta_ref, st_ref, *, n_chunks, hb):
    c_len = RET_CHUNK
    dk, dv = HEAD_DIM_RET_QK, HEAD_DIM_RET_V
    ri = lax.broadcasted_iota(jnp.int32, (c_len, c_len), 0).astype(F32)
    ci = lax.broadcasted_iota(jnp.int32, (c_len, c_len), 1).astype(F32)
    diff = ri - ci
    idx = lax.broadcasted_iota(jnp.int32, (c_len, dk), 0).astype(F32)
    chunk_decay = []
    for hh in range(hb):
        lg = lg_ref[pl.program_id(1) * hb + hh]
        decay_ref[hh] = jnp.where(diff >= 0, jnp.exp(lg * jnp.maximum(diff, 0.0)), 0.0)
        xi_ref[hh] = jnp.exp(lg * (idx + 1.0))
        zeta_ref[hh] = jnp.exp(lg * (c_len - 1.0 - idx))
        chunk_decay.append(jnp.exp(jnp.zeros((1, dv), F32) + lg * c_len))
        st_ref[hh] = jnp.zeros((dk, dv), F32)
    k_scale = dk ** -0.5
    nt = (((1,), (1,)), ((), ()))

    def chunk(c, carry):
        r0 = pl.multiple_of(c * c_len, c_len)
        rows = pl.ds(r0, c_len)
        cs, sn = cos_ref[rows, :], sin_ref[rows, :]
        qbs, kbs, qxs, kzs = [], [], [], []
        for hh in range(hb):
            q = q_ref[rows, hh * dk:(hh + 1) * dk].astype(F32)
            k = k_ref[rows, hh * dk:(hh + 1) * dk].astype(F32)
            qr = q * cs + pltpu.roll(q, dk // 2, 1) * sn
            kr = (k * cs + pltpu.roll(k, dk // 2, 1) * sn) * k_scale
            qbs.append(qr.astype(BF16))
            kbs.append(kr.astype(BF16))
            qxs.append((qr * xi_ref[hh]).astype(BF16))
            kzs.append((kr * zeta_ref[hh]).T.astype(BF16))
        vs = [v_ref[rows, hh * dv:(hh + 1) * dv] for hh in range(hb)]
        scores = [lax.dot_general(qbs[hh], kbs[hh], nt, preferred_element_type=F32) for hh in range(hb)]
        cross = [jnp.dot(qxs[hh], st_ref[hh].astype(BF16), preferred_element_type=F32) for hh in range(hb)]
        upd = [jnp.dot(kzs[hh], vs[hh], preferred_element_type=F32) for hh in range(hb)]
        sd = [(scores[hh] * decay_ref[hh]).astype(BF16) for hh in range(hb)]
        inner = [jnp.dot(sd[hh], vs[hh], preferred_element_type=F32) for hh in range(hb)]
        for hh in range(hb):
            st_ref[hh] = chunk_decay[hh] * st_ref[hh] + upd[hh]
            o = inner[hh] + cross[hh]
            mu = jnp.mean(o, axis=-1, keepdims=True)
            var = jnp.mean(jnp.square(o - mu), axis=-1, keepdims=True)
            yn = (o - mu) * lax.rsqrt(var + GN_EPS)
            cols = slice(hh * dv, (hh + 1) * dv)
            g = g_ref[rows, cols].astype(F32)
            y = (g * jax.nn.sigmoid(g)) * (yn * gnw_ref[:, cols])
            o_ref[rows, cols] = y.astype(o_ref.dtype)
        return carry

    lax.fori_loop(0, n_chunks, chunk, 0, unroll=2)


def _retention(proj3, cos_t, sin_t, gn_w, log_gamma):
    b, s, _ = proj3.shape
    nh, dk, dv, hb = N_HEADS_RET, HEAD_DIM_RET_QK, HEAD_DIM_RET_V, RET_HEADS_PER_STEP
    ng = nh // hb
    q_blk = 3 * N_HEADS_MOBA * HEAD_DIM_MOBA // (hb * dk)
    k_blk = q_blk + ng
    v_blk = (k_blk + ng) * dk // dv
    g_blk = v_blk + ng
    grid_spec = pltpu.PrefetchScalarGridSpec(
        num_scalar_prefetch=1,
        grid=(b, ng),
        in_specs=[pl.BlockSpec((None, s, hb * dk), lambda bi, g, lg: (bi, 0, q_blk + g)),
                  pl.BlockSpec((None, s, hb * dk), lambda bi, g, lg: (bi, 0, k_blk + g)),
                  pl.BlockSpec((None, s, hb * dv), lambda bi, g, lg: (bi, 0, v_blk + g)),
                  pl.BlockSpec((None, s, hb * dv), lambda bi, g, lg: (bi, 0, g_blk + g)),
                  pl.BlockSpec((s, dk), lambda bi, g, lg: (0, 0)),
                  pl.BlockSpec((s, dk), lambda bi, g, lg: (0, 0)),
                  pl.BlockSpec((1, hb * dv), lambda bi, g, lg: (0, g))],
        out_specs=pl.BlockSpec((None, s, hb * dv), lambda bi, g, lg: (bi, 0, g)),
        scratch_shapes=[pltpu.VMEM((hb, RET_CHUNK, RET_CHUNK), F32), pltpu.VMEM((hb, RET_CHUNK, dk), F32),
                        pltpu.VMEM((hb, RET_CHUNK, dk), F32), pltpu.VMEM((hb, dk, dv), F32)],
    )
    return pl.pallas_call(
        functools.partial(_retention_kernel, n_chunks=s // RET_CHUNK, hb=hb),
        grid_spec=grid_spec,
        out_shape=jax.ShapeDtypeStruct((b, s, nh * dv), BF16),
        compiler_params=_params("parallel", "parallel"),
        name="retention",
    )(log_gamma, proj3, proj3, proj3, proj3, cos_t, sin_t, gn_w.reshape(1, nh * dv))


def _merge_kernel(oa_ref, yr_ref, ga_ref, gr_ref, wa_ref, wr_ref, o_ref, wab_ref, wrb_ref):
    @pl.when(pl.program_id(1) == 0)
    def _():
        wab_ref[...] = wa_ref[...].astype(BF16)
        wrb_ref[...] = wr_ref[...].astype(BF16)

    a = jnp.dot(oa_ref[...], wab_ref[...], preferred_element_type=F32)
    r = jnp.dot(yr_ref[...], wrb_ref[...], preferred_element_type=F32)
    mix = jax.nn.sigmoid(ga_ref[...].astype(F32)) * a + jax.nn.sigmoid(gr_ref[...].astype(F32)) * r
    o_ref[...] = mix.astype(o_ref.dtype)


def _merge(o_a, y_r, proj, w_a, w_r):
    t, wa_in = o_a.shape
    wr_in = y_r.shape[1]
    d = w_a.shape[1]
    tm, tn = min(MERGE_TM, t), MERGE_TN
    ga_blk = (proj.shape[1] - 2 * d) // tn
    gr_blk = ga_blk + d // tn
    return pl.pallas_call(
        _merge_kernel,
        grid=(d // tn, t // tm),
        in_specs=[pl.BlockSpec((tm, wa_in), lambda j, i: (i, 0)),
                  pl.BlockSpec((tm, wr_in), lambda j, i: (i, 0)),
                  pl.BlockSpec((tm, tn), lambda j, i: (i, ga_blk + j)),
                  pl.BlockSpec((tm, tn), lambda j, i: (i, gr_blk + j)),
                  pl.BlockSpec((wa_in, tn), lambda j, i: (0, j)),
                  pl.BlockSpec((wr_in, tn), lambda j, i: (0, j))],
        out_specs=pl.BlockSpec((tm, tn), lambda j, i: (i, j)),
        out_shape=jax.ShapeDtypeStruct((t, d), BF16),
        scratch_shapes=[pltpu.VMEM((wa_in, tn), BF16), pltpu.VMEM((wr_in, tn), BF16)],
        compiler_params=_params("parallel", "arbitrary"),
        name="merge",
    )(o_a, y_r, proj, proj, w_a, w_r)


def _outproj_kernel(x_ref, mix_ref, wo_ref, nw_ref, wr_ref, br_ref, x1_ref, h_ref, route_ref):
    tm = x_ref.shape[0]
    sub = tm // OUTPROJ_SUB
    blocks = [pl.ds(k * sub, sub) for k in range(OUTPROJ_SUB)]
    proj = [jnp.dot(mix_ref[rows, :], wo_ref[...], preferred_element_type=F32) for rows in blocks]
    parts = []
    for rows, pr in zip(blocks, proj):
        x1 = x_ref[rows, :] + pr
        x1_ref[rows, :] = x1
        ms = jnp.mean(x1 * x1, axis=-1, keepdims=True)
        h = (x1 * lax.rsqrt(ms + RMS_EPS)) * nw_ref[...]
        h_ref[rows, :] = h
        h_hi = h.astype(BF16)
        h_lo = (h - h_hi.astype(F32)).astype(BF16)
        r = (jnp.dot(h_hi, wr_ref[...], preferred_element_type=F32)
             + jnp.dot(h_lo, wr_ref[...], preferred_element_type=F32))
        parts.append((r[:, :V7X_LANES] + r[:, V7X_LANES:]).T[:ROUTER_ROWS, :])
    logits = jnp.concatenate(parts, axis=1) + br_ref[...]
    row = lax.broadcasted_iota(jnp.int32, logits.shape, 0)
    ninf = -jnp.inf
    big = ROUTER_ROWS
    gl = jnp.where(row < N_GROUPS, logits, ninf)
    gmax = jnp.max(gl, axis=0, keepdims=True)
    gsum = jnp.sum(jnp.exp(gl - gmax), axis=0, keepdims=True)
    gidx = jnp.min(jnp.where(gl == gmax, row, big), axis=0, keepdims=True)
    g_weight = 1.0 / gsum
    lo = ROUTER_EXPERT_ROW + EXPERTS_PER_GROUP * gidx
    in_group = (row >= lo) & (row < lo + EXPERTS_PER_GROUP)
    el = jnp.where(in_group, logits, ninf)
    emax = jnp.max(el, axis=0, keepdims=True)
    i1 = jnp.min(jnp.where(el == emax, row, big), axis=0, keepdims=True)
    el2 = jnp.where(row == i1, ninf, el)
    emax2 = jnp.max(el2, axis=0, keepdims=True)
    i2 = jnp.min(jnp.where(el2 == emax2, row, big), axis=0, keepdims=True)
    esum = jnp.sum(jnp.exp(el - emax), axis=0, keepdims=True)
    p1 = 1.0 / esum
    p2 = jnp.exp(emax2 - emax) / esum
    c1 = g_weight * (p1 / (p1 + p2))
    c2 = g_weight * (p2 / (p1 + p2))
    e1 = (i1 - ROUTER_EXPERT_ROW).astype(F32)
    e2 = (i2 - ROUTER_EXPERT_ROW).astype(F32)
    orow = lax.broadcasted_iota(jnp.int32, route_ref.shape, 0)
    route_ref[...] = jnp.where(orow == 0, c1, jnp.where(orow == 1, c2, jnp.where(orow == 2, e1,
                               jnp.where(orow == 3, e2, 0.0))))


def _outproj(x2, mix, w_out, norm_w, w_router_t, b_router_t):
    t, d = x2.shape
    tm = min(OUTPROJ_TM, t)
    row = lambda i: (i, 0)
    const = lambda i: (0, 0)
    return pl.pallas_call(
        _outproj_kernel,
        grid=(t // tm,),
        in_specs=[pl.BlockSpec((tm, d), row),
                  pl.BlockSpec((tm, d), row),
                  pl.BlockSpec((d, d), const),
                  pl.BlockSpec((1, d), const),
                  pl.BlockSpec((d, 2 * V7X_LANES), const),
                  pl.BlockSpec((ROUTER_ROWS, 1), const)],
        out_specs=[pl.BlockSpec((tm, d), row),
                   pl.BlockSpec((tm, d), row),
                   pl.BlockSpec((ROUTE_OUT_ROWS, tm), lambda i: (0, i))],
        out_shape=[jax.ShapeDtypeStruct((t, d), F32),
                   jax.ShapeDtypeStruct((t, d), F32),
                   jax.ShapeDtypeStruct((ROUTE_OUT_ROWS, t), F32)],
        compiler_params=_params("parallel"),
        name="outproj",
    )(x2, mix, w_out, norm_w.reshape(1, d), w_router_t, b_router_t)


def _router_params(w_rg, b_rg, w_re, b_re):
    d = w_rg.shape[0]
    gap = ROUTER_EXPERT_ROW - N_GROUPS
    w = jnp.concatenate([w_rg, jnp.zeros((d, gap), F32),
                         jnp.transpose(w_re, (1, 0, 2)).reshape(d, N_EXPERTS),
                         jnp.zeros((d, V7X_LANES - ROUTER_ROWS), F32)], axis=1)
    w_hi = w.astype(BF16)
    w_lo = (w - w_hi.astype(F32)).astype(BF16)
    b_t = jnp.concatenate([b_rg, jnp.zeros((gap,), F32), b_re.reshape(N_EXPERTS)])
    return jnp.concatenate([w_hi, w_lo], axis=1), b_t.reshape(ROUTER_ROWS, 1)


def _meta(meta_ref, row, col=0):
    return meta_ref[row * PLAN_LANES + col]


def _dispatch_kernel(pos_ref, h_ref, zeros_hbm, x_hbm, hbuf, sem):
    del zeros_hbm
    i = pl.program_id(0)
    tt = h_ref.shape[0]
    n_tok = tt * pl.num_programs(0)
    slot = lax.rem(i, 2)

    def drain(s):
        for k in range(2):
            pltpu.make_async_copy(hbuf.at[s], x_hbm.at[pl.ds(0, tt), :], sem.at[s]).wait()

    @pl.when(i >= 2)
    def _():
        drain(slot)

    hbuf[slot] = h_ref[...]
    base = i * tt

    def start(r, c):
        for k in range(2):
            p = pos_ref[k * n_tok + base + r]
            pltpu.make_async_copy(hbuf.at[slot, pl.ds(r, 1), :], x_hbm.at[pl.ds(p, 1), :],
                                  sem.at[slot]).start()
        return c

    lax.fori_loop(0, tt, start, 0, unroll=True)

    @pl.when(i == pl.num_programs(0) - 1)
    def _():
        @pl.when(i >= 1)
        def _():
            drain(1 - slot)
        drain(slot)


def _dispatch(h, pos, zeros):
    t, d = h.shape
    tt = min(DISPATCH_TT, t)
    grid_spec = pltpu.PrefetchScalarGridSpec(
        num_scalar_prefetch=1,
        grid=(t // tt,),
        in_specs=[pl.BlockSpec((tt, d), lambda i, pos: (i, 0)), pl.BlockSpec(memory_space=pl.ANY)],
        out_specs=pl.BlockSpec(memory_space=pl.ANY),
        scratch_shapes=[pltpu.VMEM((2, tt, d), F32), pltpu.SemaphoreType.DMA((2,))],
    )
    return pl.pallas_call(
        _dispatch_kernel,
        grid_spec=grid_spec,
        out_shape=jax.ShapeDtypeStruct(zeros.shape, zeros.dtype),
        input_output_aliases={2: 0},
        compiler_params=_params("arbitrary"),
        name="dispatch",
    )(pos, h, zeros)


def _experts_kernel(meta_ref, x_ref, wg_hbm, wu_hbm, wd_hbm, y_ref, wgb, wub, wdb, wsem):
    i = pl.program_id(0)
    n_tiles = _meta(meta_ref, META_N_TILES)
    n_act = _meta(meta_ref, META_N_ACTIVE)

    def weight_copies(q):
        e, s = _meta(meta_ref, META_ACTIVE, q), lax.rem(q, 2)
        return (pltpu.make_async_copy(wg_hbm.at[e], wgb.at[s], wsem.at[s, 0]),
                pltpu.make_async_copy(wu_hbm.at[e], wub.at[s], wsem.at[s, 1]),
                pltpu.make_async_copy(wd_hbm.at[e], wdb.at[s], wsem.at[s, 2]))

    @pl.when(i == 0)
    def _():
        for c in weight_copies(0):
            c.start()

    @pl.when(i < n_tiles)
    def _():
        q = _meta(meta_ref, META_TILE_SEQ, i)
        s = lax.rem(q, 2)

        @pl.when((i == 0) | (q != _meta(meta_ref, META_TILE_SEQ, jnp.maximum(i - 1, 0))))
        def _():
            for c in weight_copies(q):
                c.wait()

            @pl.when(q + 1 < n_act)
            def _():
                for c in weight_copies(q + 1):
                    c.start()

        x = x_ref[...].astype(BF16)
        gate = jnp.dot(x, wgb[s], preferred_element_type=F32)
        up = jnp.dot(x, wub[s], preferred_element_type=F32)
        act = (gate * jax.nn.sigmoid(gate)) * up
        y_ref[...] = jnp.dot(act.astype(BF16), wdb[s], preferred_element_type=F32)

    @pl.when(i >= n_tiles)
    def _():
        y_ref[...] = jnp.zeros_like(y_ref)


def _experts(x_sorted, meta, w_gate, w_up, w_down):
    d = x_sorted.shape[1]
    f = w_gate.shape[2]
    tm = MOE_TM
    max_tiles = x_sorted.shape[0] // tm
    any_spec = pl.BlockSpec(memory_space=pl.ANY)
    grid_spec = pltpu.PrefetchScalarGridSpec(
        num_scalar_prefetch=1,
        grid=(max_tiles,),
        in_specs=[pl.BlockSpec((tm, d), lambda i, meta: (jnp.minimum(i, _meta(meta, META_N_TILES) - 1), 0)),
                  any_spec, any_spec, any_spec],
        out_specs=pl.BlockSpec((tm, d), lambda i, meta: (i, 0)),
        scratch_shapes=[pltpu.VMEM((2, d, f), BF16), pltpu.VMEM((2, d, f), BF16), pltpu.VMEM((2, f, d), BF16),
                        pltpu.SemaphoreType.DMA((2, 3))],
    )
    return pl.pallas_call(
        _experts_kernel,
        grid_spec=grid_spec,
        out_shape=jax.ShapeDtypeStruct((max_tiles * tm, d), F32),
        compiler_params=_params("arbitrary"),
        name="experts",
    )(meta, x_sorted, w_gate, w_up, w_down)


def _combine_kernel(pos_ref, x_ref, cw_ref, y_hbm, nw_ref, o_ref, ybuf, sem, *, final_norm):
    i = pl.program_id(0)
    tm = x_ref.shape[0]
    n_tok = tm * pl.num_programs(0)
    slot = lax.rem(i, 2)

    def start_gather(tile, dst_slot, unroll):
        base = tile * tm

        def start(r, c):
            for k in range(2):
                p = pos_ref[k * n_tok + base + r]
                pltpu.make_async_copy(y_hbm.at[pl.ds(p, 1), :], ybuf.at[dst_slot, k, pl.ds(r, 1), :],
                                      sem.at[dst_slot]).start()
            return c

        lax.fori_loop(0, tm, start, 0, unroll=unroll)

    @pl.when(i == 0)
    def _():
        start_gather(0, 0, unroll=8)

    @pl.when(i + 1 < pl.num_programs(0))
    def _():
        start_gather(i + 1, 1 - slot, unroll=True)

    for k in range(2):
        pltpu.make_async_copy(y_hbm.at[pl.ds(0, tm), :], ybuf.at[slot, k], sem.at[slot]).wait()
    cw = cw_ref[...]
    x = x_ref[...] + (cw[:, 0:1] * ybuf[slot, 0] + cw[:, 1:2] * ybuf[slot, 1])
    if final_norm:
        ms = jnp.mean(x * x, axis=-1, keepdims=True)
        x = (x * lax.rsqrt(ms + RMS_EPS)) * nw_ref[...]
    o_ref[...] = x


def _combine(x1, cw, y_sorted, pos, norm_w, final_norm):
    t, d = x1.shape
    tm = min(COMBINE_TM, t)
    grid_spec = pltpu.PrefetchScalarGridSpec(
        num_scalar_prefetch=1,
        grid=(t // tm,),
        in_specs=[pl.BlockSpec((tm, d), lambda i, pos: (i, 0)),
                  pl.BlockSpec((tm, 2), lambda i, pos: (i, 0)),
                  pl.BlockSpec(memory_space=pl.ANY),
                  pl.BlockSpec((1, d), lambda i, pos: (0, 0))],
        out_specs=pl.BlockSpec((tm, d), lambda i, pos: (i, 0)),
        scratch_shapes=[pltpu.VMEM((2, 2, tm, d), F32), pltpu.SemaphoreType.DMA((2,))],
    )
    return pl.pallas_call(
        functools.partial(_combine_kernel, final_norm=final_norm),
        grid_spec=grid_spec,
        out_shape=jax.ShapeDtypeStruct((t, d), F32),
        compiler_params=_params("arbitrary"),
        name="combine",
    )(pos, x1, cw, y_sorted, norm_w.reshape(1, d))


def _rope_tables(s):
    half = HEAD_DIM_RET_QK // 2
    inv = ROPE_BASE ** (-np.arange(half, dtype=np.float64) / half)
    ang = np.arange(s, dtype=np.float64)[:, None] * inv[None, :]
    cos, sin = np.cos(ang), np.sin(ang)
    return (jnp.asarray(np.concatenate([cos, cos], axis=-1), F32),
            jnp.asarray(np.concatenate([-sin, sin], axis=-1), F32))


META_TILE_SEQ, META_ACTIVE, META_N_ACTIVE, META_N_TILES = 0, 1, 2, 3
META_ROWS = 8
PLAN_LANES = V7X_LANES


def _plan_kernel(route_ref, pfx_ref, tri_ref, pos_ref, meta_ref, *, tm):
    ne, lanes = N_EXPERTS, PLAN_LANES
    t = route_ref.shape[1]
    nblk = t // lanes
    ids = lax.broadcasted_iota(jnp.int32, (ne, lanes), 0).astype(F32)

    def onehots(blk):
        cols = slice(blk * lanes, (blk + 1) * lanes)
        return [(ids == route_ref[2 + k:3 + k, cols]).astype(F32) for k in range(2)]

    stacked = jnp.concatenate([sum(onehots(blk)) for blk in range(nblk)], axis=0)
    prod = jnp.dot(stacked.astype(BF16), pfx_ref[...], preferred_element_type=F32)
    carry = jnp.zeros((ne, lanes), F32)
    before = []
    for blk in range(nblk):
        rows = slice(blk * ne, (blk + 1) * ne)
        before.append(prod[rows, :lanes] + carry)
        carry = carry + prod[rows, lanes:]
    counts = carry
    tiles_per = jnp.floor((counts + (tm - 1.0)) * (1.0 / tm))
    tri = tri_ref[...]
    tile_end = jnp.dot(tri, tiles_per.astype(BF16), preferred_element_type=F32)
    row_start = (tile_end - tiles_per) * tm
    for k in range(2):
        pieces = []
        for blk in range(nblk):
            oh = onehots(blk)[k]
            pieces.append(jnp.sum(oh * (before[blk] + row_start), axis=0, keepdims=True))
        pos_ref[k:k + 1, :] = jnp.concatenate(pieces, axis=1).astype(jnp.int32)

    owns = (tiles_per > 0.0).astype(F32)
    seq = jnp.dot(tri, owns.astype(BF16), preferred_element_type=F32) - 1.0
    n_active = seq[ne - 1:ne, :] + 1.0
    n_tiles = tile_end[ne - 1:ne, :]
    lane = lax.broadcasted_iota(jnp.int32, (ne, lanes), 1).astype(F32)
    tile = jnp.minimum(lane, n_tiles - 1.0)
    tile_seq = jnp.sum(owns * (tile_end <= tile).astype(F32), axis=0, keepdims=True)
    active = jnp.sum(jnp.where((owns > 0.0) & (seq == lane), ids, 0.0), axis=0, keepdims=True)
    zero = jnp.zeros((1, lanes), F32)
    rows = {META_TILE_SEQ: tile_seq, META_ACTIVE: active, META_N_ACTIVE: n_active, META_N_TILES: n_tiles}
    meta_ref[...] = jnp.concatenate([rows.get(r, zero) for r in range(META_ROWS)], axis=0).astype(jnp.int32)


def _plan(route_t, tm, max_tiles):
    t = route_t.shape[1]
    assert max_tiles <= PLAN_LANES and N_EXPERTS <= PLAN_LANES and t % PLAN_LANES == 0
    r = jnp.arange(PLAN_LANES)
    prefix = jnp.concatenate([(r[:, None] < r[None, :]).astype(BF16),
                              jnp.ones((PLAN_LANES, PLAN_LANES), BF16)], axis=1)
    e = jnp.arange(N_EXPERTS)
    tri = (e[None, :] <= e[:, None]).astype(BF16)
    pos, meta = pl.pallas_call(
        functools.partial(_plan_kernel, tm=tm),
        out_shape=[jax.ShapeDtypeStruct((2, t), jnp.int32),
                   jax.ShapeDtypeStruct((META_ROWS, PLAN_LANES), jnp.int32)],
        compiler_params=pltpu.CompilerParams(vmem_limit_bytes=V7X_VMEM_LIMIT_BYTES),
        name="plan",
    )(route_t, prefix, tri)
    return pos.reshape(2 * t), meta.reshape(META_ROWS * PLAN_LANES)


def kernel(x, norm_mix_w, w_in, ret_gn_w, w_branch_moba, w_branch_ret, w_out, norm_ffn_w, w_router_group, b_router_group, w_router_expert, b_router_expert, w_expert_gate, w_expert_up, w_expert_down, norm_final_w):
    b, s, d = x.shape
    t = b * s
    depth = w_in.shape[0]
    assert s % MOBA_BLOCK == 0 and s % RET_CHUNK == 0 and t % MOE_TM == 0
    cos_t, sin_t = _rope_tables(s)
    log_gamma = jnp.log(1.0 - 2.0 ** (-5.0 - jnp.arange(N_HEADS_RET, dtype=F32)))
    x2 = x.reshape(t, d)
    for l in range(depth):
        max_tiles = 2 * t // MOE_TM + N_EXPERTS
        proj, zeros, expert_w = _inproj(x2, norm_mix_w[l], w_in[l], max_tiles * MOE_TM,
                                        (w_expert_gate[l], w_expert_up[l], w_expert_down[l]))
        proj3 = proj.reshape(b, s, proj.shape[1])
        o_a = _moba(proj3).reshape(t, -1)
        y_r = _retention(proj3, cos_t, sin_t, ret_gn_w[l], log_gamma).reshape(t, -1)
        mix = _merge(o_a, y_r, proj, w_branch_moba[l], w_branch_ret[l])
        w_router_t, b_router_t = _router_params(w_router_group[l], b_router_group[l],
                                                w_router_expert[l], b_router_expert[l])
        x1, h, route_t = _outproj(x2, mix, w_out[l].astype(BF16), norm_ffn_w[l], w_router_t, b_router_t)
        pos, meta = _plan(route_t, MOE_TM, max_tiles)
        x_sorted = _dispatch(h, pos, zeros)
        y_sorted = _experts(x_sorted, meta, *expert_w)
        x2 = _combine(x1, route_t[0:2].T, y_sorted, pos, norm_final_w, final_norm=(l == depth - 1))
    return x2.reshape(b, s, d)
```

```python
import functools

import jax
import jax.numpy as jnp
import numpy as np
from jax import lax
from jax.experimental import pallas as pl
from jax.experimental.pallas import tpu as pltpu

F32 = jnp.float32
BF16 = jnp.bfloat16

V7X_LANES = 128
V7X_VMEM_LIMIT_BYTES = 56 * 1024 * 1024

N_HEADS_MOBA = 8
HEAD_DIM_MOBA = 128
MOBA_BLOCK = 256
MOBA_TOPK = 3
N_HEADS_RET = 8
HEAD_DIM_RET_QK = 128
HEAD_DIM_RET_V = 256
ROPE_BASE = 10000.0
N_GROUPS = 4
EXPERTS_PER_GROUP = 8
N_EXPERTS = N_GROUPS * EXPERTS_PER_GROUP
RMS_EPS = 1e-6
GN_EPS = 1e-6
NEG_BIG = -1e30

ROUTER_EXPERT_ROW = 8
ROUTER_ROWS = ROUTER_EXPERT_ROW + N_EXPERTS
ROUTE_OUT_ROWS = 8
MOBA_HEADS_PER_STEP = 4
MOBA_BIAS_ROWS = 16
MOBA_SUM_ROWS = 16
RET_CHUNK = 256
RET_HEADS_PER_STEP = 2
RET_UNROLL = 4
INPROJ_TM, INPROJ_TN = 1024, 1024
INPROJ_MSUB = 2
MERGE_TM, MERGE_TN = 512, 1024
OUTPROJ_TM = 512
OUTPROJ_SUB = 4
MOE_TM = 256
DISPATCH_TT = 256
COMBINE_TM = 256


def _params(*semantics):
    return pltpu.CompilerParams(dimension_semantics=semantics,
                                vmem_limit_bytes=V7X_VMEM_LIMIT_BYTES)


def _inproj_kernel(x_ref, nw_ref, w_ref, o_ref, z_ref, h_ref):
    j, m = pl.program_id(1), pl.program_id(2)

    def project(h):
        z_ref[...] = jnp.zeros_like(z_ref)
        o_ref[...] = jnp.dot(h, w_ref[...].astype(BF16), preferred_element_type=F32).astype(o_ref.dtype)

    @pl.when(j == 0)
    def _():
        x = x_ref[...]
        ms = jnp.mean(x * x, axis=-1, keepdims=True)
        h = ((x * lax.rsqrt(ms + RMS_EPS)) * nw_ref[...]).astype(BF16)
        h_ref[m] = h
        project(h)

    @pl.when(j != 0)
    def _():
        project(h_ref[m])


def _inproj(x2, norm_w, w_in, zero_rows):
    t, d = x2.shape
    n = w_in.shape[1]
    tm, tn = min(INPROJ_TM, t), INPROJ_TN
    msub = min(INPROJ_MSUB, t // tm)
    nj = n // tn
    n_steps = (t // (tm * msub)) * nj * msub
    zb = pl.cdiv(pl.cdiv(zero_rows, n_steps), 8) * 8
    n_zblocks = pl.cdiv(zero_rows, zb)

    def x_map(p, j, m):
        return (jnp.where(j == 0, p * msub + m, p * msub + msub - 1), 0)

    def z_map(p, j, m):
        return (jnp.minimum((p * nj + j) * msub + m, n_zblocks - 1), 0)

    return pl.pallas_call(
        _inproj_kernel,
        grid=(t // (tm * msub), nj, msub),
        in_specs=[pl.BlockSpec((tm, d), x_map),
                  pl.BlockSpec((1, d), lambda p, j, m: (0, 0)),
                  pl.BlockSpec((d, tn), lambda p, j, m: (0, j))],
        out_specs=[pl.BlockSpec((tm, tn), lambda p, j, m: (p * msub + m, j)),
                   pl.BlockSpec((zb, d), z_map)],
        out_shape=[jax.ShapeDtypeStruct((t, n), BF16), jax.ShapeDtypeStruct((zero_rows, d), F32)],
        scratch_shapes=[pltpu.VMEM((msub, tm, d), BF16)],
        compiler_params=_params("arbitrary", "arbitrary", "arbitrary"),
        name="inproj",
    )(x2, norm_w.reshape(1, d), w_in)


def _moba_kernel(q_ref, k_ref, v_ref, o_ref, qta_ref, ka_ref, vta_ref, km_ref, acc_ref, sc_ref, *, nb, hb):
    blk, hd = MOBA_BLOCK, HEAD_DIM_MOBA
    i = pl.program_id(2)

    @pl.when(i == 0)
    def _():
        lane = lax.broadcasted_iota(jnp.int32, (blk, hd), 1)
        srow = lax.broadcasted_iota(jnp.int32, (MOBA_SUM_ROWS, blk), 0)
        for hh in range(hb):
            cols = slice(hh * hd, (hh + 1) * hd)
            means = []
            for c in range(nb):
                rows = slice(c * blk, (c + 1) * blk)
                qta_ref[hh, 0:hd, rows] = q_ref[rows, cols].astype(F32).T.astype(BF16)
                qta_ref[hh, hd:2 * hd, rows] = jnp.zeros((hd, blk), BF16)
                ka_ref[hh, rows, 0:hd] = k_ref[rows, cols]
                ka_ref[hh, rows, hd:2 * hd] = (lane == c).astype(BF16)
                vta_ref[hh, 0:hd, rows] = v_ref[rows, cols].astype(F32).T.astype(BF16)
                vta_ref[hh, hd:hd + MOBA_SUM_ROWS, rows] = (srow == 0).astype(BF16)
                means.append(jnp.sum(k_ref[rows, cols].astype(F32), axis=0, keepdims=True) * (1.0 / blk))
            means += [jnp.zeros((1, hd), F32)] * (MOBA_BIAS_ROWS - nb)
            rest = jnp.concatenate(means, axis=0)
            for part in range(3):
                term = rest.astype(BF16)
                km_ref[hh, part * MOBA_BIAS_ROWS:(part + 1) * MOBA_BIAS_ROWS, :] = term
                rest = rest - term.astype(F32)

    q0 = pl.multiple_of(i * blk, blk)
    qcols = pl.ds(q0, blk)
    c_exp = (hd ** -0.5) * 1.4426950408889634
    row = lax.broadcasted_iota(jnp.int32, (MOBA_BIAS_ROWS, blk), 0)
    kpos = lax.broadcasted_iota(jnp.int32, (blk, blk), 0)
    qpos = lax.broadcasted_iota(jnp.int32, (blk, blk), 1)

    qts = [qta_ref[hh, 0:hd, qcols] for hh in range(hb)]
    gates = [jnp.dot(km_ref[hh], qts[hh], preferred_element_type=F32) for hh in range(hb)]
    own = [jnp.dot(ka_ref[hh, qcols, 0:hd], qts[hh], preferred_element_type=F32)
           for hh in range(hb)]
    for hh in range(hb):
        g3 = gates[hh]
        gate = (g3[0:MOBA_BIAS_ROWS] + g3[MOBA_BIAS_ROWS:2 * MOBA_BIAS_ROWS]) + g3[2 * MOBA_BIAS_ROWS:]
        gate = jnp.where(row < i, gate, -jnp.inf)
        bias = jnp.full((MOBA_BIAS_ROWS, blk), NEG_BIG, F32)
        for _ in range(MOBA_TOPK):
            top = jnp.max(gate, axis=0, keepdims=True)
            is_top = (gate == top) & (top > -jnp.inf)
            first = jnp.min(jnp.where(is_top, row, MOBA_BIAS_ROWS), axis=0, keepdims=True)
            pick = row == first
            bias = jnp.where(pick, 0.0, bias)
            gate = jnp.where(pick, -jnp.inf, gate)
        qta_ref[hh, hd:hd + MOBA_BIAS_ROWS, qcols] = bias.astype(BF16)

    n_trips = (i + 1) // 2

    def key_rows(t):
        return pl.ds(pl.multiple_of(t * (2 * blk), 2 * blk), 2 * blk)

    def score_dots(t):
        return [jnp.dot(ka_ref[hh, key_rows(t), :], qta_ref[hh, :, qcols], preferred_element_type=F32)
                for hh in range(hb)]

    def scores_to(t, slot):
        for hh, s in enumerate(score_dots(t)):
            sc_ref[slot, hh] = s

    scores_to(0, 0)

    init, probs = [], []
    for hh in range(hb):
        s = jnp.where(kpos <= qpos, own[hh] * c_exp, NEG_BIG)
        m0 = jnp.max(s, axis=0, keepdims=True)
        init.append(m0)
        probs.append(jnp.exp2(s - m0).astype(BF16))
    for hh in range(hb):
        acc_ref[hh] = jnp.dot(vta_ref[hh, :, qcols], probs[hh], preferred_element_type=F32)

    def fold(t, slot, ms):
        probs, out = [], []
        for hh in range(hb):
            s = sc_ref[slot, hh] * c_exp
            m_new = jnp.maximum(ms[hh], jnp.max(s, axis=0, keepdims=True))
            probs.append((jnp.exp2(ms[hh] - m_new), jnp.exp2(s - m_new).astype(BF16)))
            out.append(m_new)
        for hh in range(hb):
            alpha, p = probs[hh]
            acc_ref[hh] = alpha * acc_ref[hh] + jnp.dot(vta_ref[hh, :, key_rows(t)], p,
                                                        preferred_element_type=F32)
        return tuple(out)

    def two_trips(u, ms):
        t = 2 * u
        scores_to(t + 1, 1)
        ms = fold(t, 0, ms)
        scores_to(t + 2, 0)
        return fold(t + 1, 1, ms)

    n_loop = jnp.maximum(n_trips - 1, 0) // 2
    ms = lax.fori_loop(0, n_loop, two_trips, tuple(init))
    t_tail = 2 * n_loop

    @pl.when(n_trips - t_tail == 2)
    def _():
        scores_to(t_tail + 1, 1)
        fold(t_tail + 1, 1, fold(t_tail, 0, ms))

    @pl.when(n_trips - t_tail == 1)
    def _():
        fold(t_tail, 0, ms)

    for hh in range(hb):
        acc = acc_ref[hh]
        o_ref[:, hh * hd:(hh + 1) * hd] = (acc[0:hd, :] / acc[hd:hd + 1, :]).T.astype(o_ref.dtype)


def _moba(proj3):
    b, s, _ = proj3.shape
    nb = s // MOBA_BLOCK
    assert nb % 2 == 0 and nb <= MOBA_BIAS_ROWS
    hd, nh, hb = HEAD_DIM_MOBA, N_HEADS_MOBA, MOBA_HEADS_PER_STEP
    ng = nh // hb
    return pl.pallas_call(
        functools.partial(_moba_kernel, nb=nb, hb=hb),
        grid=(b, ng, nb),
        in_specs=[pl.BlockSpec((None, s, hb * hd), lambda bi, g, i: (bi, 0, g)),
                  pl.BlockSpec((None, s, hb * hd), lambda bi, g, i: (bi, 0, ng + g)),
                  pl.BlockSpec((None, s, hb * hd), lambda bi, g, i: (bi, 0, 2 * ng + g))],
        out_specs=pl.BlockSpec((None, MOBA_BLOCK, hb * hd), lambda bi, g, i: (bi, i, g)),
        out_shape=jax.ShapeDtypeStruct((b, s, nh * hd), BF16),
        scratch_shapes=[pltpu.VMEM((hb, 2 * hd, s), BF16), pltpu.VMEM((hb, s, 2 * hd), BF16),
                        pltpu.VMEM((hb, hd + MOBA_SUM_ROWS, s), BF16),
                        pltpu.VMEM((hb, 3 * MOBA_BIAS_ROWS, hd), BF16),
                        pltpu.VMEM((hb, hd + MOBA_SUM_ROWS, MOBA_BLOCK), F32),
                        pltpu.VMEM((2, hb, 2 * MOBA_BLOCK, MOBA_BLOCK), F32)],
        compiler_params=_params("parallel", "parallel", "arbitrary"),
        name="moba",
    )(proj3, proj3, proj3)


def _retention_kernel(lg_ref, q_ref, k_ref, v_ref, g_ref, cos_ref, sin_ref, gnw_ref, o_ref,
                      decay_ref, xi_ref, zeta_ref, st_ref, *, n_chunks, hb):
    c_len = RET_CHUNK
    dk, dv = HEAD_DIM_RET_QK, HEAD_DIM_RET_V
    ri = lax.broadcasted_iota(jnp.int32, (c_len, c_len), 0).astype(F32)
    ci = lax.broadcasted_iota(jnp.int32, (c_len, c_len), 1).astype(F32)
    diff = ri - ci
    idx = lax.broadcasted_iota(jnp.int32, (c_len, dk), 0).astype(F32)
    chunk_decay = []
    for hh in range(hb):
        lg = lg_ref[pl.program_id(1) * hb + hh]
        decay_ref[hh] = jnp.where(diff >= 0, jnp.exp(lg * jnp.maximum(diff, 0.0)), 0.0)
        xi_ref[hh] = jnp.exp(lg * (idx + 1.0))
        zeta_ref[hh] = jnp.exp(lg * (c_len - 1.0 - idx))
        chunk_decay.append(jnp.exp(jnp.zeros((1, dv), F32) + lg * c_len))
        st_ref[hh] = jnp.zeros((dk, dv), F32)
    k_scale = dk ** -0.5
    nt = (((1,), (1,)), ((), ()))

    def chunk(c, carry):
        r0 = pl.multiple_of(c * c_len, c_len)
        rows = pl.ds(r0, c_len)
        cs, sn = cos_ref[rows, :], sin_ref[rows, :]
        qbs, kbs, qxs, kzs = [], [], [], []
        for hh in range(hb):
            q = q_ref[rows, hh * dk:(hh + 1) * dk].astype(F32)
            k = k_ref[rows, hh * dk:(hh + 1) * dk].astype(F32)
            qr = q * cs + pltpu.roll(q, dk // 2, 1) * sn
            kr = (k * cs + pltpu.roll(k, dk // 2, 1) * sn) * k_scale
            qbs.append(qr.astype(BF16))
            kbs.append(kr.astype(BF16))
            qxs.append((qr * xi_ref[hh]).astype(BF16))
            kzs.append((kr * zeta_ref[hh]).T.astype(BF16))
        vs = [v_ref[rows, hh * dv:(hh + 1) * dv] for hh in range(hb)]
        scores = [lax.dot_general(qbs[hh], kbs[hh], nt, preferred_element_type=F32) for hh in range(hb)]
        cross = [jnp.dot(qxs[hh], st_ref[hh].astype(BF16), preferred_element_type=F32) for hh in range(hb)]
        upd = [jnp.dot(kzs[hh], vs[hh], preferred_element_type=F32) for hh in range(hb)]
        sd = [(scores[hh] * decay_ref[hh]).astype(BF16) for hh in range(hb)]
        inner = [jnp.dot(sd[hh], vs[hh], preferred_element_type=F32) for hh in range(hb)]
        for hh in range(hb):
            st_ref[hh] = chunk_decay[hh] * st_ref[hh] + upd[hh]
            o = inner[hh] + cross[hh]
            mu = jnp.mean(o, axis=-1, keepdims=True)
            var = jnp.mean(jnp.square(o - mu), axis=-1, keepdims=True)
            yn = (o - mu) * lax.rsqrt(var + GN_EPS)
            cols = slice(hh * dv, (hh + 1) * dv)
            g = g_ref[rows, cols].astype(F32)
            half_g = 0.5 * g
            y = (half_g + half_g * jnp.tanh(half_g)) * (yn * gnw_ref[:, cols])
            o_ref[rows, cols] = y.astype(o_ref.dtype)
        return carry

    lax.fori_loop(0, n_chunks, chunk, 0, unroll=RET_UNROLL)


def _retention(proj3, cos_t, sin_t, gn_w, log_gamma):
    b, s, _ = proj3.shape
    nh, dk, dv, hb = N_HEADS_RET, HEAD_DIM_RET_QK, HEAD_DIM_RET_V, RET_HEADS_PER_STEP
    ng = nh // hb
    q_blk = 3 * N_HEADS_MOBA * HEAD_DIM_MOBA // (hb * dk)
    k_blk = q_blk + ng
    v_blk = (k_blk + ng) * dk // dv
    g_blk = v_blk + ng
    grid_spec = pltpu.PrefetchScalarGridSpec(
        num_scalar_prefetch=1,
        grid=(b, ng),
        in_specs=[pl.BlockSpec((None, s, hb * dk), lambda bi, g, lg: (bi, 0, q_blk + g)),
                  pl.BlockSpec((None, s, hb * dk), lambda bi, g, lg: (bi, 0, k_blk + g)),
                  pl.BlockSpec((None, s, hb * dv), lambda bi, g, lg: (bi, 0, v_blk + g)),
                  pl.BlockSpec((None, s, hb * dv), lambda bi, g, lg: (bi, 0, g_blk + g)),
                  pl.BlockSpec((s, dk), lambda bi, g, lg: (0, 0)),
                  pl.BlockSpec((s, dk), lambda bi, g, lg: (0, 0)),
                  pl.BlockSpec((1, hb * dv), lambda bi, g, lg: (0, g))],
        out_specs=pl.BlockSpec((None, s, hb * dv), lambda bi, g, lg: (bi, 0, g)),
        scratch_shapes=[pltpu.VMEM((hb, RET_CHUNK, RET_CHUNK), F32), pltpu.VMEM((hb, RET_CHUNK, dk), F32),
                        pltpu.VMEM((hb, RET_CHUNK, dk), F32), pltpu.VMEM((hb, dk, dv), F32)],
    )
    return pl.pallas_call(
        functools.partial(_retention_kernel, n_chunks=s // RET_CHUNK, hb=hb),
        grid_spec=grid_spec,
        out_shape=jax.ShapeDtypeStruct((b, s, nh * dv), BF16),
        compiler_params=_params("parallel", "parallel"),
        name="retention",
    )(log_gamma, proj3, proj3, proj3, proj3, cos_t, sin_t, gn_w.reshape(1, nh * dv))


def _merge_kernel(oa_ref, yr_ref, ga_ref, gr_ref, wa_ref, wr_ref, o_ref, wab_ref, wrb_ref):
    @pl.when(pl.program_id(1) == 0)
    def _():
        wab_ref[...] = wa_ref[...].astype(BF16)
        wrb_ref[...] = wr_ref[...].astype(BF16)

    a = jnp.dot(oa_ref[...], wab_ref[...], preferred_element_type=F32)
    r = jnp.dot(yr_ref[...], wrb_ref[...], preferred_element_type=F32)
    mix = jax.nn.sigmoid(ga_ref[...].astype(F32)) * a + jax.nn.sigmoid(gr_ref[...].astype(F32)) * r
    o_ref[...] = mix.astype(o_ref.dtype)


def _merge(o_a, y_r, proj, w_a, w_r):
    t, wa_in = o_a.shape
    wr_in = y_r.shape[1]
    d = w_a.shape[1]
    tm, tn = min(MERGE_TM, t), MERGE_TN
    ga_blk = (proj.shape[1] - 2 * d) // tn
    gr_blk = ga_blk + d // tn
    return pl.pallas_call(
        _merge_kernel,
        grid=(d // tn, t // tm),
        in_specs=[pl.BlockSpec((tm, wa_in), lambda j, i: (i, 0)),
                  pl.BlockSpec((tm, wr_in), lambda j, i: (i, 0)),
                  pl.BlockSpec((tm, tn), lambda j, i: (i, ga_blk + j)),
                  pl.BlockSpec((tm, tn), lambda j, i: (i, gr_blk + j)),
                  pl.BlockSpec((wa_in, tn), lambda j, i: (0, j)),
                  pl.BlockSpec((wr_in, tn), lambda j, i: (0, j))],
        out_specs=pl.BlockSpec((tm, tn), lambda j, i: (i, j)),
        out_shape=jax.ShapeDtypeStruct((t, d), BF16),
        scratch_shapes=[pltpu.VMEM((wa_in, tn), BF16), pltpu.VMEM((wr_in, tn), BF16)],
        compiler_params=_params("parallel", "arbitrary"),
        name="merge",
    )(o_a, y_r, proj, proj, w_a, w_r)


def _outproj_kernel(x_ref, mix_ref, wo_ref, nw_ref, wr_ref, br_ref, x1_ref, h_ref, route_ref):
    tm = x_ref.shape[0]
    sub = tm // OUTPROJ_SUB
    blocks = [pl.ds(k * sub, sub) for k in range(OUTPROJ_SUB)]
    proj = [jnp.dot(mix_ref[rows, :], wo_ref[...], preferred_element_type=F32) for rows in blocks]
    parts = []
    for rows, pr in zip(blocks, proj):
        x1 = x_ref[rows, :] + pr
        x1_ref[rows, :] = x1
        ms = jnp.mean(x1 * x1, axis=-1, keepdims=True)
        h = (x1 * lax.rsqrt(ms + RMS_EPS)) * nw_ref[...]
        h_ref[rows, :] = h
        h_hi = h.astype(BF16)
        h_lo = (h - h_hi.astype(F32)).astype(BF16)
        r = (jnp.dot(h_hi, wr_ref[...], preferred_element_type=F32)
             + jnp.dot(h_lo, wr_ref[...], preferred_element_type=F32))
        parts.append((r[:, :V7X_LANES] + r[:, V7X_LANES:]).T[:ROUTER_ROWS, :])
    logits = jnp.concatenate(parts, axis=1) + br_ref[...]
    row = lax.broadcasted_iota(jnp.int32, logits.shape, 0)
    ninf = -jnp.inf
    big = ROUTER_ROWS
    gl = jnp.where(row < N_GROUPS, logits, ninf)
    gmax = jnp.max(gl, axis=0, keepdims=True)
    gsum = jnp.sum(jnp.exp(gl - gmax), axis=0, keepdims=True)
    gidx = jnp.min(jnp.where(gl == gmax, row, big), axis=0, keepdims=True)
    g_weight = 1.0 / gsum
    lo = ROUTER_EXPERT_ROW + EXPERTS_PER_GROUP * gidx
    in_group = (row >= lo) & (row < lo + EXPERTS_PER_GROUP)
    el = jnp.where(in_group, logits, ninf)
    emax = jnp.max(el, axis=0, keepdims=True)
    i1 = jnp.min(jnp.where(el == emax, row, big), axis=0, keepdims=True)
    el2 = jnp.where(row == i1, ninf, el)
    emax2 = jnp.max(el2, axis=0, keepdims=True)
    i2 = jnp.min(jnp.where(el2 == emax2, row, big), axis=0, keepdims=True)
    esum = jnp.sum(jnp.exp(el - emax), axis=0, keepdims=True)
    p1 = 1.0 / esum
    p2 = jnp.exp(emax2 - emax) / esum
    c1 = g_weight * (p1 / (p1 + p2))
    c2 = g_weight * (p2 / (p1 + p2))
    e1 = (i1 - ROUTER_EXPERT_ROW).astype(F32)
    e2 = (i2 - ROUTER_EXPERT_ROW).astype(F32)
    orow = lax.broadcasted_iota(jnp.int32, route_ref.shape, 0)
    route_ref[...] = jnp.where(orow == 0, c1, jnp.where(orow == 1, c2, jnp.where(orow == 2, e1,
                               jnp.where(orow == 3, e2, 0.0))))


def _outproj(x2, mix, w_out, norm_w, w_router_t, b_router_t):
    t, d = x2.shape
    tm = min(OUTPROJ_TM, t)
    row = lambda i: (i, 0)
    const = lambda i: (0, 0)
    return pl.pallas_call(
        _outproj_kernel,
        grid=(t // tm,),
        in_specs=[pl.BlockSpec((tm, d), row),
                  pl.BlockSpec((tm, d), row),
                  pl.BlockSpec((d, d), const),
                  pl.BlockSpec((1, d), const),
                  pl.BlockSpec((d, 2 * V7X_LANES), const),
                  pl.BlockSpec((ROUTER_ROWS, 1), const)],
        out_specs=[pl.BlockSpec((tm, d), row),
                   pl.BlockSpec((tm, d), row),
                   pl.BlockSpec((ROUTE_OUT_ROWS, tm), lambda i: (0, i))],
        out_shape=[jax.ShapeDtypeStruct((t, d), F32),
                   jax.ShapeDtypeStruct((t, d), F32),
                   jax.ShapeDtypeStruct((ROUTE_OUT_ROWS, t), F32)],
        compiler_params=_params("parallel"),
        name="outproj",
    )(x2, mix, w_out, norm_w.reshape(1, d), w_router_t, b_router_t)


def _router_params(w_rg, b_rg, w_re, b_re):
    d = w_rg.shape[0]
    gap = ROUTER_EXPERT_ROW - N_GROUPS
    w = jnp.concatenate([w_rg, jnp.zeros((d, gap), F32),
                         jnp.transpose(w_re, (1, 0, 2)).reshape(d, N_EXPERTS),
                         jnp.zeros((d, V7X_LANES - ROUTER_ROWS), F32)], axis=1)
    w_hi = w.astype(BF16)
    w_lo = (w - w_hi.astype(F32)).astype(BF16)
    b_t = jnp.concatenate([b_rg, jnp.zeros((gap,), F32), b_re.reshape(N_EXPERTS)])
    return jnp.concatenate([w_hi, w_lo], axis=1), b_t.reshape(ROUTER_ROWS, 1)


def _meta(meta_ref, row, col=0):
    return meta_ref[row * PLAN_LANES + col]


def _dispatch_kernel(pos_ref, h_ref, zeros_hbm, x_hbm, hbuf, sem):
    del zeros_hbm
    i = pl.program_id(0)
    tt = h_ref.shape[0]
    n_tok = tt * pl.num_programs(0)
    slot = lax.rem(i, 2)

    def drain(s):
        for k in range(2):
            pltpu.make_async_copy(hbuf.at[s], x_hbm.at[pl.ds(0, tt), :], sem.at[s]).wait()

    @pl.when(i >= 2)
    def _():
        drain(slot)

    hbuf[slot] = h_ref[...]
    base = i * tt

    def start(r, c):
        for k in range(2):
            p = pos_ref[k * n_tok + base + r]
            pltpu.make_async_copy(hbuf.at[slot, pl.ds(r, 1), :], x_hbm.at[pl.ds(p, 1), :],
                                  sem.at[slot]).start()
        return c

    lax.fori_loop(0, tt, start, 0, unroll=True)

    @pl.when(i == pl.num_programs(0) - 1)
    def _():
        @pl.when(i >= 1)
        def _():
            drain(1 - slot)
        drain(slot)


def _dispatch(h, pos, zeros):
    t, d = h.shape
    tt = min(DISPATCH_TT, t)
    grid_spec = pltpu.PrefetchScalarGridSpec(
        num_scalar_prefetch=1,
        grid=(t // tt,),
        in_specs=[pl.BlockSpec((tt, d), lambda i, pos: (i, 0)), pl.BlockSpec(memory_space=pl.ANY)],
        out_specs=pl.BlockSpec(memory_space=pl.ANY),
        scratch_shapes=[pltpu.VMEM((2, tt, d), F32), pltpu.SemaphoreType.DMA((2,))],
    )
    return pl.pallas_call(
        _dispatch_kernel,
        grid_spec=grid_spec,
        out_shape=jax.ShapeDtypeStruct(zeros.shape, zeros.dtype),
        input_output_aliases={2: 0},
        compiler_params=_params("arbitrary"),
        name="dispatch",
    )(pos, h, zeros)


def _experts_kernel(meta_ref, x_ref, wg_hbm, wu_hbm, wd_hbm, y_ref, wgf, wuf, wdf, wgb, wub, wdb, wsem):
    i = pl.program_id(0)
    n_tiles = _meta(meta_ref, META_N_TILES)
    n_act = _meta(meta_ref, META_N_ACTIVE)

    def weight_copies(q):
        e = _meta(meta_ref, META_ACTIVE, q)
        return (pltpu.make_async_copy(wg_hbm.at[e], wgf, wsem.at[0]),
                pltpu.make_async_copy(wu_hbm.at[e], wuf, wsem.at[1]),
                pltpu.make_async_copy(wd_hbm.at[e], wdf, wsem.at[2]))

    @pl.when(i == 0)
    def _():
        for c in weight_copies(0):
            c.start()

    @pl.when(i < n_tiles)
    def _():
        q = _meta(meta_ref, META_TILE_SEQ, i)

        @pl.when((i == 0) | (q != _meta(meta_ref, META_TILE_SEQ, jnp.maximum(i - 1, 0))))
        def _():
            for c in weight_copies(q):
                c.wait()
            wgb[...] = wgf[...].astype(BF16)
            wub[...] = wuf[...].astype(BF16)
            wdb[...] = wdf[...].astype(BF16)

            @pl.when(q + 1 < n_act)
            def _():
                for c in weight_copies(q + 1):
                    c.start()

        x = x_ref[...].astype(BF16)
        gate = jnp.dot(x, wgb[...], preferred_element_type=F32)
        up = jnp.dot(x, wub[...], preferred_element_type=F32)
        act = (gate * jax.nn.sigmoid(gate)) * up
        y_ref[...] = jnp.dot(act.astype(BF16), wdb[...], preferred_element_type=F32)

    @pl.when(i >= n_tiles)
    def _():
        y_ref[...] = jnp.zeros_like(y_ref)


def _experts(x_sorted, meta, w_gate, w_up, w_down):
    d = x_sorted.shape[1]
    f = w_gate.shape[2]
    tm = MOE_TM
    max_tiles = x_sorted.shape[0] // tm
    any_spec = pl.BlockSpec(memory_space=pl.ANY)
    grid_spec = pltpu.PrefetchScalarGridSpec(
        num_scalar_prefetch=1,
        grid=(max_tiles,),
        in_specs=[pl.BlockSpec((tm, d), lambda i, meta: (jnp.minimum(i, _meta(meta, META_N_TILES) - 1), 0)),
                  any_spec, any_spec, any_spec],
        out_specs=pl.BlockSpec((tm, d), lambda i, meta: (i, 0)),
        scratch_shapes=[pltpu.VMEM((d, f), F32), pltpu.VMEM((d, f), F32), pltpu.VMEM((f, d), F32),
                        pltpu.VMEM((d, f), BF16), pltpu.VMEM((d, f), BF16), pltpu.VMEM((f, d), BF16),
                        pltpu.SemaphoreType.DMA((3,))],
    )
    return pl.pallas_call(
        _experts_kernel,
        grid_spec=grid_spec,
        out_shape=jax.ShapeDtypeStruct((max_tiles * tm, d), F32),
        compiler_params=_params("arbitrary"),
        name="experts",
    )(meta, x_sorted, w_gate, w_up, w_down)


def _combine_kernel(pos_ref, x_ref, cw_ref, y_hbm, nw_ref, o_ref, ybuf, sem, *, final_norm):
    i = pl.program_id(0)
    tm = x_ref.shape[0]
    n_tok = tm * pl.num_programs(0)
    slot = lax.rem(i, 2)

    def start_gather(tile, dst_slot, unroll):
        base = tile * tm

        def start(r, c):
            for k in range(2):
                p = pos_ref[k * n_tok + base + r]
                pltpu.make_async_copy(y_hbm.at[pl.ds(p, 1), :], ybuf.at[dst_slot, k, pl.ds(r, 1), :],
                                      sem.at[dst_slot]).start()
            return c

        lax.fori_loop(0, tm, start, 0, unroll=unroll)

    @pl.when(i == 0)
    def _():
        start_gather(0, 0, unroll=8)

    @pl.when(i + 1 < pl.num_programs(0))
    def _():
        start_gather(i + 1, 1 - slot, unroll=True)

    for k in range(2):
        pltpu.make_async_copy(y_hbm.at[pl.ds(0, tm), :], ybuf.at[slot, k], sem.at[slot]).wait()
    cw = cw_ref[...]
    x = x_ref[...] + (cw[:, 0:1] * ybuf[slot, 0] + cw[:, 1:2] * ybuf[slot, 1])
    if final_norm:
        ms = jnp.mean(x * x, axis=-1, keepdims=True)
        x = (x * lax.rsqrt(ms + RMS_EPS)) * nw_ref[...]
    o_ref[...] = x


def _combine(x1, cw, y_sorted, pos, norm_w, final_norm):
    t, d = x1.shape
    tm = min(COMBINE_TM, t)
    grid_spec = pltpu.PrefetchScalarGridSpec(
        num_scalar_prefetch=1,
        grid=(t // tm,),
        in_specs=[pl.BlockSpec((tm, d), lambda i, pos: (i, 0)),
                  pl.BlockSpec((tm, 2), lambda i, pos: (i, 0)),
                  pl.BlockSpec(memory_space=pl.ANY),
                  pl.BlockSpec((1, d), lambda i, pos: (0, 0))],
        out_specs=pl.BlockSpec((tm, d), lambda i, pos: (i, 0)),
        scratch_shapes=[pltpu.VMEM((2, 2, tm, d), F32), pltpu.SemaphoreType.DMA((2,))],
    )
    return pl.pallas_call(
        functools.partial(_combine_kernel, final_norm=final_norm),
        grid_spec=grid_spec,
        out_shape=jax.ShapeDtypeStruct((t, d), F32),
        compiler_params=_params("arbitrary"),
        name="combine",
    )(pos, x1, cw, y_sorted, norm_w.reshape(1, d))


def _rope_tables(s):
    half = HEAD_DIM_RET_QK // 2
    inv = ROPE_BASE ** (-np.arange(half, dtype=np.float64) / half)
    ang = np.arange(s, dtype=np.float64)[:, None] * inv[None, :]
    cos, sin = np.cos(ang), np.sin(ang)
    return (jnp.asarray(np.concatenate([cos, cos], axis=-1), F32),
            jnp.asarray(np.concatenate([-sin, sin], axis=-1), F32))


META_TILE_SEQ, META_ACTIVE, META_N_ACTIVE, META_N_TILES = 0, 1, 2, 3
META_ROWS = 8
PLAN_LANES = V7X_LANES


def _plan_kernel(route_ref, pfx_ref, tri_ref, pos_ref, meta_ref, *, tm):
    ne, lanes = N_EXPERTS, PLAN_LANES
    t = route_ref.shape[1]
    nblk = t // lanes
    ids = lax.broadcasted_iota(jnp.int32, (ne, lanes), 0).astype(F32)

    def onehots(blk):
        cols = slice(blk * lanes, (blk + 1) * lanes)
        return [(ids == route_ref[2 + k:3 + k, cols]).astype(F32) for k in range(2)]

    stacked = jnp.concatenate([sum(onehots(blk)) for blk in range(nblk)], axis=0)
    prod = jnp.dot(stacked.astype(BF16), pfx_ref[...], preferred_element_type=F32)
    carry = jnp.zeros((ne, lanes), F32)
    before = []
    for blk in range(nblk):
        rows = slice(blk * ne, (blk + 1) * ne)
        before.append(prod[rows, :lanes] + carry)
        carry = carry + prod[rows, lanes:]
    counts = carry
    tiles_per = jnp.floor((counts + (tm - 1.0)) * (1.0 / tm))
    tri = tri_ref[...]
    tile_end = jnp.dot(tri, tiles_per.astype(BF16), preferred_element_type=F32)
    row_start = (tile_end - tiles_per) * tm
    for k in range(2):
        pieces = []
        for blk in range(nblk):
            oh = onehots(blk)[k]
            pieces.append(jnp.sum(oh * (before[blk] + row_start), axis=0, keepdims=True))
        pos_ref[k:k + 1, :] = jnp.concatenate(pieces, axis=1).astype(jnp.int32)

    owns = (tiles_per > 0.0).astype(F32)
    seq = jnp.dot(tri, owns.astype(BF16), preferred_element_type=F32) - 1.0
    n_active = seq[ne - 1:ne, :] + 1.0
    n_tiles = tile_end[ne - 1:ne, :]
    lane = lax.broadcasted_iota(jnp.int32, (ne, lanes), 1).astype(F32)
    tile = jnp.minimum(lane, n_tiles - 1.0)
    tile_seq = jnp.sum(owns * (tile_end <= tile).astype(F32), axis=0, keepdims=True)
    active = jnp.sum(jnp.where((owns > 0.0) & (seq == lane), ids, 0.0), axis=0, keepdims=True)
    zero = jnp.zeros((1, lanes), F32)
    rows = {META_TILE_SEQ: tile_seq, META_ACTIVE: active, META_N_ACTIVE: n_active, META_N_TILES: n_tiles}
    meta_ref[...] = jnp.concatenate([rows.get(r, zero) for r in range(META_ROWS)], axis=0).astype(jnp.int32)


def _plan(route_t, tm, max_tiles):
    t = route_t.shape[1]
    assert max_tiles <= PLAN_LANES and N_EXPERTS <= PLAN_LANES and t % PLAN_LANES == 0
    r = jnp.arange(PLAN_LANES)
    prefix = jnp.concatenate([(r[:, None] < r[None, :]).astype(BF16),
                              jnp.ones((PLAN_LANES, PLAN_LANES), BF16)], axis=1)
    e = jnp.arange(N_EXPERTS)
    tri = (e[None, :] <= e[:, None]).astype(BF16)
    pos, meta = pl.pallas_call(
        functools.partial(_plan_kernel, tm=tm),
        out_shape=[jax.ShapeDtypeStruct((2, t), jnp.int32),
                   jax.ShapeDtypeStruct((META_ROWS, PLAN_LANES), jnp.int32)],
        compiler_params=pltpu.CompilerParams(vmem_limit_bytes=V7X_VMEM_LIMIT_BYTES),
        name="plan",
    )(route_t, prefix, tri)
    return pos.reshape(2 * t), meta.reshape(META_ROWS * PLAN_LANES)


def kernel(x, norm_mix_w, w_in, ret_gn_w, w_branch_moba, w_branch_ret, w_out, norm_ffn_w, w_router_group, b_router_group, w_router_expert, b_router_expert, w_expert_gate, w_expert_up, w_expert_down, norm_final_w):
    b, s, d = x.shape
    t = b * s
    depth = w_in.shape[0]
    assert s % MOBA_BLOCK == 0 and s % RET_CHUNK == 0 and t % MOE_TM == 0
    cos_t, sin_t = _rope_tables(s)
    log_gamma = jnp.log(1.0 - 2.0 ** (-5.0 - jnp.arange(N_HEADS_RET, dtype=F32)))
    x2 = x.reshape(t, d)
    for l in range(depth):
        max_tiles = 2 * t // MOE_TM + N_EXPERTS
        proj, zeros = _inproj(x2, norm_mix_w[l], w_in[l], max_tiles * MOE_TM)
        proj3 = proj.reshape(b, s, proj.shape[1])
        o_a = _moba(proj3).reshape(t, -1)
        y_r = _retention(proj3, cos_t, sin_t, ret_gn_w[l], log_gamma).reshape(t, -1)
        mix = _merge(o_a, y_r, proj, w_branch_moba[l], w_branch_ret[l])
        w_router_t, b_router_t = _router_params(w_router_group[l], b_router_group[l],
                                                w_router_expert[l], b_router_expert[l])
        x1, h, route_t = _outproj(x2, mix, w_out[l].astype(BF16), norm_ffn_w[l], w_router_t, b_router_t)
        pos, meta = _plan(route_t, MOE_TM, max_tiles)
        x_sorted = _dispatch(h, pos, zeros)
        y_sorted = _experts(x_sorted, meta, w_expert_gate[l], w_expert_up[l], w_expert_down[l])
        x2 = _combine(x1, route_t[0:2].T, y_sorted, pos, norm_final_w, final_norm=(l == depth - 1))
    return x2.reshape(b, s, d)
```

```python
import functools

import jax
import jax.numpy as jnp
import numpy as np
from jax import lax
from jax.experimental import pallas as pl
from jax.experimental.pallas import tpu as pltpu

F32 = jnp.float32
BF16 = jnp.bfloat16

V7X_LANES = 128
V7X_VMEM_LIMIT_BYTES = 56 * 1024 * 1024

N_HEADS_MOBA = 8
HEAD_DIM_MOBA = 128
MOBA_BLOCK = 256
MOBA_TOPK = 3
N_HEADS_RET = 8
HEAD_DIM_RET_QK = 128
HEAD_DIM_RET_V = 256
ROPE_BASE = 10000.0
N_GROUPS = 4
EXPERTS_PER_GROUP = 8
N_EXPERTS = N_GROUPS * EXPERTS_PER_GROUP
RMS_EPS = 1e-6
GN_EPS = 1e-6
NEG_BIG = -1e30

ROUTER_EXPERT_ROW = 8
ROUTER_ROWS = ROUTER_EXPERT_ROW + N_EXPERTS
ROUTE_OUT_ROWS = 8
MOBA_HEADS_PER_STEP = 4
MOBA_BIAS_ROWS = 16
MOBA_SUM_ROWS = 16
RET_CHUNK = 256
RET_HEADS_PER_STEP = 2
INPROJ_TM, INPROJ_TN = 1024, 1024
INPROJ_MSUB = 2
MERGE_TM, MERGE_TN = 512, 1024
OUTPROJ_TM = 512
OUTPROJ_SUB = 4
MOE_TM = 256
DISPATCH_TT = 256
COMBINE_TM = 256


def _params(*semantics):
    return pltpu.CompilerParams(dimension_semantics=semantics,
                                vmem_limit_bytes=V7X_VMEM_LIMIT_BYTES)


def _inproj_kernel(x_ref, nw_ref, w_ref, o_ref, z_ref, h_ref):
    j, m = pl.program_id(1), pl.program_id(2)

    def project(h):
        z_ref[...] = jnp.zeros_like(z_ref)
        o_ref[...] = jnp.dot(h, w_ref[...].astype(BF16), preferred_element_type=F32).astype(o_ref.dtype)

    @pl.when(j == 0)
    def _():
        x = x_ref[...]
        ms = jnp.mean(x * x, axis=-1, keepdims=True)
        h = ((x * lax.rsqrt(ms + RMS_EPS)) * nw_ref[...]).astype(BF16)
        h_ref[m] = h
        project(h)

    @pl.when(j != 0)
    def _():
        project(h_ref[m])


def _inproj(x2, norm_w, w_in, zero_rows):
    t, d = x2.shape
    n = w_in.shape[1]
    tm, tn = min(INPROJ_TM, t), INPROJ_TN
    msub = min(INPROJ_MSUB, t // tm)
    nj = n // tn
    n_steps = (t // (tm * msub)) * nj * msub
    zb = pl.cdiv(pl.cdiv(zero_rows, n_steps), 8) * 8
    n_zblocks = pl.cdiv(zero_rows, zb)

    def x_map(p, j, m):
        return (jnp.where(j == 0, p * msub + m, p * msub + msub - 1), 0)

    def z_map(p, j, m):
        return (jnp.minimum((p * nj + j) * msub + m, n_zblocks - 1), 0)

    return pl.pallas_call(
        _inproj_kernel,
        grid=(t // (tm * msub), nj, msub),
        in_specs=[pl.BlockSpec((tm, d), x_map),
                  pl.BlockSpec((1, d), lambda p, j, m: (0, 0)),
                  pl.BlockSpec((d, tn), lambda p, j, m: (0, j))],
        out_specs=[pl.BlockSpec((tm, tn), lambda p, j, m: (p * msub + m, j)),
                   pl.BlockSpec((zb, d), z_map)],
        out_shape=[jax.ShapeDtypeStruct((t, n), BF16), jax.ShapeDtypeStruct((zero_rows, d), F32)],
        scratch_shapes=[pltpu.VMEM((msub, tm, d), BF16)],
        compiler_params=_params("arbitrary", "arbitrary", "arbitrary"),
        name="inproj",
    )(x2, norm_w.reshape(1, d), w_in)


def _moba_kernel(q_ref, k_ref, v_ref, o_ref, qta_ref, ka_ref, vta_ref, km_ref, acc_ref, sc_ref, *, nb, hb):
    blk, hd = MOBA_BLOCK, HEAD_DIM_MOBA
    i = pl.program_id(2)

    @pl.when(i == 0)
    def _():
        lane = lax.broadcasted_iota(jnp.int32, (blk, hd), 1)
        srow = lax.broadcasted_iota(jnp.int32, (MOBA_SUM_ROWS, blk), 0)
        for hh in range(hb):
            cols = slice(hh * hd, (hh + 1) * hd)
            means = []
            for c in range(nb):
                rows = slice(c * blk, (c + 1) * blk)
                qta_ref[hh, 0:hd, rows] = q_ref[rows, cols].astype(F32).T.astype(BF16)
                qta_ref[hh, hd:2 * hd, rows] = jnp.zeros((hd, blk), BF16)
                ka_ref[hh, rows, 0:hd] = k_ref[rows, cols]
                ka_ref[hh, rows, hd:2 * hd] = (lane == c).astype(BF16)
                vta_ref[hh, 0:hd, rows] = v_ref[rows, cols].astype(F32).T.astype(BF16)
                vta_ref[hh, hd:hd + MOBA_SUM_ROWS, rows] = (srow == 0).astype(BF16)
                means.append(jnp.sum(k_ref[rows, cols].astype(F32), axis=0, keepdims=True) * (1.0 / blk))
            means += [jnp.zeros((1, hd), F32)] * (MOBA_BIAS_ROWS - nb)
            rest = jnp.concatenate(means, axis=0)
            for part in range(3):
                term = rest.astype(BF16)
                km_ref[hh, part * MOBA_BIAS_ROWS:(part + 1) * MOBA_BIAS_ROWS, :] = term
                rest = rest - term.astype(F32)

    q0 = pl.multiple_of(i * blk, blk)
    qcols = pl.ds(q0, blk)
    c_exp = (hd ** -0.5) * 1.4426950408889634
    row = lax.broadcasted_iota(jnp.int32, (MOBA_BIAS_ROWS, blk), 0)
    kpos = lax.broadcasted_iota(jnp.int32, (blk, blk), 0)
    qpos = lax.broadcasted_iota(jnp.int32, (blk, blk), 1)

    qts = [qta_ref[hh, 0:hd, qcols] for hh in range(hb)]
    gates = [jnp.dot(km_ref[hh], qts[hh], preferred_element_type=F32) for hh in range(hb)]
    own = [jnp.dot(ka_ref[hh, qcols, 0:hd], qts[hh], preferred_element_type=F32)
           for hh in range(hb)]
    for hh in range(hb):
        g3 = gates[hh]
        gate = (g3[0:MOBA_BIAS_ROWS] + g3[MOBA_BIAS_ROWS:2 * MOBA_BIAS_ROWS]) + g3[2 * MOBA_BIAS_ROWS:]
        gate = jnp.where(row < i, gate, -jnp.inf)
        bias = jnp.full((MOBA_BIAS_ROWS, blk), NEG_BIG, F32)
        for _ in range(MOBA_TOPK):
            top = jnp.max(gate, axis=0, keepdims=True)
            is_top = (gate == top) & (top > -jnp.inf)
            first = jnp.min(jnp.where(is_top, row, MOBA_BIAS_ROWS), axis=0, keepdims=True)
            pick = row == first
            bias = jnp.where(pick, 0.0, bias)
            gate = jnp.where(pick, -jnp.inf, gate)
        qta_ref[hh, hd:hd + MOBA_BIAS_ROWS, qcols] = bias.astype(BF16)

    n_trips = (i + 1) // 2

    def key_rows(t):
        return pl.ds(pl.multiple_of(t * (2 * blk), 2 * blk), 2 * blk)

    def score_dots(t):
        return [jnp.dot(ka_ref[hh, key_rows(t), :], qta_ref[hh, :, qcols], preferred_element_type=F32)
                for hh in range(hb)]

    def scores_to(t, slot):
        for hh, s in enumerate(score_dots(t)):
            sc_ref[slot, hh] = s

    scores_to(0, 0)

    init, probs = [], []
    for hh in range(hb):
        s = jnp.where(kpos <= qpos, own[hh] * c_exp, NEG_BIG)
        m0 = jnp.max(s, axis=0, keepdims=True)
        init.append(m0)
        probs.append(jnp.exp2(s - m0).astype(BF16))
    for hh in range(hb):
        acc_ref[hh] = jnp.dot(vta_ref[hh, :, qcols], probs[hh], preferred_element_type=F32)

    def fold(t, slot, ms):
        probs, out = [], []
        for hh in range(hb):
            s = sc_ref[slot, hh] * c_exp
            m_new = jnp.maximum(ms[hh], jnp.max(s, axis=0, keepdims=True))
            probs.append((jnp.exp2(ms[hh] - m_new), jnp.exp2(s - m_new).astype(BF16)))
            out.append(m_new)
        for hh in range(hb):
            alpha, p = probs[hh]
            acc_ref[hh] = alpha * acc_ref[hh] + jnp.dot(vta_ref[hh, :, key_rows(t)], p,
                                                        preferred_element_type=F32)
        return tuple(out)

    def two_trips(u, ms):
        t = 2 * u
        scores_to(t + 1, 1)
        ms = fold(t, 0, ms)
        scores_to(t + 2, 0)
        return fold(t + 1, 1, ms)

    n_loop = jnp.maximum(n_trips - 1, 0) // 2
    ms = lax.fori_loop(0, n_loop, two_trips, tuple(init))
    t_tail = 2 * n_loop

    @pl.when(n_trips - t_tail == 2)
    def _():
        scores_to(t_tail + 1, 1)
        fold(t_tail + 1, 1, fold(t_tail, 0, ms))

    @pl.when(n_trips - t_tail == 1)
    def _():
        fold(t_tail, 0, ms)

    for hh in range(hb):
        acc = acc_ref[hh]
        o_ref[:, hh * hd:(hh + 1) * hd] = (acc[0:hd, :] / acc[hd:hd + 1, :]).T.astype(o_ref.dtype)


def _moba(proj3):
    b, s, _ = proj3.shape
    nb = s // MOBA_BLOCK
    assert nb % 2 == 0 and nb <= MOBA_BIAS_ROWS
    hd, nh, hb = HEAD_DIM_MOBA, N_HEADS_MOBA, MOBA_HEADS_PER_STEP
    ng = nh // hb
    return pl.pallas_call(
        functools.partial(_moba_kernel, nb=nb, hb=hb),
        grid=(b, ng, nb),
        in_specs=[pl.BlockSpec((None, s, hb * hd), lambda bi, g, i: (bi, 0, g)),
                  pl.BlockSpec((None, s, hb * hd), lambda bi, g, i: (bi, 0, ng + g)),
                  pl.BlockSpec((None, s, hb * hd), lambda bi, g, i: (bi, 0, 2 * ng + g))],
        out_specs=pl.BlockSpec((None, MOBA_BLOCK, hb * hd), lambda bi, g, i: (bi, i, g)),
        out_shape=jax.ShapeDtypeStruct((b, s, nh * hd), BF16),
        scratch_shapes=[pltpu.VMEM((hb, 2 * hd, s), BF16), pltpu.VMEM((hb, s, 2 * hd), BF16),
                        pltpu.VMEM((hb, hd + MOBA_SUM_ROWS, s), BF16),
                        pltpu.VMEM((hb, 3 * MOBA_BIAS_ROWS, hd), BF16),
                        pltpu.VMEM((hb, hd + MOBA_SUM_ROWS, MOBA_BLOCK), F32),
                        pltpu.VMEM((2, hb, 2 * MOBA_BLOCK, MOBA_BLOCK), F32)],
        compiler_params=_params("parallel", "parallel", "arbitrary"),
        name="moba",
    )(proj3, proj3, proj3)


def _retention_kernel(lg_ref, q_ref, k_ref, v_ref, g_ref, cos_ref, sin_ref, gnw_ref, o_ref,
                      decay_ref, xi_ref, zeta_ref, st_ref, *, n_chunks, hb):
    c_len = RET_CHUNK
    dk, dv = HEAD_DIM_RET_QK, HEAD_DIM_RET_V
    ri = lax.broadcasted_iota(jnp.int32, (c_len, c_len), 0).astype(F32)
    ci = lax.broadcasted_iota(jnp.int32, (c_len, c_len), 1).astype(F32)
    diff = ri - ci
    idx = lax.broadcasted_iota(jnp.int32, (c_len, dk), 0).astype(F32)
    chunk_decay = []
    for hh in range(hb):
        lg = lg_ref[pl.program_id(1) * hb + hh]
        decay_ref[hh] = jnp.where(diff >= 0, jnp.exp(lg * jnp.maximum(diff, 0.0)), 0.0)
        xi_ref[hh] = jnp.exp(lg * (idx + 1.0))
        zeta_ref[hh] = jnp.exp(lg * (c_len - 1.0 - idx))
        chunk_decay.append(jnp.exp(jnp.zeros((1, dv), F32) + lg * c_len))
        st_ref[hh] = jnp.zeros((dk, dv), F32)
    k_scale = dk ** -0.5
    nt = (((1,), (1,)), ((), ()))

    def chunk(c, carry):
        r0 = pl.multiple_of(c * c_len, c_len)
        rows = pl.ds(r0, c_len)
        cs, sn = cos_ref[rows, :], sin_ref[rows, :]
        qbs, kbs, qxs, kzs = [], [], [], []
        for hh in range(hb):
            q = q_ref[rows, hh * dk:(hh + 1) * dk].astype(F32)
            k = k_ref[rows, hh * dk:(hh + 1) * dk].astype(F32)
            qr = q * cs + pltpu.roll(q, dk // 2, 1) * sn
            kr = (k * cs + pltpu.roll(k, dk // 2, 1) * sn) * k_scale
            qbs.append(qr.astype(BF16))
            kbs.append(kr.astype(BF16))
            qxs.append((qr * xi_ref[hh]).astype(BF16))
            kzs.append((kr * zeta_ref[hh]).T.astype(BF16))
        vs = [v_ref[rows, hh * dv:(hh + 1) * dv] for hh in range(hb)]
        scores = [lax.dot_general(qbs[hh], kbs[hh], nt, preferred_element_type=F32) for hh in range(hb)]
        cross = [jnp.dot(qxs[hh], st_ref[hh].astype(BF16), preferred_element_type=F32) for hh in range(hb)]
        upd = [jnp.dot(kzs[hh], vs[hh], preferred_element_type=F32) for hh in range(hb)]
        sd = [(scores[hh] * decay_ref[hh]).astype(BF16) for hh in range(hb)]
        inner = [jnp.dot(sd[hh], vs[hh], preferred_element_type=F32) for hh in range(hb)]
        for hh in range(hb):
            st_ref[hh] = chunk_decay[hh] * st_ref[hh] + upd[hh]
            o = inner[hh] + cross[hh]
            mu = jnp.mean(o, axis=-1, keepdims=True)
            var = jnp.mean(jnp.square(o - mu), axis=-1, keepdims=True)
            yn = (o - mu) * lax.rsqrt(var + GN_EPS)
            cols = slice(hh * dv, (hh + 1) * dv)
            g = g_ref[rows, cols].astype(F32)
            y = (g * jax.nn.sigmoid(g)) * (yn * gnw_ref[:, cols])
            o_ref[rows, cols] = y.astype(o_ref.dtype)
        return carry

    lax.fori_loop(0, n_chunks, chunk, 0, unroll=2)


def _retention(proj3, cos_t, sin_t, gn_w, log_gamma):
    b, s, _ = proj3.shape
    nh, dk, dv, hb = N_HEADS_RET, HEAD_DIM_RET_QK, HEAD_DIM_RET_V, RET_HEADS_PER_STEP
    ng = nh // hb
    q_blk = 3 * N_HEADS_MOBA * HEAD_DIM_MOBA // (hb * dk)
    k_blk = q_blk + ng
    v_blk = (k_blk + ng) * dk // dv
    g_blk = v_blk + ng
    grid_spec = pltpu.PrefetchScalarGridSpec(
        num_scalar_prefetch=1,
        grid=(b, ng),
        in_specs=[pl.BlockSpec((None, s, hb * dk), lambda bi, g, lg: (bi, 0, q_blk + g)),
                  pl.BlockSpec((None, s, hb * dk), lambda bi, g, lg: (bi, 0, k_blk + g)),
                  pl.BlockSpec((None, s, hb * dv), lambda bi, g, lg: (bi, 0, v_blk + g)),
                  pl.BlockSpec((None, s, hb * dv), lambda bi, g, lg: (bi, 0, g_blk + g)),
                  pl.BlockSpec((s, dk), lambda bi, g, lg: (0, 0)),
                  pl.BlockSpec((s, dk), lambda bi, g, lg: (0, 0)),
                  pl.BlockSpec((1, hb * dv), lambda bi, g, lg: (0, g))],
        out_specs=pl.BlockSpec((None, s, hb * dv), lambda bi, g, lg: (bi, 0, g)),
        scratch_shapes=[pltpu.VMEM((hb, RET_CHUNK, RET_CHUNK), F32), pltpu.VMEM((hb, RET_CHUNK, dk), F32),
                        pltpu.VMEM((hb, RET_CHUNK, dk), F32), pltpu.VMEM((hb, dk, dv), F32)],
    )
    return pl.pallas_call(
        functools.partial(_retention_kernel, n_chunks=s // RET_CHUNK, hb=hb),
        grid_spec=grid_spec,
        out_shape=jax.ShapeDtypeStruct((b, s, nh * dv), BF16),
        compiler_params=_params("parallel", "parallel"),
        name="retention",
    )(log_gamma, proj3, proj3, proj3, proj3, cos_t, sin_t, gn_w.reshape(1, nh * dv))


def _merge_kernel(oa_ref, yr_ref, ga_ref, gr_ref, wa_ref, wr_ref, o_ref, wab_ref, wrb_ref):
    @pl.when(pl.program_id(1) == 0)
    def _():
        wab_ref[...] = wa_ref[...].astype(BF16)
        wrb_ref[...] = wr_ref[...].astype(BF16)

    a = jnp.dot(oa_ref[...], wab_ref[...], preferred_element_type=F32)
    r = jnp.dot(yr_ref[...], wrb_ref[...], preferred_element_type=F32)
    mix = jax.nn.sigmoid(ga_ref[...].astype(F32)) * a + jax.nn.sigmoid(gr_ref[...].astype(F32)) * r
    o_ref[...] = mix.astype(o_ref.dtype)


def _merge(o_a, y_r, proj, w_a, w_r):
    t, wa_in = o_a.shape
    wr_in = y_r.shape[1]
    d = w_a.shape[1]
    tm, tn = min(MERGE_TM, t), MERGE_TN
    ga_blk = (proj.shape[1] - 2 * d) // tn
    gr_blk = ga_blk + d // tn
    return pl.pallas_call(
        _merge_kernel,
        grid=(d // tn, t // tm),
        in_specs=[pl.BlockSpec((tm, wa_in), lambda j, i: (i, 0)),
                  pl.BlockSpec((tm, wr_in), lambda j, i: (i, 0)),
                  pl.BlockSpec((tm, tn), lambda j, i: (i, ga_blk + j)),
                  pl.BlockSpec((tm, tn), lambda j, i: (i, gr_blk + j)),
                  pl.BlockSpec((wa_in, tn), lambda j, i: (0, j)),
                  pl.BlockSpec((wr_in, tn), lambda j, i: (0, j))],
        out_specs=pl.BlockSpec((tm, tn), lambda j, i: (i, j)),
        out_shape=jax.ShapeDtypeStruct((t, d), BF16),
        scratch_shapes=[pltpu.VMEM((wa_in, tn), BF16), pltpu.VMEM((wr_in, tn), BF16)],
        compiler_params=_params("parallel", "arbitrary"),
        name="merge",
    )(o_a, y_r, proj, proj, w_a, w_r)


def _outproj_kernel(x_ref, mix_ref, wo_ref, nw_ref, wr_ref, br_ref, x1_ref, h_ref, route_ref):
    tm = x_ref.shape[0]
    sub = tm // OUTPROJ_SUB
    blocks = [pl.ds(k * sub, sub) for k in range(OUTPROJ_SUB)]
    proj = [jnp.dot(mix_ref[rows, :], wo_ref[...], preferred_element_type=F32) for rows in blocks]
    parts = []
    for rows, pr in zip(blocks, proj):
        x1 = x_ref[rows, :] + pr
        x1_ref[rows, :] = x1
        ms = jnp.mean(x1 * x1, axis=-1, keepdims=True)
        h = (x1 * lax.rsqrt(ms + RMS_EPS)) * nw_ref[...]
        h_ref[rows, :] = h
        h_hi = h.astype(BF16)
        h_lo = (h - h_hi.astype(F32)).astype(BF16)
        r = (jnp.dot(h_hi, wr_ref[...], preferred_element_type=F32)
             + jnp.dot(h_lo, wr_ref[...], preferred_element_type=F32))
        parts.append((r[:, :V7X_LANES] + r[:, V7X_LANES:]).T[:ROUTER_ROWS, :])
    logits = jnp.concatenate(parts, axis=1) + br_ref[...]
    row = lax.broadcasted_iota(jnp.int32, logits.shape, 0)
    ninf = -jnp.inf
    big = ROUTER_ROWS
    gl = jnp.where(row < N_GROUPS, logits, ninf)
    gmax = jnp.max(gl, axis=0, keepdims=True)
    gsum = jnp.sum(jnp.exp(gl - gmax), axis=0, keepdims=True)
    gidx = jnp.min(jnp.where(gl == gmax, row, big), axis=0, keepdims=True)
    g_weight = 1.0 / gsum
    lo = ROUTER_EXPERT_ROW + EXPERTS_PER_GROUP * gidx
    in_group = (row >= lo) & (row < lo + EXPERTS_PER_GROUP)
    el = jnp.where(in_group, logits, ninf)
    emax = jnp.max(el, axis=0, keepdims=True)
    i1 = jnp.min(jnp.where(el == emax, row, big), axis=0, keepdims=True)
    el2 = jnp.where(row == i1, ninf, el)
    emax2 = jnp.max(el2, axis=0, keepdims=True)
    i2 = jnp.min(jnp.where(el2 == emax2, row, big), axis=0, keepdims=True)
    esum = jnp.sum(jnp.exp(el - emax), axis=0, keepdims=True)
    p1 = 1.0 / esum
    p2 = jnp.exp(emax2 - emax) / esum
    c1 = g_weight * (p1 / (p1 + p2))
    c2 = g_weight * (p2 / (p1 + p2))
    e1 = (i1 - ROUTER_EXPERT_ROW).astype(F32)
    e2 = (i2 - ROUTER_EXPERT_ROW).astype(F32)
    orow = lax.broadcasted_iota(jnp.int32, route_ref.shape, 0)
    route_ref[...] = jnp.where(orow == 0, c1, jnp.where(orow == 1, c2, jnp.where(orow == 2, e1,
                               jnp.where(orow == 3, e2, 0.0))))


def _outproj(x2, mix, w_out, norm_w, w_router_t, b_router_t):
    t, d = x2.shape
    tm = min(OUTPROJ_TM, t)
    row = lambda i: (i, 0)
    const = lambda i: (0, 0)
    return pl.pallas_call(
        _outproj_kernel,
        grid=(t // tm,),
        in_specs=[pl.BlockSpec((tm, d), row),
                  pl.BlockSpec((tm, d), row),
                  pl.BlockSpec((d, d), const),
                  pl.BlockSpec((1, d), const),
                  pl.BlockSpec((d, 2 * V7X_LANES), const),
                  pl.BlockSpec((ROUTER_ROWS, 1), const)],
        out_specs=[pl.BlockSpec((tm, d), row),
                   pl.BlockSpec((tm, d), row),
                   pl.BlockSpec((ROUTE_OUT_ROWS, tm), lambda i: (0, i))],
        out_shape=[jax.ShapeDtypeStruct((t, d), F32),
                   jax.ShapeDtypeStruct((t, d), F32),
                   jax.ShapeDtypeStruct((ROUTE_OUT_ROWS, t), F32)],
        compiler_params=_params("parallel"),
        name="outproj",
    )(x2, mix, w_out, norm_w.reshape(1, d), w_router_t, b_router_t)


def _router_params(w_rg, b_rg, w_re, b_re):
    d = w_rg.shape[0]
    gap = ROUTER_EXPERT_ROW - N_GROUPS
    w = jnp.concatenate([w_rg, jnp.zeros((d, gap), F32),
                         jnp.transpose(w_re, (1, 0, 2)).reshape(d, N_EXPERTS),
                         jnp.zeros((d, V7X_LANES - ROUTER_ROWS), F32)], axis=1)
    w_hi = w.astype(BF16)
    w_lo = (w - w_hi.astype(F32)).astype(BF16)
    b_t = jnp.concatenate([b_rg, jnp.zeros((gap,), F32), b_re.reshape(N_EXPERTS)])
    return jnp.concatenate([w_hi, w_lo], axis=1), b_t.reshape(ROUTER_ROWS, 1)


def _meta(meta_ref, row, col=0):
    return meta_ref[row * PLAN_LANES + col]


def _dispatch_kernel(pos_ref, h_ref, zeros_hbm, x_hbm, hbuf, sem):
    del zeros_hbm
    i = pl.program_id(0)
    tt = h_ref.shape[0]
    n_tok = tt * pl.num_programs(0)
    slot = lax.rem(i, 2)

    def drain(s):
        for k in range(2):
            pltpu.make_async_copy(hbuf.at[s], x_hbm.at[pl.ds(0, tt), :], sem.at[s]).wait()

    @pl.when(i >= 2)
    def _():
        drain(slot)

    hbuf[slot] = h_ref[...]
    base = i * tt

    def start(r, c):
        for k in range(2):
            p = pos_ref[k * n_tok + base + r]
            pltpu.make_async_copy(hbuf.at[slot, pl.ds(r, 1), :], x_hbm.at[pl.ds(p, 1), :],
                                  sem.at[slot]).start()
        return c

    lax.fori_loop(0, tt, start, 0, unroll=True)

    @pl.when(i == pl.num_programs(0) - 1)
    def _():
        @pl.when(i >= 1)
        def _():
            drain(1 - slot)
        drain(slot)


def _dispatch(h, pos, zeros):
    t, d = h.shape
    tt = min(DISPATCH_TT, t)
    grid_spec = pltpu.PrefetchScalarGridSpec(
        num_scalar_prefetch=1,
        grid=(t // tt,),
        in_specs=[pl.BlockSpec((tt, d), lambda i, pos: (i, 0)), pl.BlockSpec(memory_space=pl.ANY)],
        out_specs=pl.BlockSpec(memory_space=pl.ANY),
        scratch_shapes=[pltpu.VMEM((2, tt, d), F32), pltpu.SemaphoreType.DMA((2,))],
    )
    return pl.pallas_call(
        _dispatch_kernel,
        grid_spec=grid_spec,
        out_shape=jax.ShapeDtypeStruct(zeros.shape, zeros.dtype),
        input_output_aliases={2: 0},
        compiler_params=_params("arbitrary"),
        name="dispatch",
    )(pos, h, zeros)


def _experts_kernel(meta_ref, x_ref, wg_hbm, wu_hbm, wd_hbm, y_ref, wgf, wuf, wdf, wgb, wub, wdb, wsem):
    i = pl.program_id(0)
    n_tiles = _meta(meta_ref, META_N_TILES)
    n_act = _meta(meta_ref, META_N_ACTIVE)

    def weight_copies(q):
        e = _meta(meta_ref, META_ACTIVE, q)
        return (pltpu.make_async_copy(wg_hbm.at[e], wgf, wsem.at[0]),
                pltpu.make_async_copy(wu_hbm.at[e], wuf, wsem.at[1]),
                pltpu.make_async_copy(wd_hbm.at[e], wdf, wsem.at[2]))

    @pl.when(i == 0)
    def _():
        for c in weight_copies(0):
            c.start()

    @pl.when(i < n_tiles)
    def _():
        q = _meta(meta_ref, META_TILE_SEQ, i)

        @pl.when((i == 0) | (q != _meta(meta_ref, META_TILE_SEQ, jnp.maximum(i - 1, 0))))
        def _():
            for c in weight_copies(q):
                c.wait()
            wgb[...] = wgf[...].astype(BF16)
            wub[...] = wuf[...].astype(BF16)
            wdb[...] = wdf[...].astype(BF16)

            @pl.when(q + 1 < n_act)
            def _():
                for c in weight_copies(q + 1):
                    c.start()

        x = x_ref[...].astype(BF16)
        gate = jnp.dot(x, wgb[...], preferred_element_type=F32)
        up = jnp.dot(x, wub[...], preferred_element_type=F32)
        act = (gate * jax.nn.sigmoid(gate)) * up
        y_ref[...] = jnp.dot(act.astype(BF16), wdb[...], preferred_element_type=F32)

    @pl.when(i >= n_tiles)
    def _():
        y_ref[...] = jnp.zeros_like(y_ref)


def _experts(x_sorted, meta, w_gate, w_up, w_down):
    d = x_sorted.shape[1]
    f = w_gate.shape[2]
    tm = MOE_TM
    max_tiles = x_sorted.shape[0] // tm
    any_spec = pl.BlockSpec(memory_space=pl.ANY)
    grid_spec = pltpu.PrefetchScalarGridSpec(
        num_scalar_prefetch=1,
        grid=(max_tiles,),
        in_specs=[pl.BlockSpec((tm, d), lambda i, meta: (jnp.minimum(i, _meta(meta, META_N_TILES) - 1), 0)),
                  any_spec, any_spec, any_spec],
        out_specs=pl.BlockSpec((tm, d), lambda i, meta: (i, 0)),
        scratch_shapes=[pltpu.VMEM((d, f), F32), pltpu.VMEM((d, f), F32), pltpu.VMEM((f, d), F32),
                        pltpu.VMEM((d, f), BF16), pltpu.VMEM((d, f), BF16), pltpu.VMEM((f, d), BF16),
                        pltpu.SemaphoreType.DMA((3,))],
    )
    return pl.pallas_call(
        _experts_kernel,
        grid_spec=grid_spec,
        out_shape=jax.ShapeDtypeStruct((max_tiles * tm, d), F32),
        compiler_params=_params("arbitrary"),
        name="experts",
    )(meta, x_sorted, w_gate, w_up, w_down)


def _combine_kernel(pos_ref, x_ref, cw_ref, y_hbm, nw_ref, o_ref, ybuf, sem, *, final_norm):
    i = pl.program_id(0)
    tm = x_ref.shape[0]
    n_tok = tm * pl.num_programs(0)
    slot = lax.rem(i, 2)

    def start_gather(tile, dst_slot, unroll):
        base = tile * tm

        def start(r, c):
            for k in range(2):
                p = pos_ref[k * n_tok + base + r]
                pltpu.make_async_copy(y_hbm.at[pl.ds(p, 1), :], ybuf.at[dst_slot, k, pl.ds(r, 1), :],
                                      sem.at[dst_slot]).start()
            return c

        lax.fori_loop(0, tm, start, 0, unroll=unroll)

    @pl.when(i == 0)
    def _():
        start_gather(0, 0, unroll=8)

    @pl.when(i + 1 < pl.num_programs(0))
    def _():
        start_gather(i + 1, 1 - slot, unroll=True)

    for k in range(2):
        pltpu.make_async_copy(y_hbm.at[pl.ds(0, tm), :], ybuf.at[slot, k], sem.at[slot]).wait()
    cw = cw_ref[...]
    x = x_ref[...] + (cw[:, 0:1] * ybuf[slot, 0] + cw[:, 1:2] * ybuf[slot, 1])
    if final_norm:
        ms = jnp.mean(x * x, axis=-1, keepdims=True)
        x = (x * lax.rsqrt(ms + RMS_EPS)) * nw_ref[...]
    o_ref[...] = x


def _combine(x1, cw, y_sorted, pos, norm_w, final_norm):
    t, d = x1.shape
    tm = min(COMBINE_TM, t)
    grid_spec = pltpu.PrefetchScalarGridSpec(
        num_scalar_prefetch=1,
        grid=(t // tm,),
        in_specs=[pl.BlockSpec((tm, d), lambda i, pos: (i, 0)),
                  pl.BlockSpec((tm, 2), lambda i, pos: (i, 0)),
                  pl.BlockSpec(memory_space=pl.ANY),
                  pl.BlockSpec((1, d), lambda i, pos: (0, 0))],
        out_specs=pl.BlockSpec((tm, d), lambda i, pos: (i, 0)),
        scratch_shapes=[pltpu.VMEM((2, 2, tm, d), F32), pltpu.SemaphoreType.DMA((2,))],
    )
    return pl.pallas_call(
        functools.partial(_combine_kernel, final_norm=final_norm),
        grid_spec=grid_spec,
        out_shape=jax.ShapeDtypeStruct((t, d), F32),
        compiler_params=_params("arbitrary"),
        name="combine",
    )(pos, x1, cw, y_sorted, norm_w.reshape(1, d))


def _rope_tables(s):
    half = HEAD_DIM_RET_QK // 2
    inv = ROPE_BASE ** (-np.arange(half, dtype=np.float64) / half)
    ang = np.arange(s, dtype=np.float64)[:, None] * inv[None, :]
    cos, sin = np.cos(ang), np.sin(ang)
    return (jnp.asarray(np.concatenate([cos, cos], axis=-1), F32),
            jnp.asarray(np.concatenate([-sin, sin], axis=-1), F32))


META_TILE_SEQ, META_ACTIVE, META_N_ACTIVE, META_N_TILES = 0, 1, 2, 3
META_ROWS = 8
PLAN_LANES = V7X_LANES


def _plan_kernel(route_ref, pfx_ref, tri_ref, pos_ref, meta_ref, *, tm):
    ne, lanes = N_EXPERTS, PLAN_LANES
    t = route_ref.shape[1]
    nblk = t // lanes
    ids = lax.broadcasted_iota(jnp.int32, (ne, lanes), 0).astype(F32)

    def onehots(blk):
        cols = slice(blk * lanes, (blk + 1) * lanes)
        return [(ids == route_ref[2 + k:3 + k, cols]).astype(F32) for k in range(2)]

    stacked = jnp.concatenate([sum(onehots(blk)) for blk in range(nblk)], axis=0)
    prod = jnp.dot(stacked.astype(BF16), pfx_ref[...], preferred_element_type=F32)
    carry = jnp.zeros((ne, lanes), F32)
    before = []
    for blk in range(nblk):
        rows = slice(blk * ne, (blk + 1) * ne)
        before.append(prod[rows, :lanes] + carry)
        carry = carry + prod[rows, lanes:]
    counts = carry
    tiles_per = jnp.floor((counts + (tm - 1.0)) * (1.0 / tm))
    tri = tri_ref[...]
    tile_end = jnp.dot(tri, tiles_per.astype(BF16), preferred_element_type=F32)
    row_start = (tile_end - tiles_per) * tm
    for k in range(2):
        pieces = []
        for blk in range(nblk):
            oh = onehots(blk)[k]
            pieces.append(jnp.sum(oh * (before[blk] + row_start), axis=0, keepdims=True))
        pos_ref[k:k + 1, :] = jnp.concatenate(pieces, axis=1).astype(jnp.int32)

    owns = (tiles_per > 0.0).astype(F32)
    seq = jnp.dot(tri, owns.astype(BF16), preferred_element_type=F32) - 1.0
    n_active = seq[ne - 1:ne, :] + 1.0
    n_tiles = tile_end[ne - 1:ne, :]
    lane = lax.broadcasted_iota(jnp.int32, (ne, lanes), 1).astype(F32)
    tile = jnp.minimum(lane, n_tiles - 1.0)
    tile_seq = jnp.sum(owns * (tile_end <= tile).astype(F32), axis=0, keepdims=True)
    active = jnp.sum(jnp.where((owns > 0.0) & (seq == lane), ids, 0.0), axis=0, keepdims=True)
    zero = jnp.zeros((1, lanes), F32)
    rows = {META_TILE_SEQ: tile_seq, META_ACTIVE: active, META_N_ACTIVE: n_active, META_N_TILES: n_tiles}
    meta_ref[...] = jnp.concatenate([rows.get(r, zero) for r in range(META_ROWS)], axis=0).astype(jnp.int32)


def _plan(route_t, tm, max_tiles):
    t = route_t.shape[1]
    assert max_tiles <= PLAN_LANES and N_EXPERTS <= PLAN_LANES and t % PLAN_LANES == 0
    r = jnp.arange(PLAN_LANES)
    prefix = jnp.concatenate([(r[:, None] < r[None, :]).astype(BF16),
                              jnp.ones((PLAN_LANES, PLAN_LANES), BF16)], axis=1)
    e = jnp.arange(N_EXPERTS)
    tri = (e[None, :] <= e[:, None]).astype(BF16)
    pos, meta = pl.pallas_call(
        functools.partial(_plan_kernel, tm=tm),
        out_shape=[jax.ShapeDtypeStruct((2, t), jnp.int32),
                   jax.ShapeDtypeStruct((META_ROWS, PLAN_LANES), jnp.int32)],
        compiler_params=pltpu.CompilerParams(vmem_limit_bytes=V7X_VMEM_LIMIT_BYTES),
        name="plan",
    )(route_t, prefix, tri)
    return pos.reshape(2 * t), meta.reshape(META_ROWS * PLAN_LANES)


def kernel(x, norm_mix_w, w_in, ret_gn_w, w_branch_moba, w_branch_ret, w_out, norm_ffn_w, w_router_group, b_router_group, w_router_expert, b_router_expert, w_expert_gate, w_expert_up, w_expert_down, norm_final_w):
    b, s, d = x.shape
    t = b * s
    depth = w_in.shape[0]
    assert s % MOBA_BLOCK == 0 and s % RET_CHUNK == 0 and t % MOE_TM == 0
    cos_t, sin_t = _rope_tables(s)
    log_gamma = jnp.log(1.0 - 2.0 ** (-5.0 - jnp.arange(N_HEADS_RET, dtype=F32)))
    x2 = x.reshape(t, d)
    for l in range(depth):
        max_tiles = 2 * t // MOE_TM + N_EXPERTS
        proj, zeros = _inproj(x2, norm_mix_w[l], w_in[l], max_tiles * MOE_TM)
        proj3 = proj.reshape(b, s, proj.shape[1])
        o_a = _moba(proj3).reshape(t, -1)
        y_r = _retention(proj3, cos_t, sin_t, ret_gn_w[l], log_gamma).reshape(t, -1)
        mix = _merge(o_a, y_r, proj, w_branch_moba[l], w_branch_ret[l])
        w_router_t, b_router_t = _router_params(w_router_group[l], b_router_group[l],
                                                w_router_expert[l], b_router_expert[l])
        x1, h, route_t = _outproj(x2, mix, w_out[l].astype(BF16), norm_ffn_w[l], w_router_t, b_router_t)
        pos, meta = _plan(route_t, MOE_TM, max_tiles)
        x_sorted = _dispatch(h, pos, zeros)
        y_sorted = _experts(x_sorted, meta, w_expert_gate[l], w_expert_up[l], w_expert_down[l])
        x2 = _combine(x1, route_t[0:2].T, y_sorted, pos, norm_final_w, final_norm=(l == depth - 1))
    return x2.reshape(b, s, d)
```
